```python
import jax, jax.numpy as jnp
from jax import lax
import numpy as np

D_MODEL = 2048
BATCH = 1
SEQ = 16384
DEPTH = 1
DEC_BATCH = 1
DEC_SEQ = 8192
PAST_LEN = 128

HEAD_DIM = 128
ATTN_HEADS = 8
ATTN_KV_HEADS = 2
HG_HEADS = 4
HG_DK = 128
HG_DV = 128
MEM_HEADS = 4
MEM_LEN = 256
GRID_W = 64
ROPE_THETA = 10000.0
ROPE_PAIRS = HEAD_DIM // 4
Q_BLOCK = 128
HG_CHUNK = 64
N_EXPERTS = 16
EXPERT_CAPACITY_FACTOR = 2
D_EXPERT = 2048
EPS = 1e-6
IN_SPLITS = (ATTN_HEADS * HEAD_DIM,
             ATTN_KV_HEADS * HEAD_DIM,
             ATTN_KV_HEADS * HEAD_DIM,
             HG_HEADS * HG_DK,
             HG_HEADS * HG_DK,
             HG_HEADS * HG_DK,
             HG_HEADS * HG_DV,
             HG_HEADS * HG_DV,
             MEM_HEADS * HEAD_DIM)
D_IN = sum(IN_SPLITS)
SPLIT_POINTS = tuple(int(v) for v in np.cumsum(IN_SPLITS)[:-1])
MIX_WIDTH = ATTN_HEADS * HEAD_DIM + HG_HEADS * HG_DV + MEM_HEADS * HEAD_DIM

kernel_name = "hymba_gqa_hgrn2_memxattn_ecmoe_bidir"


def rmsnorm(x, g):
    xf = x.astype(jnp.float32)
    y = xf * lax.rsqrt(jnp.mean(xf * xf, axis=-1, keepdims=True) + EPS)
    return (y * g.astype(jnp.float32)).astype(x.dtype)


def axial_rope_tables(seq_len, dtype):
    n_rows = seq_len // GRID_W
    row = jnp.repeat(jnp.arange(n_rows), GRID_W).astype(jnp.float32)
    col = jnp.tile(jnp.arange(GRID_W), n_rows).astype(jnp.float32)
    inv = ROPE_THETA ** (-jnp.arange(ROPE_PAIRS, dtype=jnp.float32) / ROPE_PAIRS)
    ang = jnp.stack([row[:, None] * inv, col[:, None] * inv], axis=1)
    ang = jnp.broadcast_to(ang[:, :, None, :], (seq_len, 2, 2, ROPE_PAIRS)).reshape(seq_len, HEAD_DIM)
    return jnp.cos(ang).astype(dtype), jnp.sin(ang).astype(dtype)


def apply_rope(x, cos, sin):
    x4 = x.reshape(x.shape[:-1] + (2, 2, ROPE_PAIRS))
    rot = jnp.concatenate([-x4[..., 1:, :], x4[..., :1, :]], axis=-2).reshape(x.shape)
    return x * cos[None, :, None, :] + rot * sin[None, :, None, :]


def grid_attention(q, k, v):
    B, S, Hq, Dh = q.shape
    G = Hq // ATTN_KV_HEADS
    nb = S // Q_BLOCK
    qb = q.reshape(B, nb, Q_BLOCK, ATTN_KV_HEADS, G, Dh).transpose(1, 0, 2, 3, 4, 5)
    scale = Dh ** -0.5

    def block(qi):
        s = jnp.einsum('bqhgd,bshd->bhgqs', qi, k).astype(jnp.float32) * scale
        p = jax.nn.softmax(s, axis=-1).astype(v.dtype)
        return jnp.einsum('bhgqs,bshd->bqhgd', p, v)

    o = lax.map(block, qb)
    return o.transpose(1, 0, 2, 3, 4, 5).reshape(B, S, Hq * Dh)


def hgrn2_direction(q, k, v, logf):
    B, S, H, dk = q.shape
    dv = v.shape[-1]
    nc = S // HG_CHUNK

    def to_chunks(a):
        return a.reshape(B, nc, HG_CHUNK, H, a.shape[-1]).transpose(1, 0, 3, 2, 4)

    lower = jnp.tril(jnp.ones((HG_CHUNK, HG_CHUNK), dtype=bool))

    def step(state, inp):
        qc, kc, vc, gc = inp
        b = jnp.cumsum(gc, axis=-2)
        inter = jnp.einsum('bhtk,bhkv->bhtv', qc * jnp.exp(b), state)
        diff = b[:, :, :, None, :] - b[:, :, None, :, :]
        decay = jnp.exp(jnp.where(lower[:, :, None], diff, -jnp.inf))
        att = jnp.einsum('bhtk,bhtsk,bhsk->bhts', qc, decay, kc)
        intra = jnp.einsum('bhts,bhsv->bhtv', att, vc)
        b_end = b[:, :, -1:, :]
        new_state = jnp.exp(b_end[:, :, 0, :])[..., None] * state + jnp.einsum(
            'bhsk,bhsv->bhkv', kc * jnp.exp(b_end - b), vc)
        return new_state, inter + intra

    s0 = jnp.zeros((B, H, dk, dv), jnp.float32)
    _, o = lax.scan(step, s0, (to_chunks(q), to_chunks(k), to_chunks(v), to_chunks(logf)))
    return o.transpose(1, 0, 3, 2, 4).reshape(B, S, H, dv)


def memory_attention(qm, mem_n, w_mem_kv, g_mq, g_mk):
    B, S = qm.shape[:2]
    M = mem_n.shape[1]
    kv = (mem_n @ w_mem_kv).reshape(B, M, 2, MEM_HEADS, HEAD_DIM)
    km = rmsnorm(kv[:, :, 0], g_mk)
    vm = kv[:, :, 1]
    qm = rmsnorm(qm.reshape(B, S, MEM_HEADS, HEAD_DIM), g_mq)
    s = jnp.einsum('bqhd,bmhd->bhqm', qm, km).astype(jnp.float32) * (HEAD_DIM ** -0.5)
    p = jax.nn.softmax(s, axis=-1).astype(vm.dtype)
    return jnp.einsum('bhqm,bmhd->bqhd', p, vm).reshape(B, S, MEM_HEADS * HEAD_DIM)


def expert_choice_ffn(h, w_router, w_gate, w_up, w_down):
    B, S, D = h.shape
    N = B * S
    hf = h.reshape(N, D)
    aff = jax.nn.softmax((hf @ w_router).astype(jnp.float32), axis=-1)
    cap = EXPERT_CAPACITY_FACTOR * N // N_EXPERTS
    gate, idx = lax.top_k(aff.T, cap)
    xs = hf[idx]
    hid = jax.nn.silu(jnp.einsum('ecd,edf->ecf', xs, w_gate)) * jnp.einsum('ecd,edf->ecf', xs, w_up)
    out = jnp.einsum('ecf,efd->ecd', hid, w_down) * gate[..., None].astype(h.dtype)
    y = jnp.zeros_like(hf).at[idx.reshape(-1)].add(out.reshape(N_EXPERTS * cap, D))
    return y.reshape(B, S, D)


def hybrid_layer(x, mem, cos, sin, g_mix, w_in, g_q, g_k, g_hg_out, lb_f, lb_b, g_mem, w_mem_kv,
                 g_mq, g_mk, w_out, g_ffn, w_router, w_gate, w_up, w_down):
    B, S, _ = x.shape
    f32 = jnp.float32
    h = rmsnorm(x, g_mix)
    aq, ak, av, hq, zf, zb, hi, hg, mq = jnp.split(h @ w_in, SPLIT_POINTS, axis=-1)

    aq = apply_rope(rmsnorm(aq.reshape(B, S, ATTN_HEADS, HEAD_DIM), g_q), cos, sin)
    ak = apply_rope(rmsnorm(ak.reshape(B, S, ATTN_KV_HEADS, HEAD_DIM), g_k), cos, sin)
    av = av.reshape(B, S, ATTN_KV_HEADS, HEAD_DIM)
    attn = grid_attention(aq, ak, av)

    q_h = jax.nn.silu(hq.astype(f32)).reshape(B, S, HG_HEADS, HG_DK)
    v_h = hi.astype(f32).reshape(B, S, HG_HEADS, HG_DV)

    def forget(z, lb):
        lb = lb.reshape(HG_HEADS, HG_DK)
        f = lb + (1.0 - lb) * jax.nn.sigmoid(z.astype(f32).reshape(B, S, HG_HEADS, HG_DK))
        return 1.0 - f, jnp.log(f)

    k_f, g_f = forget(zf, lb_f)
    k_b, g_b = forget(zb, lb_b)
    rev = lambda a: jnp.flip(a, axis=1)
    o = hgrn2_direction(q_h, k_f, v_h, g_f) + rev(hgrn2_direction(rev(q_h), rev(k_b), rev(v_h), rev(g_b)))
    o = rmsnorm(o, g_hg_out) * jax.nn.silu(hg.astype(f32).reshape(B, S, HG_HEADS, HG_DV))
    hgrn = o.reshape(B, S, HG_HEADS * HG_DV).astype(x.dtype)

    memo = memory_attention(mq, rmsnorm(mem, g_mem), w_mem_kv, g_mq, g_mk)

    x = x + jnp.concatenate([attn, hgrn, memo], axis=-1) @ w_out
    x = x + expert_choice_ffn(rmsnorm(x, g_ffn), w_router, w_gate, w_up, w_down)
    return x


def setup_inputs(seed: int = 0) -> dict:
    key = jax.random.key(seed)
    ks = jax.random.split(key, 24)
    nrm = lambda k, shape, scale: jax.random.normal(k, shape, jnp.float32) * scale
    gain = lambda k, shape: 1.0 + 0.02 * jax.random.normal(k, shape, jnp.float32)
    L, D = DEPTH, D_MODEL
    return {
        "x_prompt": nrm(ks[0], (BATCH, SEQ, D), 1.0),
        "x_sample": nrm(ks[1], (DEC_BATCH, DEC_SEQ, D), 1.0),
        "mem_prompt": nrm(ks[2], (BATCH, MEM_LEN, D), 1.0),
        "mem_sample": nrm(ks[3], (DEC_BATCH, MEM_LEN, D), 1.0),
        "g_mix": gain(ks[4], (L, D)),
        "w_in": nrm(ks[5], (L, D, D_IN), D ** -0.5),
        "g_q": gain(ks[6], (L, HEAD_DIM)),
        "g_k": gain(ks[7], (L, HEAD_DIM)),
        "g_hg_out": gain(ks[8], (L, HG_DV)),
        "lb_fwd": nrm(ks[9], (L + 1, HG_HEADS * HG_DK), 1.0),
        "lb_bwd": nrm(ks[10], (L + 1, HG_HEADS * HG_DK), 1.0),
        "g_mem": gain(ks[11], (L, D)),
        "w_mem_kv": nrm(ks[12], (L, D, 2 * MEM_HEADS * HEAD_DIM), D ** -0.5),
        "g_mq": gain(ks[13], (L, HEAD_DIM)),
        "g_mk": gain(ks[14], (L, HEAD_DIM)),
        "w_out": nrm(ks[15], (L, MIX_WIDTH, D), MIX_WIDTH ** -0.5),
        "g_ffn": gain(ks[16], (L, D)),
        "w_router": nrm(ks[17], (L, D, N_EXPERTS), D ** -0.5),
        "w_gate": nrm(ks[18], (L, N_EXPERTS, D, D_EXPERT), D ** -0.5),
        "w_up": nrm(ks[19], (L, N_EXPERTS, D, D_EXPERT), D ** -0.5),
        "w_down": nrm(ks[20], (L, N_EXPERTS, D_EXPERT, D), D_EXPERT ** -0.5),
    }


def reference(x_prompt, x_sample, mem_prompt, mem_sample, g_mix, w_in, g_q, g_k, g_hg_out, lb_fwd, lb_bwd,
              g_mem, w_mem_kv, g_mq, g_mk, w_out, g_ffn, w_router, w_gate, w_up, w_down):
    lb_f_all = jnp.cumsum(jax.nn.softmax(lb_fwd.astype(jnp.float32), axis=0), axis=0)
    lb_b_all = jnp.cumsum(jax.nn.softmax(lb_bwd.astype(jnp.float32), axis=0), axis=0)
    cos_p, sin_p = axial_rope_tables(x_prompt.shape[1], x_prompt.dtype)
    cos_s, sin_s = axial_rope_tables(x_sample.shape[1], x_sample.dtype)
    y_prompt, y_sample = x_prompt, x_sample
    for l in range(DEPTH):
        params = (g_mix[l], w_in[l], g_q[l], g_k[l], g_hg_out[l], lb_f_all[l], lb_b_all[l], g_mem[l],
                  w_mem_kv[l], g_mq[l], g_mk[l], w_out[l], g_ffn[l], w_router[l], w_gate[l], w_up[l], w_down[l])
        y_prompt = hybrid_layer(y_prompt, mem_prompt, cos_p, sin_p, *params)
        y_sample = hybrid_layer(y_sample, mem_sample, cos_s, sin_s, *params)
    return (y_prompt, y_sample)
```

```python
import functools

import jax
import jax.numpy as jnp
from jax import lax
from jax.experimental import pallas as pl
from jax.experimental.pallas import tpu as pltpu

F32 = jnp.float32
BF16 = jnp.bfloat16
I32 = jnp.int32

D_MODEL = 2048
HEAD_DIM = 128
ATTN_HEADS = 8
ATTN_KV_HEADS = 2
KV_GROUP = ATTN_HEADS // ATTN_KV_HEADS
HG_HEADS = 4
HG_WIDTH = HG_HEADS * HEAD_DIM
MEM_HEADS = 4
GRID_W = 64
ROPE_THETA = 10000.0
ROPE_PAIRS = HEAD_DIM // 4
N_EXPERTS = 16
EXPERT_CAPACITY_FACTOR = 2
D_EXPERT = 2048
EPS = 1e-6
HG_CHUNK = 128
LANES = 128
NEG_BIG = -1e30

C_Q, C_K, C_V, C_HQ, C_ZF, C_ZB, C_HI, C_HG, C_MQ, C_END = 0, 1024, 1280, 1536, 2048, 2560, 3072, 3584, 4096, 4608

VMEM_LIMIT = 56 * 1024 * 1024


def _cparams(sem, vmem=VMEM_LIMIT):
    return pltpu.CompilerParams(dimension_semantics=sem, vmem_limit_bytes=vmem)


def _nt(a, b):
    return lax.dot_general(a, b, (((1,), (1,)), ((), ())), preferred_element_type=F32)


def _nn(a, b):
    return jnp.dot(a, b, preferred_element_type=F32)


def _ind(mask):
    return jnp.where(mask, 1.0, 0.0).astype(BF16)


def _rms(x, g):
    return x * lax.rsqrt(jnp.mean(x * x, axis=-1, keepdims=True) + EPS) * g


def _memkv_kernel(mem_ref, gmem_ref, w_ref, gmk_ref, km_ref, vm_ref):
    h = _rms(mem_ref[...], gmem_ref[...]).astype(BF16)
    kv = _nn(h, w_ref[...])
    for hh in range(MEM_HEADS):
        sl = slice(hh * HEAD_DIM, (hh + 1) * HEAD_DIM)
        km_ref[:, sl] = _rms(kv[:, sl], gmk_ref[...]).astype(BF16)
    vm_ref[...] = kv[:, MEM_HEADS * HEAD_DIM:].astype(BF16)


def _mem_kv(mem, g_mem, w_mem_kv_bf, g_mk):
    m = mem.shape[0]
    width = MEM_HEADS * HEAD_DIM
    return pl.pallas_call(
        _memkv_kernel,
        out_shape=(jax.ShapeDtypeStruct((m, width), BF16), jax.ShapeDtypeStruct((m, width), BF16)),
        name="mem_kv",
        compiler_params=_cparams(None),
    )(mem, g_mem, w_mem_kv_bf, g_mk)


def _norm_rope(a, g, cos, sin_signed, first_half):
    y = _rms(a, g)
    rot = jnp.where(first_half, pltpu.roll(y, HEAD_DIM - ROPE_PAIRS, 1), pltpu.roll(y, ROPE_PAIRS, 1))
    return y * cos + rot * sin_signed


def _inproj_kernel(x_ref, gmix_ref, w_ref, wvt_ref, wvit_ref, cos_ref, sin_ref, gq_ref, gk_ref, gmq_ref, lbf_ref, lbb_ref,
                   q_ref, k_ref, vt_ref, hq_ref, kf_ref, gf_ref, kb_ref, gb_ref, vi_ref, vit_ref, og_ref, mq_ref, h_scr):
    tm = x_ref.shape[0]
    h_scr[...] = _rms(x_ref[...], gmix_ref[...]).astype(BF16)
    h = h_scr[...]
    cos = cos_ref[...]
    sin = sin_ref[...]
    lane = lax.broadcasted_iota(I32, (tm, HEAD_DIM), 1)
    first_half = (lane % (2 * ROPE_PAIRS)) < ROPE_PAIRS
    scale = HEAD_DIM ** -0.5

    def proj(c0, c1):
        return _nn(h, w_ref[:, c0:c1])

    for c in range(ATTN_HEADS // 4):
        a = proj(C_Q + 512 * c, C_Q + 512 * (c + 1))
        for hh in range(4):
            sl = slice(hh * HEAD_DIM, (hh + 1) * HEAD_DIM)
            y = _norm_rope(a[:, sl], gq_ref[...], cos, sin, first_half) * scale
            q_ref[:, 512 * c + hh * HEAD_DIM:512 * c + (hh + 1) * HEAD_DIM] = y.astype(BF16)
    a = proj(C_K, C_V)
    for hh in range(ATTN_KV_HEADS):
        sl = slice(hh * HEAD_DIM, (hh + 1) * HEAD_DIM)
        k_ref[:, sl] = _norm_rope(a[:, sl], gk_ref[...], cos, sin, first_half).astype(BF16)
    vt_ref[...] = _nt(wvt_ref[...], h).astype(BF16)

    hq_ref[...] = jax.nn.silu(proj(C_HQ, C_ZF)).astype(BF16)

    def forget(z, lb_ref, k_out, g_out):
        a2 = lb_ref[...]
        e2 = jnp.exp(a2 - jnp.max(a2, axis=0, keepdims=True))
        lb = e2[0:1, :] / jnp.sum(e2, axis=0, keepdims=True)
        f = lb + (1.0 - lb) * jax.nn.sigmoid(z)
        g_out[...] = jnp.log(f)
        k_out[...] = ((1.0 - lb) * jax.nn.sigmoid(-z)).astype(BF16)

    forget(proj(C_ZF, C_ZB), lbf_ref, kf_ref, gf_ref)
    forget(proj(C_ZB, C_HI), lbb_ref, kb_ref, gb_ref)
    vi_ref[...] = proj(C_HI, C_HG).astype(BF16)
    vit_ref[...] = _nt(wvit_ref[...], h).astype(BF16)
    og_ref[...] = jax.nn.silu(proj(C_HG, C_MQ)).astype(BF16)
    a = proj(C_MQ, C_END)
    for hh in range(MEM_HEADS):
        sl = slice(hh * HEAD_DIM, (hh + 1) * HEAD_DIM)
        mq_ref[:, sl] = (_rms(a[:, sl], gmq_ref[...]) * scale).astype(BF16)


def _in_proj(x, g_mix, w_bf, wvt, wvit, cos, sin_signed, g_q, g_k, g_mq, lb_f, lb_b, tm):
    n = x.shape[0]
    grid = (n // tm,)
    row = lambda w: pl.BlockSpec((tm, w), lambda i: (i, 0))
    col = lambda h: pl.BlockSpec((h, tm), lambda i: (0, i))
    full = lambda a: pl.BlockSpec(a.shape, lambda i: (0,) * a.ndim)
    res = lambda a: pl.BlockSpec(a.shape, lambda i: (0,) * a.ndim, pipeline_mode=pl.Buffered(1))
    sd = jax.ShapeDtypeStruct
    out_shape = (sd((n, 1024), BF16), sd((n, 256), BF16), sd((256, n), BF16),
                 sd((n, 512), BF16), sd((n, 512), BF16), sd((n, 512), F32), sd((n, 512), BF16), sd((n, 512), F32),
                 sd((n, 512), BF16), sd((512, n), BF16), sd((n, 512), BF16), sd((n, 512), BF16))
    out_specs = (row(1024), row(256), col(256), row(512), row(512), row(512), row(512), row(512),
                 row(512), col(512), row(512), row(512))
    return pl.pallas_call(
        _inproj_kernel,
        grid=grid,
        in_specs=[row(D_MODEL), full(g_mix), res(w_bf), res(wvt), res(wvit), row(HEAD_DIM), row(HEAD_DIM),
                  full(g_q), full(g_k), full(g_mq), full(lb_f), full(lb_b)],
        out_specs=out_specs,
        out_shape=out_shape,
        scratch_shapes=[pltpu.VMEM((tm, D_MODEL), BF16)],
        name="in_proj",
        compiler_params=_cparams(("parallel",)),
    )(x, g_mix, w_bf, wvt, wvit, cos, sin_signed, g_q, g_k, g_mq, lb_f, lb_b)


def _attn_kernel(q_ref, k_ref, vt_ref, o_ref, *, tk, nk):
    tq = q_ref.shape[0]
    for hh in range(KV_GROUP):
        sl = slice(hh * HEAD_DIM, (hh + 1) * HEAD_DIM)
        qh = q_ref[:, sl]

        def body(i, carry):
            m, l, acc = carry
            off = pl.multiple_of(i * tk, tk)
            s = _nt(k_ref[pl.ds(off, tk), :], qh)
            m_new = jnp.maximum(m, jnp.max(s, axis=0, keepdims=True))
            p = jnp.exp(s - m_new)
            alpha = jnp.exp(m - m_new)
            l = alpha * l + jnp.sum(p, axis=0, keepdims=True)
            acc = alpha * acc + _nn(vt_ref[:, pl.ds(off, tk)], p.astype(BF16))
            return m_new, l, acc

        init = (jnp.full((1, tq), NEG_BIG, F32), jnp.zeros((1, tq), F32), jnp.zeros((HEAD_DIM, tq), F32))
        m, l, acc = lax.fori_loop(0, nk, body, init)
        o_ref[:, sl] = (acc / l).T.astype(BF16)


def _attention(q, k, vt, tq, tk):
    s = q.shape[0]
    width = KV_GROUP * HEAD_DIM
    return pl.pallas_call(
        functools.partial(_attn_kernel, tk=tk, nk=s // tk),
        grid=(ATTN_KV_HEADS, s // tq),
        in_specs=[pl.BlockSpec((tq, width), lambda g, i: (i, g)),
                  pl.BlockSpec((s, HEAD_DIM), lambda g, i: (0, g)),
                  pl.BlockSpec((HEAD_DIM, s), lambda g, i: (g, 0))],
        out_specs=pl.BlockSpec((tq, width), lambda g, i: (i, g)),
        out_shape=jax.ShapeDtypeStruct((s, ATTN_HEADS * HEAD_DIM), BF16),
        name="attention",
        compiler_params=_cparams(("parallel", "parallel")),
    )(q, k, vt)


def _hgrn_levels():
    b, out = HG_CHUNK, []
    while b >= 2:
        out.append(b)
        b //= 2
    return out


def _hgrn_chunk(q, k, g, v, vt, state_ref, rev):
    c = HG_CHUNK
    w = HG_WIDTH
    row = lax.broadcasted_iota(I32, (c, w), 0)
    b = g
    sh = 1
    while sh < c:
        if rev:
            b = b + jnp.where(row < c - sh, pltpu.roll(b, c - sh, 0), 0.0)
        else:
            b = b + jnp.where(row >= sh, pltpu.roll(b, sh, 0), 0.0)
        sh *= 2
    qf = q.astype(F32)
    kf = k.astype(F32)
    ti = lax.broadcasted_iota(I32, (c, c), 0)
    si = lax.broadcasted_iota(I32, (c, c), 1)
    att = [jnp.where(ti == si, _nt(q[:, h * HEAD_DIM:(h + 1) * HEAD_DIM], k[:, h * HEAD_DIM:(h + 1) * HEAD_DIM]), 0.0)
           for h in range(HG_HEADS)]
    for blk in _hgrn_levels():
        half = blk // 2
        ref_row = half if rev else half - 1
        pos = row % blk
        if blk >= 8:
            r = jnp.concatenate(
                [jnp.broadcast_to(b[s0 + ref_row:s0 + ref_row + 1, :], (blk, w)) for s0 in range(0, c, blk)], axis=0)
        else:
            r = b
            for d in range(-ref_row, blk - ref_row):
                if d != 0:
                    r = jnp.where(pos - ref_row == d, pltpu.roll(b, d % c, 0), r)
        e = jnp.exp(-jnp.abs(b - r))
        is_q = (pos < half) if rev else (pos >= half)
        ql = jnp.where(is_q, qf * e, 0.0).astype(BF16)
        kl = jnp.where(is_q, 0.0, kf * e).astype(BF16)
        same = (ti // blk) == (si // blk)
        for h in range(HG_HEADS):
            sl = slice(h * HEAD_DIM, (h + 1) * HEAD_DIM)
            att[h] = att[h] + jnp.where(same, _nt(ql[:, sl], kl[:, sl]), 0.0)
    b_end = b[0:1, :] if rev else b[c - 1:c, :]
    qd = (qf * jnp.exp(b)).astype(BF16)
    kd = (kf * jnp.exp(b_end - b)).astype(BF16)
    dec = jnp.exp(b_end)
    outs = []
    for h in range(HG_HEADS):
        sl = slice(h * HEAD_DIM, (h + 1) * HEAD_DIM)
        st = state_ref[h]
        o = _nn(att[h].astype(BF16), v[:, sl]) + _nt(qd[:, sl], st.astype(BF16))
        state_ref[h] = dec[:, sl] * st + _nn(vt[sl, :], kd[:, sl])
        outs.append(o)
    return jnp.concatenate(outs, axis=1)


def _hgrn_fwd_kernel(q_ref, k_ref, g_ref, v_ref, vt_ref, o_ref, state_ref, *, nch):
    @pl.when(pl.program_id(0) == 0)
    def _():
        state_ref[...] = jnp.zeros_like(state_ref)

    def body(i, carry):
        r0 = pl.multiple_of(i * HG_CHUNK, HG_CHUNK)
        rs = pl.ds(r0, HG_CHUNK)
        o_ref[rs, :] = _hgrn_chunk(q_ref[rs, :], k_ref[rs, :], g_ref[rs, :], v_ref[rs, :], vt_ref[:, rs], state_ref, False)
        return carry

    lax.fori_loop(0, nch, body, 0)


def _hgrn_bwd_kernel(q_ref, k_ref, g_ref, v_ref, vt_ref, of_ref, og_ref, gout_ref, o_ref, state_ref, *, nch):
    @pl.when(pl.program_id(0) == 0)
    def _():
        state_ref[...] = jnp.zeros_like(state_ref)

    def body(i, carry):
        r0 = pl.multiple_of((nch - 1 - i) * HG_CHUNK, HG_CHUNK)
        rs = pl.ds(r0, HG_CHUNK)
        o = _hgrn_chunk(q_ref[rs, :], k_ref[rs, :], g_ref[rs, :], v_ref[rs, :], vt_ref[:, rs], state_ref, True)
        o = o + of_ref[rs, :]
        og = og_ref[rs, :].astype(F32)
        for h in range(HG_HEADS):
            sl = slice(h * HEAD_DIM, (h + 1) * HEAD_DIM)
            o_ref[rs, sl] = (_rms(o[:, sl], gout_ref[...]) * og[:, sl]).astype(BF16)
        return carry

    lax.fori_loop(0, nch, body, 0)


def _hgrn(hq, kf, gf, kb, gb, vi, vit, og, g_out, tb):
    n = hq.shape[0]
    nblk = n // tb
    nch = tb // HG_CHUNK
    state = pltpu.VMEM((HG_HEADS, HEAD_DIM, HEAD_DIM), F32)
    fr = lambda i: (i, 0)
    fc = lambda i: (0, i)
    o_f = pl.pallas_call(
        functools.partial(_hgrn_fwd_kernel, nch=nch),
        grid=(nblk,),
        in_specs=[pl.BlockSpec((tb, HG_WIDTH), fr)] * 4 + [pl.BlockSpec((HG_WIDTH, tb), fc)],
        out_specs=pl.BlockSpec((tb, HG_WIDTH), fr),
        out_shape=jax.ShapeDtypeStruct((n, HG_WIDTH), F32),
        scratch_shapes=[state],
        name="hgrn_fwd",
        compiler_params=_cparams(("arbitrary",)),
    )(hq, kf, gf, vi, vit)
    br = lambda i: (nblk - 1 - i, 0)
    bc = lambda i: (0, nblk - 1 - i)
    return pl.pallas_call(
        functools.partial(_hgrn_bwd_kernel, nch=nch),
        grid=(nblk,),
        in_specs=[pl.BlockSpec((tb, HG_WIDTH), br)] * 4 + [pl.BlockSpec((HG_WIDTH, tb), bc)]
        + [pl.BlockSpec((tb, HG_WIDTH), br)] * 2 + [pl.BlockSpec(g_out.shape, lambda i: (0, 0))],
        out_specs=pl.BlockSpec((tb, HG_WIDTH), br),
        out_shape=jax.ShapeDtypeStruct((n, HG_WIDTH), BF16),
        scratch_shapes=[state],
        name="hgrn_bwd",
        compiler_params=_cparams(("arbitrary",)),
    )(hq, kb, gb, vi, vit, o_f, og, g_out)


def _outproj_kernel(x_ref, attn_ref, hgrn_ref, mq_ref, km_ref, vm_ref, w_ref, gffn_ref, wrt_ref,
                    y_ref, h2_ref, aff_ref):
    memo = []
    for hh in range(MEM_HEADS):
        sl = slice(hh * HEAD_DIM, (hh + 1) * HEAD_DIM)
        s = _nt(mq_ref[:, sl], km_ref[:, sl])
        p = jnp.exp(s - jnp.max(s, axis=-1, keepdims=True))
        p = p / jnp.sum(p, axis=-1, keepdims=True)
        memo.append(_nn(p.astype(BF16), vm_ref[:, sl]).astype(BF16))
    n_attn = ATTN_HEADS * HEAD_DIM
    y = x_ref[...] + _nn(attn_ref[...], w_ref[0:n_attn, :]) + _nn(hgrn_ref[...], w_ref[n_attn:n_attn + HG_WIDTH, :])
    for hh in range(MEM_HEADS):
        r0 = n_attn + HG_WIDTH + hh * HEAD_DIM
        y = y + _nn(memo[hh], w_ref[r0:r0 + HEAD_DIM, :])
    y_ref[...] = y
    h2 = _rms(y, gffn_ref[...])
    h2_ref[...] = h2.astype(BF16)
    logits = lax.dot_general(wrt_ref[...], h2, (((1,), (1,)), ((), ())), preferred_element_type=F32,
                             precision=lax.Precision.HIGHEST)
    e = jnp.exp(logits - jnp.max(logits, axis=0, keepdims=True))
    aff_ref[...] = e / jnp.sum(e, axis=0, keepdims=True)


def _out_proj(x, attn, hgrn, mq, km, vm, w_out_bf, g_ffn, w_router_t, tm):
    n = x.shape[0]
    row = lambda w: pl.BlockSpec((tm, w), lambda i: (i, 0))
    full = lambda a: pl.BlockSpec(a.shape, lambda i: (0,) * a.ndim)
    res = lambda a: pl.BlockSpec(a.shape, lambda i: (0,) * a.ndim, pipeline_mode=pl.Buffered(1))
    sd = jax.ShapeDtypeStruct
    return pl.pallas_call(
        _outproj_kernel,
        grid=(n // tm,),
        in_specs=[row(D_MODEL), row(1024), row(512), row(512), full(km), full(vm), res(w_out_bf), full(g_ffn),
                  full(w_router_t)],
        out_specs=(row(D_MODEL), row(D_MODEL), pl.BlockSpec((N_EXPERTS, tm), lambda i: (0, i))),
        out_shape=(sd((n, D_MODEL), F32), sd((n, D_MODEL), BF16), sd((N_EXPERTS, n), F32)),
        name="out_proj",
        compiler_params=_cparams(("parallel",)),
    )(x, attn, hgrn, mq, km, vm, w_out_bf, g_ffn, w_router_t)


def _split3(x):
    hi = x.astype(BF16)
    r1 = x - hi.astype(F32)
    mid = r1.astype(BF16)
    lo = (r1 - mid.astype(F32)).astype(BF16)
    return hi, mid, lo


def _topk_kernel(aff_ref, afft_ref, idx_ref, gate_ref, pos_ref, off_ref, cl_scr, offb_scr, totb_scr, *, cap):
    e_n, nb, _ = aff_ref.shape
    rows = e_n * nb
    aff = aff_ref[...]
    keys = lax.bitcast_convert_type(aff, I32)

    def count(mask):
        s = jnp.sum(mask.astype(F32), axis=1, keepdims=True)
        return jnp.sum(s, axis=2, keepdims=True)

    def bis(i, t):
        cand = t | (jnp.int32(1) << (30 - i))
        return jnp.where(count(keys >= cand) >= cap, cand, t)

    thr = lax.fori_loop(0, 31, bis, jnp.zeros((e_n, 1, 1), I32))
    gt = keys > thr
    eq = keys == thr
    need = cap - count(gt)

    upper = _ind(lax.broadcasted_iota(I32, (LANES, LANES), 0) <= lax.broadcasted_iota(I32, (LANES, LANES), 1))
    ones = jnp.ones((LANES, LANES), BF16)
    bi = lax.broadcasted_iota(I32, (nb, nb), 0)
    bj = lax.broadcasted_iota(I32, (nb, nb), 1)
    strict_lower = _ind(bj < bi)

    def prefix(mask):
        m2 = _ind(mask).reshape(rows, LANES)
        cl = _nn(m2, upper).reshape(e_n, nb, LANES)
        tot = _nn(m2, ones).reshape(e_n, nb, LANES)
        off = jnp.stack([_nn(strict_lower, tot[e].astype(BF16)) for e in range(e_n)], axis=0)
        return cl, off, tot

    cl, off, _ = prefix(eq)
    rank_eq = off + cl - eq.astype(F32)
    sel = gt | (eq & (rank_eq < need))
    cl, off, tot = prefix(sel)
    pos_ref[...] = jnp.where(sel, off + cl - 1.0, -1.0).astype(I32)
    off_ref[...] = off.astype(I32)
    cl_scr[...] = cl
    offb_scr[...] = off
    totb_scr[...] = tot

    lower_incl = _ind(lax.broadcasted_iota(I32, (LANES, LANES), 1) <= lax.broadcasted_iota(I32, (LANES, LANES), 0))
    reps = cap // LANES
    s_row = lax.broadcasted_iota(I32, (nb, cap), 1).astype(F32)
    b_col = lax.broadcasted_iota(I32, (nb, cap), 0).astype(F32)
    j_col = lax.broadcasted_iota(I32, (LANES, cap), 0).astype(F32)

    def per_expert(e, carry):
        sel_e = _ind(pos_ref[e] >= 0)
        clt = _nt(lower_incl, sel_e)
        offt = jnp.concatenate([offb_scr[e]] * reps, axis=1)
        endt = offt + jnp.concatenate([totb_scr[e]] * reps, axis=1)
        hit = (offt <= s_row) & (s_row < endt)
        onehot = _ind(hit)
        g_cnt = _nn(clt.astype(BF16), onehot)
        local = s_row[0:1, :] - jnp.sum(jnp.where(hit, offt, 0.0), axis=0, keepdims=True)
        j_row = jnp.sum((g_cnt <= local).astype(F32), axis=0, keepdims=True)
        b_row = jnp.sum(jnp.where(hit, b_col, 0.0), axis=0, keepdims=True)
        idx_ref[e] = (b_row * LANES + j_row).astype(I32)
        hi, mid, lo = _split3(afft_ref[e])
        g_aff = _nn(hi, onehot) + _nn(mid, onehot) + _nn(lo, onehot)
        gate_ref[e] = jnp.sum(jnp.where(j_col == j_row, g_aff, 0.0), axis=0, keepdims=True)
        return carry

    lax.fori_loop(0, e_n, per_expert, 0)


def _topk(aff_t, cap):
    e_n, n = aff_t.shape
    nb = n // LANES
    aff3 = aff_t.reshape(e_n, nb, LANES)
    afft3 = jnp.swapaxes(aff3, 1, 2)
    sd = jax.ShapeDtypeStruct
    return pl.pallas_call(
        functools.partial(_topk_kernel, cap=cap),
        out_shape=(sd((e_n, 1, cap), I32), sd((e_n, 1, cap), F32), sd((e_n, nb, LANES), I32), sd((e_n, nb, LANES), I32)),
        scratch_shapes=[pltpu.VMEM((e_n, nb, LANES), F32)] * 3,
        name="topk",
        compiler_params=_cparams(None),
    )(aff3, afft3)


GATHER_RING = 16


def _gather_kernel(idx_ref, src_ref, dst_ref, sem, *, cap):
    e = pl.program_id(0)

    def copy(s, k):
        return pltpu.make_async_copy(src_ref.at[idx_ref[0, 0, s]], dst_ref.at[e * cap + s], sem.at[k])

    def group(gi, carry):
        for k in range(GATHER_RING):
            s = gi * GATHER_RING + k

            @pl.when(gi > 0)
            def _():
                copy(s - GATHER_RING, k).wait()

            copy(s, k).start()
        return carry

    lax.fori_loop(0, cap // GATHER_RING, group, 0)
    for k in range(GATHER_RING):
        copy(cap - GATHER_RING + k, k).wait()


def _gather(h2, idx, cap):
    n = h2.shape[0]
    src = h2.reshape(n, D_MODEL // LANES, LANES)
    out = pl.pallas_call(
        functools.partial(_gather_kernel, cap=cap),
        grid=(N_EXPERTS,),
        in_specs=[pl.BlockSpec((1, 1, cap), lambda e: (e, 0, 0), memory_space=pltpu.SMEM),
                  pl.BlockSpec(memory_space=pl.ANY)],
        out_specs=pl.BlockSpec(memory_space=pl.ANY),
        out_shape=jax.ShapeDtypeStruct((N_EXPERTS * cap, D_MODEL // LANES, LANES), BF16),
        scratch_shapes=[pltpu.SemaphoreType.DMA((GATHER_RING,))],
        name="gather",
        compiler_params=_cparams(("arbitrary",)),
    )(idx, src)
    return out.reshape(N_EXPERTS, cap, D_MODEL)


def _ffn_kernel(x_ref, gate_ref, wg_ref, wu_ref, wd_ref, o_ref, acc_ref):
    f = pl.program_id(2)
    x = x_ref[0]
    hid = jax.nn.silu(_nn(x, wg_ref[0])) * _nn(x, wu_ref[0])
    part = _nn(hid.astype(BF16), wd_ref[0])

    @pl.when(f == 0)
    def _():
        acc_ref[...] = part

    @pl.when(f > 0)
    def _():
        acc_ref[...] += part

    @pl.when(f == pl.num_programs(2) - 1)
    def _():
        o_ref[0] = (acc_ref[...] * gate_ref[0]).astype(BF16)


def _ffn(xs, gate_col, wg, wu, wd, ts, tf):
    e_n, cap, d = xs.shape
    return pl.pallas_call(
        _ffn_kernel,
        grid=(e_n, cap // ts, D_EXPERT // tf),
        in_specs=[pl.BlockSpec((1, ts, d), lambda e, s, f: (e, s, 0)),
                  pl.BlockSpec((1, ts, 1), lambda e, s, f: (e, s, 0)),
                  pl.BlockSpec((1, d, tf), lambda e, s, f: (e, 0, f)),
                  pl.BlockSpec((1, d, tf), lambda e, s, f: (e, 0, f)),
                  pl.BlockSpec((1, tf, d), lambda e, s, f: (e, f, 0))],
        out_specs=pl.BlockSpec((1, ts, d), lambda e, s, f: (e, s, 0)),
        out_shape=jax.ShapeDtypeStruct((e_n, cap, d), BF16),
        scratch_shapes=[pltpu.VMEM((ts, d), F32)],
        name="ffn",
        compiler_params=_cparams(("parallel", "parallel", "arbitrary")),
    )(xs, gate_col, wg, wu, wd)


ROW_GROUP = 16
WIN_GROUPS = LANES // ROW_GROUP + 1
WIN = WIN_GROUPS * ROW_GROUP


def _combine_kernel(off_ref, y1_ref, post_ref, rows_ref, o_ref, buf, sem, *, cap):
    b = pl.program_id(0)
    nb = pl.num_programs(0)

    def first_group(blk, e):
        return jnp.minimum(off_ref[e, blk] // ROW_GROUP, cap // ROW_GROUP - WIN_GROUPS)

    def copy(blk, slot, e):
        return pltpu.make_async_copy(rows_ref.at[e, pl.ds(first_group(blk, e), WIN_GROUPS)],
                                     buf.at[slot, pl.ds(e * WIN_GROUPS, WIN_GROUPS)], sem.at[slot, e])

    def start(blk, slot):
        for e in range(N_EXPERTS):
            copy(blk, slot, e).start()

    def wait(blk, slot):
        for e in range(N_EXPERTS):
            copy(blk, slot, e).wait()

    slot = b % 2

    @pl.when(b == 0)
    def _():
        start(0, 0)

    @pl.when(b + 1 < nb)
    def _():
        start(b + 1, 1 - slot)

    post = post_ref[...]
    lane = lax.broadcasted_iota(I32, (LANES, LANES), 1)

    def rel(e):
        return jnp.broadcast_to(post[:, e:e + 1], (LANES, LANES)) - first_group(b, e) * ROW_GROUP

    pieces = []
    for t in range(N_EXPERTS * WIN // LANES):
        e0 = t * LANES // WIN
        miss = rel(e0) - (lane + (t * LANES - e0 * WIN))
        if (e0 + 1) * WIN < (t + 1) * LANES:
            j1 = lane + (t * LANES - (e0 + 1) * WIN)
            miss = jnp.where(j1 >= 0, rel(e0 + 1) - j1, miss)
        pieces.append(_ind(miss == 0))
    w = jnp.concatenate(pieces, axis=1)
    wait(b, slot)
    o_ref[...] = y1_ref[...] + _nn(w, buf[slot].reshape(N_EXPERTS * WIN, o_ref.shape[1]))


def _combine(y1, pos_t, off, rows, cap):
    n = y1.shape[0]
    nb = n // LANES
    d = y1.shape[1]
    grid_spec = pltpu.PrefetchScalarGridSpec(
        num_scalar_prefetch=1,
        grid=(nb,),
        in_specs=[pl.BlockSpec((LANES, d), lambda b, off: (b, 0)),
                  pl.BlockSpec((LANES, N_EXPERTS), lambda b, off: (b, 0)),
                  pl.BlockSpec(memory_space=pl.ANY)],
        out_specs=pl.BlockSpec((LANES, d), lambda b, off: (b, 0)),
        scratch_shapes=[pltpu.VMEM((2, N_EXPERTS * WIN_GROUPS, ROW_GROUP, d), BF16),
                        pltpu.SemaphoreType.DMA((2, N_EXPERTS))],
    )
    assert cap % ROW_GROUP == 0 and cap >= WIN
    rows = rows.reshape(N_EXPERTS, cap // ROW_GROUP, ROW_GROUP, d)
    return pl.pallas_call(
        functools.partial(_combine_kernel, cap=cap),
        grid_spec=grid_spec,
        out_shape=jax.ShapeDtypeStruct((n, d), F32),
        name="combine",
        compiler_params=_cparams(("arbitrary",)),
    )(off, y1, pos_t, rows)


def _rope_tables(seq_len):
    t = jnp.arange(seq_len)
    row = (t // GRID_W).astype(F32)
    col = (t % GRID_W).astype(F32)
    inv = ROPE_THETA ** (-jnp.arange(ROPE_PAIRS, dtype=F32) / ROPE_PAIRS)
    ang = jnp.stack([row[:, None] * inv, col[:, None] * inv], axis=1)
    ang = jnp.broadcast_to(ang[:, :, None, :], (seq_len, 2, 2, ROPE_PAIRS)).reshape(seq_len, HEAD_DIM)
    first_half = (jnp.arange(HEAD_DIM) % (2 * ROPE_PAIRS)) < ROPE_PAIRS
    return jnp.cos(ang), jnp.where(first_half[None, :], -jnp.sin(ang), jnp.sin(ang))


def _pick(n, pref):
    t = min(n, pref)
    assert n % t == 0, (n, pref)
    return t


def _layer(x, mem, p):
    n = x.shape[0]
    cos, sin_signed = _rope_tables(n)
    km, vm = _mem_kv(mem, p["g_mem"], p["w_mem_kv"], p["g_mk"])
    (q, k, vt, hq, kf, gf, kb, gb, vi, vit, og, mq) = _in_proj(
        x, p["g_mix"], p["w_in"], p["wvt"], p["wvit"], cos, sin_signed, p["g_q"], p["g_k"], p["g_mq"],
        p["lb_fwd"], p["lb_bwd"], _pick(n, 256))
    attn = _attention(q, k, vt, _pick(n, 256), _pick(n, 512))
    hgrn = _hgrn(hq, kf, gf, kb, gb, vi, vit, og, p["g_hg_out"], _pick(n, 512))
    y1, h2, aff_t = _out_proj(x, attn, hgrn, mq, km, vm, p["w_out"], p["g_ffn"], p["w_router_t"], _pick(n, 256))
    cap = EXPERT_CAPACITY_FACTOR * n // N_EXPERTS
    idx, gate, pos, off = _topk(aff_t, cap)
    xs = _gather(h2, idx, cap)
    gate_col = gate.reshape(N_EXPERTS, cap, 1)
    rows = _ffn(xs, gate_col, p["w_gate"], p["w_up"], p["w_down"], _pick(cap, 1024), 512)
    pos_t = pos.reshape(N_EXPERTS, n).T
    off_s = off[:, :, 0]
    return _combine(y1, pos_t, off_s, rows, cap)


def kernel(x_prompt, x_sample, mem_prompt, mem_sample, g_mix, w_in, g_q, g_k, g_hg_out, lb_fwd, lb_bwd, g_mem, w_mem_kv,
           g_mq, g_mk, w_out, g_ffn, w_router, w_gate, w_up, w_down):
    assert g_mix.shape[0] == 1 and lb_fwd.shape[0] == 2, "single layer: lower bound is the first cumulative-softmax row"
    w_in_bf = w_in[0].astype(BF16)
    p = {
        "g_mix": g_mix, "w_in": w_in_bf,
        "wvt": w_in_bf[:, C_V:C_HQ].T, "wvit": w_in_bf[:, C_HI:C_HG].T,
        "g_q": g_q, "g_k": g_k, "g_hg_out": g_hg_out, "lb_fwd": lb_fwd, "lb_bwd": lb_bwd,
        "g_mem": g_mem, "w_mem_kv": w_mem_kv[0].astype(BF16), "g_mq": g_mq, "g_mk": g_mk,
        "w_out": w_out[0].astype(BF16), "g_ffn": g_ffn, "w_router_t": w_router[0].T,
        "w_gate": w_gate[0].astype(BF16), "w_up": w_up[0].astype(BF16), "w_down": w_down[0].astype(BF16),
    }
    y_prompt = _layer(x_prompt[0], mem_prompt[0], p)
    y_sample = _layer(x_sample[0], mem_sample[0], p)
    return (y_prompt[None], y_sample[None])
```

```python
import functools

import jax
import jax.numpy as jnp
from jax import lax
from jax.experimental import pallas as pl
from jax.experimental.pallas import tpu as pltpu

F32 = jnp.float32
BF16 = jnp.bfloat16
I32 = jnp.int32

D_MODEL = 2048
HEAD_DIM = 128
ATTN_HEADS = 8
ATTN_KV_HEADS = 2
KV_GROUP = ATTN_HEADS // ATTN_KV_HEADS
HG_HEADS = 4
HG_WIDTH = HG_HEADS * HEAD_DIM
MEM_HEADS = 4
GRID_W = 64
ROPE_THETA = 10000.0
ROPE_PAIRS = HEAD_DIM // 4
N_EXPERTS = 16
EXPERT_CAPACITY_FACTOR = 2
D_EXPERT = 2048
EPS = 1e-6
HG_CHUNK = 128
LANES = 128
NEG_BIG = -1e30
LOG2E = 1.4426950408889634

C_Q, C_K, C_V, C_HQ, C_ZF, C_ZB, C_HI, C_HG, C_MQ, C_END = 0, 1024, 1280, 1536, 2048, 2560, 3072, 3584, 4096, 4608

VMEM_LIMIT = 56 * 1024 * 1024


def _cparams(sem, vmem=VMEM_LIMIT):
    return pltpu.CompilerParams(dimension_semantics=sem, vmem_limit_bytes=vmem)


def _nt(a, b):
    return lax.dot_general(a, b, (((1,), (1,)), ((), ())), preferred_element_type=F32)


def _nn(a, b):
    return jnp.dot(a, b, preferred_element_type=F32)


def _ind(mask):
    return jnp.where(mask, 1.0, 0.0).astype(BF16)


def _rms(x, g):
    return x * lax.rsqrt(jnp.mean(x * x, axis=-1, keepdims=True) + EPS) * g


def _memkv_kernel(mem_ref, gmem_ref, w_ref, gmk_ref, km_ref, vm_ref):
    h = _rms(mem_ref[...], gmem_ref[...]).astype(BF16)
    kv = _nn(h, w_ref[...])
    for hh in range(MEM_HEADS):
        sl = slice(hh * HEAD_DIM, (hh + 1) * HEAD_DIM)
        km_ref[:, sl] = _rms(kv[:, sl], gmk_ref[...]).astype(BF16)
    vm_ref[...] = kv[:, MEM_HEADS * HEAD_DIM:].astype(BF16)


def _mem_kv(mem, g_mem, w_mem_kv_bf, g_mk):
    m = mem.shape[0]
    width = MEM_HEADS * HEAD_DIM
    return pl.pallas_call(
        _memkv_kernel,
        out_shape=(jax.ShapeDtypeStruct((m, width), BF16), jax.ShapeDtypeStruct((m, width), BF16)),
        name="mem_kv",
        compiler_params=_cparams(None),
    )(mem, g_mem, w_mem_kv_bf, g_mk)


def _norm_rope(a, g, cos, sin_signed, first_half):
    y = _rms(a, g)
    rot = jnp.where(first_half, pltpu.roll(y, HEAD_DIM - ROPE_PAIRS, 1), pltpu.roll(y, ROPE_PAIRS, 1))
    return y * cos + rot * sin_signed


def _inproj_kernel(x_ref, gmix_ref, w_ref, wvt_ref, wvit_ref, cos_ref, sin_ref, gq_ref, gk_ref, gmq_ref, lbf_ref, lbb_ref,
                   q_ref, k_ref, vt_ref, hq_ref, kf_ref, gf_ref, kb_ref, gb_ref, vi_ref, vit_ref, og_ref, mq_ref, h_scr):
    tm = x_ref.shape[0]
    h_scr[...] = _rms(x_ref[...], gmix_ref[...]).astype(BF16)
    h = h_scr[...]
    cos = cos_ref[...]
    sin = sin_ref[...]
    lane = lax.broadcasted_iota(I32, (tm, HEAD_DIM), 1)
    first_half = (lane % (2 * ROPE_PAIRS)) < ROPE_PAIRS
    scale = HEAD_DIM ** -0.5

    def proj(c0, c1):
        return _nn(h, w_ref[:, c0:c1])

    for c in range(ATTN_HEADS // 4):
        a = proj(C_Q + 512 * c, C_Q + 512 * (c + 1))
        for hh in range(4):
            sl = slice(hh * HEAD_DIM, (hh + 1) * HEAD_DIM)
            y = _norm_rope(a[:, sl], gq_ref[...], cos, sin, first_half) * (scale * LOG2E)
            q_ref[:, 512 * c + hh * HEAD_DIM:512 * c + (hh + 1) * HEAD_DIM] = y.astype(BF16)
    a = proj(C_K, C_V)
    for hh in range(ATTN_KV_HEADS):
        sl = slice(hh * HEAD_DIM, (hh + 1) * HEAD_DIM)
        k_ref[:, sl] = _norm_rope(a[:, sl], gk_ref[...], cos, sin, first_half).astype(BF16)
    vt_ref[...] = _nt(wvt_ref[...], h).astype(BF16)

    hq_ref[...] = jax.nn.silu(proj(C_HQ, C_ZF)).astype(BF16)

    def forget(z, lb_ref, k_out, g_out):
        a2 = lb_ref[...]
        e2 = jnp.exp(a2 - jnp.max(a2, axis=0, keepdims=True))
        lb = e2[0:1, :] / jnp.sum(e2, axis=0, keepdims=True)
        f = lb + (1.0 - lb) * jax.nn.sigmoid(z)
        g_out[...] = jnp.log(f)
        k_out[...] = ((1.0 - lb) * jax.nn.sigmoid(-z)).astype(BF16)

    forget(proj(C_ZF, C_ZB), lbf_ref, kf_ref, gf_ref)
    forget(proj(C_ZB, C_HI), lbb_ref, kb_ref, gb_ref)
    vi_ref[...] = proj(C_HI, C_HG).astype(BF16)
    vit_ref[...] = _nt(wvit_ref[...], h).astype(BF16)
    og_ref[...] = jax.nn.silu(proj(C_HG, C_MQ)).astype(BF16)
    a = proj(C_MQ, C_END)
    for hh in range(MEM_HEADS):
        sl = slice(hh * HEAD_DIM, (hh + 1) * HEAD_DIM)
        mq_ref[:, sl] = (_rms(a[:, sl], gmq_ref[...]) * scale).astype(BF16)


def _in_proj(x, g_mix, w_bf, wvt, wvit, cos, sin_signed, g_q, g_k, g_mq, lb_f, lb_b, tm):
    n = x.shape[0]
    grid = (n // tm,)
    row = lambda w: pl.BlockSpec((tm, w), lambda i: (i, 0))
    col = lambda h: pl.BlockSpec((h, tm), lambda i: (0, i))
    full = lambda a: pl.BlockSpec(a.shape, lambda i: (0,) * a.ndim)
    res = lambda a: pl.BlockSpec(a.shape, lambda i: (0,) * a.ndim, pipeline_mode=pl.Buffered(1))
    sd = jax.ShapeDtypeStruct
    out_shape = (sd((n, 1024), BF16), sd((n, 256), BF16), sd((256, n), BF16),
                 sd((n, 512), BF16), sd((n, 512), BF16), sd((n, 512), F32), sd((n, 512), BF16), sd((n, 512), F32),
                 sd((n, 512), BF16), sd((512, n), BF16), sd((n, 512), BF16), sd((n, 512), BF16))
    out_specs = (row(1024), row(256), col(256), row(512), row(512), row(512), row(512), row(512),
                 row(512), col(512), row(512), row(512))
    return pl.pallas_call(
        _inproj_kernel,
        grid=grid,
        in_specs=[row(D_MODEL), full(g_mix), res(w_bf), res(wvt), res(wvit), row(HEAD_DIM), row(HEAD_DIM),
                  full(g_q), full(g_k), full(g_mq), full(lb_f), full(lb_b)],
        out_specs=out_specs,
        out_shape=out_shape,
        scratch_shapes=[pltpu.VMEM((tm, D_MODEL), BF16)],
        name="in_proj",
        compiler_params=_cparams(("parallel",)),
    )(x, g_mix, w_bf, wvt, wvit, cos, sin_signed, g_q, g_k, g_mq, lb_f, lb_b)


ONES_ROWS = 16


def _attn_kernel(q_ref, k_ref, vt_ref, o_ref, acc_ref, s_ref, *, tk, nk):
    tq = q_ref.shape[0]
    acc_ref[...] = jnp.zeros_like(acc_ref)
    ones = jnp.ones((ONES_ROWS, tk), BF16)

    def scores(hh, off):
        s_ref[hh] = _nt(k_ref[pl.ds(off, tk), :], q_ref[:, hh * HEAD_DIM:(hh + 1) * HEAD_DIM])

    lead = 2
    for hh in range(lead):
        scores(hh, 0)

    def body(i, ms):
        off = pl.multiple_of(i * tk, tk)
        off_next = pl.multiple_of(jnp.minimum(i + 1, nk - 1) * tk, tk)
        vx = jnp.concatenate([vt_ref[:, pl.ds(off, tk)], ones], axis=0)
        out = []
        for hh in range(KV_GROUP):
            s = s_ref[hh]
            m_new = jnp.maximum(ms[hh], jnp.max(s, axis=0, keepdims=True))
            p = jnp.exp2((s - m_new).astype(BF16))
            alpha = jnp.exp2(ms[hh] - m_new)
            nxt = hh + lead
            scores(nxt % KV_GROUP, off if nxt < KV_GROUP else off_next)
            acc_ref[hh] = alpha * acc_ref[hh] + _nn(vx, p)
            out.append(m_new)
        return tuple(out)

    init = tuple(jnp.full((1, tq), NEG_BIG, F32) for _ in range(KV_GROUP))
    lax.fori_loop(0, nk, body, init)
    for hh in range(KV_GROUP):
        a = acc_ref[hh]
        o_ref[:, hh * HEAD_DIM:(hh + 1) * HEAD_DIM] = (a[0:HEAD_DIM] / a[HEAD_DIM:HEAD_DIM + 1]).T.astype(BF16)


def _attention(q, k, vt, tq, tk):
    s = q.shape[0]
    width = KV_GROUP * HEAD_DIM
    return pl.pallas_call(
        functools.partial(_attn_kernel, tk=tk, nk=s // tk),
        grid=(ATTN_KV_HEADS, s // tq),
        in_specs=[pl.BlockSpec((tq, width), lambda g, i: (i, g)),
                  pl.BlockSpec((s, HEAD_DIM), lambda g, i: (0, g)),
                  pl.BlockSpec((HEAD_DIM, s), lambda g, i: (g, 0))],
        out_specs=pl.BlockSpec((tq, width), lambda g, i: (i, g)),
        out_shape=jax.ShapeDtypeStruct((s, ATTN_HEADS * HEAD_DIM), BF16),
        scratch_shapes=[pltpu.VMEM((KV_GROUP, HEAD_DIM + ONES_ROWS, tq), F32), pltpu.VMEM((KV_GROUP, tk, tq), F32)],
        name="attention",
        compiler_params=_cparams(("parallel", "parallel")),
    )(q, k, vt)


def _hgrn_levels():
    b, out = HG_CHUNK, []
    while b >= 2:
        out.append(b)
        b //= 2
    return out


def _hgrn_chunk(q, k, g, v, vt, state_ref, rev):
    c = HG_CHUNK
    w = HG_WIDTH
    row = lax.broadcasted_iota(I32, (c, w), 0)
    b = g
    sh = 1
    while sh < c:
        if rev:
            b = b + jnp.where(row < c - sh, pltpu.roll(b, c - sh, 0), 0.0)
        else:
            b = b + jnp.where(row >= sh, pltpu.roll(b, sh, 0), 0.0)
        sh *= 2
    qf = q.astype(F32)
    kf = k.astype(F32)
    ti = lax.broadcasted_iota(I32, (c, c), 0)
    si = lax.broadcasted_iota(I32, (c, c), 1)
    att = [jnp.where(ti == si, _nt(q[:, h * HEAD_DIM:(h + 1) * HEAD_DIM], k[:, h * HEAD_DIM:(h + 1) * HEAD_DIM]), 0.0)
           for h in range(HG_HEADS)]
    for blk in _hgrn_levels():
        half = blk // 2
        ref_row = half if rev else half - 1
        pos = row % blk
        if blk >= 8:
            r = jnp.concatenate(
                [jnp.broadcast_to(b[s0 + ref_row:s0 + ref_row + 1, :], (blk, w)) for s0 in range(0, c, blk)], axis=0)
        else:
            r = b
            for d in range(-ref_row, blk - ref_row):
                if d != 0:
                    r = jnp.where(pos - ref_row == d, pltpu.roll(b, d % c, 0), r)
        e = jnp.exp(-jnp.abs(b - r))
        is_q = (pos < half) if rev else (pos >= half)
        ql = jnp.where(is_q, qf * e, 0.0).astype(BF16)
        kl = jnp.where(is_q, 0.0, kf * e).astype(BF16)
        same = (ti // blk) == (si // blk)
        for h in range(HG_HEADS):
            sl = slice(h * HEAD_DIM, (h + 1) * HEAD_DIM)
            att[h] = att[h] + jnp.where(same, _nt(ql[:, sl], kl[:, sl]), 0.0)
    b_end = b[0:1, :] if rev else b[c - 1:c, :]
    qd = (qf * jnp.exp(b)).astype(BF16)
    kd = (kf * jnp.exp(b_end - b)).astype(BF16)
    dec = jnp.exp(b_end)
    outs = []
    for h in range(HG_HEADS):
        sl = slice(h * HEAD_DIM, (h + 1) * HEAD_DIM)
        st = state_ref[h]
        o = _nn(att[h].astype(BF16), v[:, sl]) + _nt(qd[:, sl], st.astype(BF16))
        state_ref[h] = dec[:, sl] * st + _nn(vt[sl, :], kd[:, sl])
        outs.append(o)
    return jnp.concatenate(outs, axis=1)


def _hgrn_fwd_kernel(q_ref, k_ref, g_ref, v_ref, vt_ref, o_ref, state_ref, *, nch):
    @pl.when(pl.program_id(0) == 0)
    def _():
        state_ref[...] = jnp.zeros_like(state_ref)

    def body(i, carry):
        r0 = pl.multiple_of(i * HG_CHUNK, HG_CHUNK)
        rs = pl.ds(r0, HG_CHUNK)
        o_ref[rs, :] = _hgrn_chunk(q_ref[rs, :], k_ref[rs, :], g_ref[rs, :], v_ref[rs, :], vt_ref[:, rs], state_ref, False)
        return carry

    lax.fori_loop(0, nch, body, 0)


def _hgrn_bwd_kernel(q_ref, k_ref, g_ref, v_ref, vt_ref, of_ref, og_ref, gout_ref, o_ref, state_ref, *, nch):
    @pl.when(pl.program_id(0) == 0)
    def _():
        state_ref[...] = jnp.zeros_like(state_ref)

    def body(i, carry):
        r0 = pl.multiple_of((nch - 1 - i) * HG_CHUNK, HG_CHUNK)
        rs = pl.ds(r0, HG_CHUNK)
        o = _hgrn_chunk(q_ref[rs, :], k_ref[rs, :], g_ref[rs, :], v_ref[rs, :], vt_ref[:, rs], state_ref, True)
        o = o + of_ref[rs, :]
        og = og_ref[rs, :].astype(F32)
        for h in range(HG_HEADS):
            sl = slice(h * HEAD_DIM, (h + 1) * HEAD_DIM)
            o_ref[rs, sl] = (_rms(o[:, sl], gout_ref[...]) * og[:, sl]).astype(BF16)
        return carry

    lax.fori_loop(0, nch, body, 0)


def _hgrn(hq, kf, gf, kb, gb, vi, vit, og, g_out, tb):
    n = hq.shape[0]
    nblk = n // tb
    nch = tb // HG_CHUNK
    state = pltpu.VMEM((HG_HEADS, HEAD_DIM, HEAD_DIM), F32)
    fr = lambda i: (i, 0)
    fc = lambda i: (0, i)
    o_f = pl.pallas_call(
        functools.partial(_hgrn_fwd_kernel, nch=nch),
        grid=(nblk,),
        in_specs=[pl.BlockSpec((tb, HG_WIDTH), fr)] * 4 + [pl.BlockSpec((HG_WIDTH, tb), fc)],
        out_specs=pl.BlockSpec((tb, HG_WIDTH), fr),
        out_shape=jax.ShapeDtypeStruct((n, HG_WIDTH), F32),
        scratch_shapes=[state],
        name="hgrn_fwd",
        compiler_params=_cparams(("arbitrary",)),
    )(hq, kf, gf, vi, vit)
    br = lambda i: (nblk - 1 - i, 0)
    bc = lambda i: (0, nblk - 1 - i)
    return pl.pallas_call(
        functools.partial(_hgrn_bwd_kernel, nch=nch),
        grid=(nblk,),
        in_specs=[pl.BlockSpec((tb, HG_WIDTH), br)] * 4 + [pl.BlockSpec((HG_WIDTH, tb), bc)]
        + [pl.BlockSpec((tb, HG_WIDTH), br)] * 2 + [pl.BlockSpec(g_out.shape, lambda i: (0, 0))],
        out_specs=pl.BlockSpec((tb, HG_WIDTH), br),
        out_shape=jax.ShapeDtypeStruct((n, HG_WIDTH), BF16),
        scratch_shapes=[state],
        name="hgrn_bwd",
        compiler_params=_cparams(("arbitrary",)),
    )(hq, kb, gb, vi, vit, o_f, og, g_out)


def _outproj_kernel(x_ref, attn_ref, hgrn_ref, mq_ref, km_ref, vm_ref, w_ref, gffn_ref, wrt_ref,
                    y_ref, h2_ref, aff_ref, mix_ref):
    n_attn = ATTN_HEADS * HEAD_DIM
    mix_ref[:, 0:n_attn] = attn_ref[...]
    mix_ref[:, n_attn:n_attn + HG_WIDTH] = hgrn_ref[...]
    for hh in range(MEM_HEADS):
        sl = slice(hh * HEAD_DIM, (hh + 1) * HEAD_DIM)
        s = _nt(mq_ref[:, sl], km_ref[:, sl])
        p = jnp.exp(s - jnp.max(s, axis=-1, keepdims=True))
        p = p / jnp.sum(p, axis=-1, keepdims=True)
        c0 = n_attn + HG_WIDTH + hh * HEAD_DIM
        mix_ref[:, c0:c0 + HEAD_DIM] = _nn(p.astype(BF16), vm_ref[:, sl]).astype(BF16)
    y = x_ref[...] + _nn(mix_ref[...], w_ref[...])
    y_ref[...] = y
    h2 = _rms(y, gffn_ref[...])
    h2_ref[...] = h2.astype(BF16)
    logits = lax.dot_general(wrt_ref[...], h2, (((1,), (1,)), ((), ())), preferred_element_type=F32,
                             precision=lax.Precision.HIGHEST)
    e = jnp.exp(logits - jnp.max(logits, axis=0, keepdims=True))
    aff_ref[...] = e / jnp.sum(e, axis=0, keepdims=True)


def _out_proj(x, attn, hgrn, mq, km, vm, w_out_bf, g_ffn, w_router_t, tm):
    n = x.shape[0]
    row = lambda w: pl.BlockSpec((tm, w), lambda i: (i, 0))
    full = lambda a: pl.BlockSpec(a.shape, lambda i: (0,) * a.ndim)
    res = lambda a: pl.BlockSpec(a.shape, lambda i: (0,) * a.ndim, pipeline_mode=pl.Buffered(1))
    sd = jax.ShapeDtypeStruct
    return pl.pallas_call(
        _outproj_kernel,
        grid=(n // tm,),
        in_specs=[row(D_MODEL), row(1024), row(512), row(512), full(km), full(vm), res(w_out_bf), full(g_ffn),
                  full(w_router_t)],
        out_specs=(row(D_MODEL), row(D_MODEL), pl.BlockSpec((N_EXPERTS, tm), lambda i: (0, i))),
        out_shape=(sd((n, D_MODEL), F32), sd((n, D_MODEL), BF16), sd((N_EXPERTS, n), F32)),
        scratch_shapes=[pltpu.VMEM((tm, D_MODEL), BF16)],
        name="out_proj",
        compiler_params=_cparams(("parallel",)),
    )(x, attn, hgrn, mq, km, vm, w_out_bf, g_ffn, w_router_t)


def _split3(x):
    hi = x.astype(BF16)
    r1 = x - hi.astype(F32)
    mid = r1.astype(BF16)
    lo = (r1 - mid.astype(F32)).astype(BF16)
    return hi, mid, lo


def _topk_kernel(aff_ref, afft_ref, idx_ref, gate_ref, pos_ref, off_ref, cl_scr, offb_scr, totb_scr, *, cap):
    e_n, nb, _ = aff_ref.shape
    rows = e_n * nb
    aff = aff_ref[...]
    keys = lax.bitcast_convert_type(aff, I32)

    def count(mask):
        s = jnp.sum(mask.astype(F32), axis=1, keepdims=True)
        return jnp.sum(s, axis=2, keepdims=True)

    def bis(i, t):
        cand = t | (jnp.int32(1) << (30 - i))
        return jnp.where(count(keys >= cand) >= cap, cand, t)

    thr = lax.fori_loop(0, 31, bis, jnp.zeros((e_n, 1, 1), I32))
    gt = keys > thr
    eq = keys == thr
    need = cap - count(gt)

    upper = _ind(lax.broadcasted_iota(I32, (LANES, LANES), 0) <= lax.broadcasted_iota(I32, (LANES, LANES), 1))
    ones = jnp.ones((LANES, LANES), BF16)
    bi = lax.broadcasted_iota(I32, (nb, nb), 0)
    bj = lax.broadcasted_iota(I32, (nb, nb), 1)
    strict_lower = _ind(bj < bi)

    def prefix(mask):
        m2 = _ind(mask).reshape(rows, LANES)
        cl = _nn(m2, upper).reshape(e_n, nb, LANES)
        tot = _nn(m2, ones).reshape(e_n, nb, LANES)
        off = jnp.stack([_nn(strict_lower, tot[e].astype(BF16)) for e in range(e_n)], axis=0)
        return cl, off, tot

    cl, off, _ = prefix(eq)
    rank_eq = off + cl - eq.astype(F32)
    sel = gt | (eq & (rank_eq < need))
    cl, off, tot = prefix(sel)
    pos_ref[...] = jnp.where(sel, off + cl - 1.0, -1.0).astype(I32)
    off_ref[...] = off.astype(I32)
    cl_scr[...] = cl
    offb_scr[...] = off
    totb_scr[...] = tot

    lower_incl = _ind(lax.broadcasted_iota(I32, (LANES, LANES), 1) <= lax.broadcasted_iota(I32, (LANES, LANES), 0))
    reps = cap // LANES
    s_row = lax.broadcasted_iota(I32, (nb, cap), 1).astype(F32)
    b_col = lax.broadcasted_iota(I32, (nb, cap), 0).astype(F32)
    j_col = lax.broadcasted_iota(I32, (LANES, cap), 0).astype(F32)

    def per_expert(e, carry):
        sel_e = _ind(pos_ref[e] >= 0)
        clt = _nt(lower_incl, sel_e)
        offt = jnp.concatenate([offb_scr[e]] * reps, axis=1)
        endt = offt + jnp.concatenate([totb_scr[e]] * reps, axis=1)
        hit = (offt <= s_row) & (s_row < endt)
        onehot = _ind(hit)
        g_cnt = _nn(clt.astype(BF16), onehot)
        local = s_row[0:1, :] - jnp.sum(jnp.where(hit, offt, 0.0), axis=0, keepdims=True)
        j_row = jnp.sum((g_cnt <= local).astype(F32), axis=0, keepdims=True)
        b_row = jnp.sum(jnp.where(hit, b_col, 0.0), axis=0, keepdims=True)
        idx_ref[e] = (b_row * LANES + j_row).astype(I32)
        hi, mid, lo = _split3(afft_ref[e])
        g_aff = _nn(hi, onehot) + _nn(mid, onehot) + _nn(lo, onehot)
        gate_ref[e] = jnp.sum(jnp.where(j_col == j_row, g_aff, 0.0), axis=0, keepdims=True)
        return carry

    lax.fori_loop(0, e_n, per_expert, 0)


def _topk(aff_t, cap):
    e_n, n = aff_t.shape
    nb = n // LANES
    aff3 = aff_t.reshape(e_n, nb, LANES)
    afft3 = jnp.swapaxes(aff3, 1, 2)
    sd = jax.ShapeDtypeStruct
    return pl.pallas_call(
        functools.partial(_topk_kernel, cap=cap),
        out_shape=(sd((e_n, 1, cap), I32), sd((e_n, 1, cap), F32), sd((e_n, nb, LANES), I32), sd((e_n, nb, LANES), I32)),
        scratch_shapes=[pltpu.VMEM((e_n, nb, LANES), F32)] * 3,
        name="topk",
        compiler_params=_cparams(None),
    )(aff3, afft3)


GATHER_RING = 128


def _gather_kernel(idx_ref, src_ref, dst_ref, sem, *, cap):
    e = pl.program_id(0)

    def copy(s, k):
        return pltpu.make_async_copy(src_ref.at[idx_ref[0, 0, s]], dst_ref.at[e * cap + s], sem.at[k])

    def group(gi, carry):
        for k in range(GATHER_RING):
            s = gi * GATHER_RING + k

            @pl.when(gi > 0)
            def _():
                copy(s - GATHER_RING, k).wait()

            copy(s, k).start()
        return carry

    lax.fori_loop(0, cap // GATHER_RING, group, 0)
    for k in range(GATHER_RING):
        copy(cap - GATHER_RING + k, k).wait()


def _gather(h2, idx, cap):
    n = h2.shape[0]
    src = h2.reshape(n, D_MODEL // LANES, LANES)
    out = pl.pallas_call(
        functools.partial(_gather_kernel, cap=cap),
        grid=(N_EXPERTS,),
        in_specs=[pl.BlockSpec((1, 1, cap), lambda e: (e, 0, 0), memory_space=pltpu.SMEM),
                  pl.BlockSpec(memory_space=pl.ANY)],
        out_specs=pl.BlockSpec(memory_space=pl.ANY),
        out_shape=jax.ShapeDtypeStruct((N_EXPERTS * cap, D_MODEL // LANES, LANES), BF16),
        scratch_shapes=[pltpu.SemaphoreType.DMA((GATHER_RING,))],
        name="gather",
        compiler_params=_cparams(("arbitrary",)),
    )(idx, src)
    return out.reshape(N_EXPERTS, cap, D_MODEL)


def _ffn_kernel(x_ref, gate_ref, wg_ref, wu_ref, wd_ref, o_ref, acc_ref):
    f = pl.program_id(2)
    x = x_ref[0]
    hid = jax.nn.silu(_nn(x, wg_ref[0].astype(BF16))) * _nn(x, wu_ref[0].astype(BF16))
    part = _nn(hid.astype(BF16), wd_ref[0].astype(BF16))

    @pl.when(f == 0)
    def _():
        acc_ref[...] = part

    @pl.when(f > 0)
    def _():
        acc_ref[...] += part

    @pl.when(f == pl.num_programs(2) - 1)
    def _():
        o_ref[0] = (acc_ref[...] * gate_ref[0]).astype(BF16)


def _ffn(xs, gate_col, wg, wu, wd, ts, tf):
    e_n, cap, d = xs.shape
    return pl.pallas_call(
        _ffn_kernel,
        grid=(e_n, cap // ts, D_EXPERT // tf),
        in_specs=[pl.BlockSpec((1, ts, d), lambda e, s, f: (e, s, 0)),
                  pl.BlockSpec((1, ts, 1), lambda e, s, f: (e, s, 0)),
                  pl.BlockSpec((1, d, tf), lambda e, s, f: (e, 0, f)),
                  pl.BlockSpec((1, d, tf), lambda e, s, f: (e, 0, f)),
                  pl.BlockSpec((1, tf, d), lambda e, s, f: (e, f, 0))],
        out_specs=pl.BlockSpec((1, ts, d), lambda e, s, f: (e, s, 0)),
        out_shape=jax.ShapeDtypeStruct((e_n, cap, d), BF16),
        scratch_shapes=[pltpu.VMEM((ts, d), F32)],
        name="ffn",
        compiler_params=_cparams(("parallel", "parallel", "arbitrary")),
    )(xs, gate_col, wg, wu, wd)


ROW_GROUP = 16
WIN_GROUPS = LANES // ROW_GROUP + 1
WIN = WIN_GROUPS * ROW_GROUP


def _combine_kernel(off_ref, y1_ref, post_ref, rows_ref, o_ref, buf, sem, *, cap):
    b = pl.program_id(0)
    nb = pl.num_programs(0)

    def first_group(blk, e):
        return jnp.minimum(off_ref[e, blk] // ROW_GROUP, cap // ROW_GROUP - WIN_GROUPS)

    def copy(blk, slot, e):
        return pltpu.make_async_copy(rows_ref.at[e, pl.ds(first_group(blk, e), WIN_GROUPS)],
                                     buf.at[slot, pl.ds(e * WIN_GROUPS, WIN_GROUPS)], sem.at[slot, e])

    def start(blk, slot):
        for e in range(N_EXPERTS):
            copy(blk, slot, e).start()

    def wait(blk, slot):
        for e in range(N_EXPERTS):
            copy(blk, slot, e).wait()

    slot = b % 2

    @pl.when(b == 0)
    def _():
        start(0, 0)

    @pl.when(b + 1 < nb)
    def _():
        start(b + 1, 1 - slot)

    post = post_ref[...]
    lane = lax.broadcasted_iota(I32, (LANES, LANES), 1)

    def rel(e):
        return jnp.broadcast_to(post[:, e:e + 1], (LANES, LANES)) - first_group(b, e) * ROW_GROUP

    pieces = []
    for t in range(N_EXPERTS * WIN // LANES):
        e0 = t * LANES // WIN
        miss = rel(e0) - (lane + (t * LANES - e0 * WIN))
        if (e0 + 1) * WIN < (t + 1) * LANES:
            j1 = lane + (t * LANES - (e0 + 1) * WIN)
            miss = jnp.where(j1 >= 0, rel(e0 + 1) - j1, miss)
        pieces.append(_ind(miss == 0))
    w = jnp.concatenate(pieces, axis=1)
    wait(b, slot)
    o_ref[...] = y1_ref[...] + _nn(w, buf[slot].reshape(N_EXPERTS * WIN, o_ref.shape[1]))


def _combine(y1, pos_t, off, rows, cap):
    n = y1.shape[0]
    nb = n // LANES
    d = y1.shape[1]
    grid_spec = pltpu.PrefetchScalarGridSpec(
        num_scalar_prefetch=1,
        grid=(nb,),
        in_specs=[pl.BlockSpec((LANES, d), lambda b, off: (b, 0)),
                  pl.BlockSpec((LANES, N_EXPERTS), lambda b, off: (b, 0)),
                  pl.BlockSpec(memory_space=pl.ANY)],
        out_specs=pl.BlockSpec((LANES, d), lambda b, off: (b, 0)),
        scratch_shapes=[pltpu.VMEM((2, N_EXPERTS * WIN_GROUPS, ROW_GROUP, d), BF16),
                        pltpu.SemaphoreType.DMA((2, N_EXPERTS))],
    )
    assert cap % ROW_GROUP == 0 and cap >= WIN
    rows = rows.reshape(N_EXPERTS, cap // ROW_GROUP, ROW_GROUP, d)
    return pl.pallas_call(
        functools.partial(_combine_kernel, cap=cap),
        grid_spec=grid_spec,
        out_shape=jax.ShapeDtypeStruct((n, d), F32),
        name="combine",
        compiler_params=_cparams(("arbitrary",)),
    )(off, y1, pos_t, rows)


def _rope_tables(seq_len):
    t = jnp.arange(seq_len)
    row = (t // GRID_W).astype(F32)
    col = (t % GRID_W).astype(F32)
    inv = ROPE_THETA ** (-jnp.arange(ROPE_PAIRS, dtype=F32) / ROPE_PAIRS)
    ang = jnp.stack([row[:, None] * inv, col[:, None] * inv], axis=1)
    ang = jnp.broadcast_to(ang[:, :, None, :], (seq_len, 2, 2, ROPE_PAIRS)).reshape(seq_len, HEAD_DIM)
    first_half = (jnp.arange(HEAD_DIM) % (2 * ROPE_PAIRS)) < ROPE_PAIRS
    return jnp.cos(ang), jnp.where(first_half[None, :], -jnp.sin(ang), jnp.sin(ang))


def _pick(n, pref):
    t = min(n, pref)
    assert n % t == 0, (n, pref)
    return t


def _layer(x, mem, p):
    n = x.shape[0]
    cos, sin_signed = _rope_tables(n)
    km, vm = _mem_kv(mem, p["g_mem"], p["w_mem_kv"], p["g_mk"])
    (q, k, vt, hq, kf, gf, kb, gb, vi, vit, og, mq) = _in_proj(
        x, p["g_mix"], p["w_in"], p["wvt"], p["wvit"], cos, sin_signed, p["g_q"], p["g_k"], p["g_mq"],
        p["lb_fwd"], p["lb_bwd"], _pick(n, 256))
    attn = _attention(q, k, vt, _pick(n, 256), _pick(n, 1024))
    hgrn = _hgrn(hq, kf, gf, kb, gb, vi, vit, og, p["g_hg_out"], _pick(n, 512))
    y1, h2, aff_t = _out_proj(x, attn, hgrn, mq, km, vm, p["w_out"], p["g_ffn"], p["w_router_t"], _pick(n, 512))
    cap = EXPERT_CAPACITY_FACTOR * n // N_EXPERTS
    idx, gate, pos, off = _topk(aff_t, cap)
    xs = _gather(h2, idx, cap)
    gate_col = gate.reshape(N_EXPERTS, cap, 1)
    rows = _ffn(xs, gate_col, p["w_gate"], p["w_up"], p["w_down"], _pick(cap, 1024), 256)
    pos_t = pos.reshape(N_EXPERTS, n).T
    off_s = off[:, :, 0]
    return _combine(y1, pos_t, off_s, rows, cap)


def kernel(x_prompt, x_sample, mem_prompt, mem_sample, g_mix, w_in, g_q, g_k, g_hg_out, lb_fwd, lb_bwd, g_mem, w_mem_kv,
           g_mq, g_mk, w_out, g_ffn, w_router, w_gate, w_up, w_down):
    assert g_mix.shape[0] == 1 and lb_fwd.shape[0] == 2, "single layer: lower bound is the first cumulative-softmax row"
    w_in_bf = w_in[0].astype(BF16)
    p = {
        "g_mix": g_mix, "w_in": w_in_bf,
        "wvt": w_in_bf[:, C_V:C_HQ].T, "wvit": w_in_bf[:, C_HI:C_HG].T,
        "g_q": g_q, "g_k": g_k, "g_hg_out": g_hg_out, "lb_fwd": lb_fwd, "lb_bwd": lb_bwd,
        "g_mem": g_mem, "w_mem_kv": w_mem_kv[0].astype(BF16), "g_mq": g_mq, "g_mk": g_mk,
        "w_out": w_out[0].astype(BF16), "g_ffn": g_ffn, "w_router_t": w_router[0].T,
        "w_gate": w_gate[0], "w_up": w_up[0], "w_down": w_down[0],
    }
    y_prompt = _layer(x_prompt[0], mem_prompt[0], p)
    y_sample = _layer(x_sample[0], mem_sample[0], p)
    return (y_prompt[None], y_sample[None])
```

```python
import functools

import jax
import jax.numpy as jnp
from jax import lax
from jax.experimental import pallas as pl
from jax.experimental.pallas import tpu as pltpu

F32 = jnp.float32
BF16 = jnp.bfloat16
I32 = jnp.int32
U32 = jnp.uint32

D_MODEL = 2048
HEAD_DIM = 128
ATTN_HEADS = 8
ATTN_KV_HEADS = 2
KV_GROUP = ATTN_HEADS // ATTN_KV_HEADS
HG_HEADS = 4
HG_WIDTH = HG_HEADS * HEAD_DIM
MEM_HEADS = 4
GRID_W = 64
ROPE_THETA = 10000.0
ROPE_PAIRS = HEAD_DIM // 4
N_EXPERTS = 16
EXPERT_CAPACITY_FACTOR = 2
D_EXPERT = 2048
EPS = 1e-6
HG_CHUNK = 128
LANES = 128
NEG_BIG = -1e30
LOG2E = 1.4426950408889634

C_Q, C_K, C_V, C_HQ, C_ZF, C_ZB, C_HI, C_HG, C_MQ, C_END = 0, 1024, 1280, 1536, 2048, 2560, 3072, 3584, 4096, 4608

VMEM_LIMIT = 56 * 1024 * 1024


def _cparams(sem, vmem=VMEM_LIMIT):
    return pltpu.CompilerParams(dimension_semantics=sem, vmem_limit_bytes=vmem)


def _nt(a, b):
    return lax.dot_general(a, b, (((1,), (1,)), ((), ())), preferred_element_type=F32)


def _nn(a, b):
    return jnp.dot(a, b, preferred_element_type=F32)


def _ind(mask):
    return jnp.where(mask, 1.0, 0.0).astype(BF16)


def _rms(x, g):
    return x * lax.rsqrt(jnp.mean(x * x, axis=-1, keepdims=True) + EPS) * g


def _memkv_kernel(mem_ref, gmem_ref, w_ref, gmk_ref, km_ref, vm_ref):
    h = _rms(mem_ref[...], gmem_ref[...]).astype(BF16)
    kv = _nn(h, w_ref[...])
    for hh in range(MEM_HEADS):
        sl = slice(hh * HEAD_DIM, (hh + 1) * HEAD_DIM)
        km_ref[:, sl] = _rms(kv[:, sl], gmk_ref[...]).astype(BF16)
    vm_ref[...] = kv[:, MEM_HEADS * HEAD_DIM:].astype(BF16)


def _mem_kv(mem, g_mem, w_mem_kv_bf, g_mk):
    m = mem.shape[0]
    width = MEM_HEADS * HEAD_DIM
    return pl.pallas_call(
        _memkv_kernel,
        out_shape=(jax.ShapeDtypeStruct((m, width), BF16), jax.ShapeDtypeStruct((m, width), BF16)),
        name="mem_kv",
        compiler_params=_cparams(None),
    )(mem, g_mem, w_mem_kv_bf, g_mk)


def _norm_rope(a, g, cos, sin_signed, first_half):
    y = _rms(a, g)
    rot = jnp.where(first_half, pltpu.roll(y, HEAD_DIM - ROPE_PAIRS, 1), pltpu.roll(y, ROPE_PAIRS, 1))
    return y * cos + rot * sin_signed


def _inproj_kernel(x_ref, gmix_ref, w_ref, wvt_ref, wvit_ref, cos_ref, sin_ref, gq_ref, gk_ref, gmq_ref, lbf_ref, lbb_ref,
                   q_ref, k_ref, vt_ref, hq_ref, kf_ref, gf_ref, kb_ref, gb_ref, vi_ref, vit_ref, og_ref, mq_ref, h_scr):
    tm = x_ref.shape[0]
    h_scr[...] = _rms(x_ref[...], gmix_ref[...]).astype(BF16)
    h = h_scr[...]
    cos = cos_ref[...]
    sin = sin_ref[...]
    lane = lax.broadcasted_iota(I32, (tm, HEAD_DIM), 1)
    first_half = (lane % (2 * ROPE_PAIRS)) < ROPE_PAIRS
    scale = HEAD_DIM ** -0.5

    def proj(c0, c1):
        return _nn(h, w_ref[:, c0:c1])

    for c in range(ATTN_HEADS // 4):
        a = proj(C_Q + 512 * c, C_Q + 512 * (c + 1))
        for hh in range(4):
            sl = slice(hh * HEAD_DIM, (hh + 1) * HEAD_DIM)
            y = _norm_rope(a[:, sl], gq_ref[...], cos, sin, first_half) * (scale * LOG2E)
            q_ref[:, 512 * c + hh * HEAD_DIM:512 * c + (hh + 1) * HEAD_DIM] = y.astype(BF16)
    a = proj(C_K, C_V)
    for hh in range(ATTN_KV_HEADS):
        sl = slice(hh * HEAD_DIM, (hh + 1) * HEAD_DIM)
        k_ref[:, sl] = _norm_rope(a[:, sl], gk_ref[...], cos, sin, first_half).astype(BF16)
    vt_ref[...] = _nt(wvt_ref[...], h).astype(BF16)

    hq_ref[...] = jax.nn.silu(proj(C_HQ, C_ZF)).astype(BF16)

    def forget(z, lb_ref, k_out, g_out):
        a2 = lb_ref[...]
        e2 = jnp.exp(a2 - jnp.max(a2, axis=0, keepdims=True))
        lb = e2[0:1, :] / jnp.sum(e2, axis=0, keepdims=True)
        f = lb + (1.0 - lb) * jax.nn.sigmoid(z)
        g_out[...] = jnp.log(f)
        k_out[...] = ((1.0 - lb) * jax.nn.sigmoid(-z)).astype(BF16)

    forget(proj(C_ZF, C_ZB), lbf_ref, kf_ref, gf_ref)
    forget(proj(C_ZB, C_HI), lbb_ref, kb_ref, gb_ref)
    vi_ref[...] = proj(C_HI, C_HG).astype(BF16)
    vit_ref[...] = _nt(wvit_ref[...], h).astype(BF16)
    og_ref[...] = jax.nn.silu(proj(C_HG, C_MQ)).astype(BF16)
    a = proj(C_MQ, C_END)
    for hh in range(MEM_HEADS):
        sl = slice(hh * HEAD_DIM, (hh + 1) * HEAD_DIM)
        mq_ref[:, sl] = (_rms(a[:, sl], gmq_ref[...]) * scale).astype(BF16)


def _in_proj(x, g_mix, w_bf, wvt, wvit, cos, sin_signed, g_q, g_k, g_mq, lb_f, lb_b, tm):
    n = x.shape[0]
    grid = (n // tm,)
    row = lambda w: pl.BlockSpec((tm, w), lambda i: (i, 0))
    col = lambda h: pl.BlockSpec((h, tm), lambda i: (0, i))
    full = lambda a: pl.BlockSpec(a.shape, lambda i: (0,) * a.ndim)
    res = lambda a: pl.BlockSpec(a.shape, lambda i: (0,) * a.ndim, pipeline_mode=pl.Buffered(1))
    sd = jax.ShapeDtypeStruct
    out_shape = (sd((n, 1024), BF16), sd((n, 256), BF16), sd((256, n), BF16),
                 sd((n, 512), BF16), sd((n, 512), BF16), sd((n, 512), F32), sd((n, 512), BF16), sd((n, 512), F32),
                 sd((n, 512), BF16), sd((512, n), BF16), sd((n, 512), BF16), sd((n, 512), BF16))
    out_specs = (row(1024), row(256), col(256), row(512), row(512), row(512), row(512), row(512),
                 row(512), col(512), row(512), row(512))
    return pl.pallas_call(
        _inproj_kernel,
        grid=grid,
        in_specs=[row(D_MODEL), full(g_mix), res(w_bf), res(wvt), res(wvit), row(HEAD_DIM), row(HEAD_DIM),
                  full(g_q), full(g_k), full(g_mq), full(lb_f), full(lb_b)],
        out_specs=out_specs,
        out_shape=out_shape,
        scratch_shapes=[pltpu.VMEM((tm, D_MODEL), BF16)],
        name="in_proj",
        compiler_params=_cparams(("parallel",)),
    )(x, g_mix, w_bf, wvt, wvit, cos, sin_signed, g_q, g_k, g_mq, lb_f, lb_b)


ONES_ROWS = 16


def _attn_kernel(q_ref, k_ref, vt_ref, o_ref, acc_ref, s_ref, *, tk, nk):
    tq = q_ref.shape[0]
    acc_ref[...] = jnp.zeros_like(acc_ref)
    ones = jnp.ones((ONES_ROWS, tk), BF16)

    def scores(hh, off):
        s_ref[hh] = _nt(k_ref[pl.ds(off, tk), :], q_ref[:, hh * HEAD_DIM:(hh + 1) * HEAD_DIM])

    lead = 2
    for hh in range(lead):
        scores(hh, 0)

    def body(i, ms):
        off = pl.multiple_of(i * tk, tk)
        off_next = pl.multiple_of(jnp.minimum(i + 1, nk - 1) * tk, tk)
        vx = jnp.concatenate([vt_ref[:, pl.ds(off, tk)], ones], axis=0)
        out = []
        for hh in range(KV_GROUP):
            nxt = hh + lead
            scores(nxt % KV_GROUP, off if nxt < KV_GROUP else off_next)
            s = s_ref[hh]
            m_new = jnp.maximum(ms[hh], jnp.max(s, axis=0, keepdims=True))
            p = jnp.exp2((s - m_new).astype(BF16))
            alpha = jnp.exp2(ms[hh] - m_new)
            acc_ref[hh] = alpha * acc_ref[hh] + _nn(vx, p)
            out.append(m_new)
        return tuple(out)

    init = tuple(jnp.full((1, tq), NEG_BIG, F32) for _ in range(KV_GROUP))
    lax.fori_loop(0, nk, body, init, unroll=4)
    for hh in range(KV_GROUP):
        a = acc_ref[hh]
        o_ref[:, hh * HEAD_DIM:(hh + 1) * HEAD_DIM] = (a[0:HEAD_DIM] / a[HEAD_DIM:HEAD_DIM + 1]).T.astype(BF16)


def _attention(q, k, vt, tq, tk):
    s = q.shape[0]
    width = KV_GROUP * HEAD_DIM
    return pl.pallas_call(
        functools.partial(_attn_kernel, tk=tk, nk=s // tk),
        grid=(ATTN_KV_HEADS, s // tq),
        in_specs=[pl.BlockSpec((tq, width), lambda g, i: (i, g)),
                  pl.BlockSpec((s, HEAD_DIM), lambda g, i: (0, g)),
                  pl.BlockSpec((HEAD_DIM, s), lambda g, i: (g, 0))],
        out_specs=pl.BlockSpec((tq, width), lambda g, i: (i, g)),
        out_shape=jax.ShapeDtypeStruct((s, ATTN_HEADS * HEAD_DIM), BF16),
        scratch_shapes=[pltpu.VMEM((KV_GROUP, HEAD_DIM + ONES_ROWS, tq), F32), pltpu.VMEM((KV_GROUP, tk, tq), F32)],
        name="attention",
        compiler_params=_cparams(("parallel", "parallel")),
    )(q, k, vt)


def _hgrn_levels():
    b, out = HG_CHUNK, []
    while b >= 2:
        out.append(b)
        b //= 2
    return out


def _hgrn_chunk(q, k, g, v, vt, state_ref, rev):
    c = HG_CHUNK
    w = HG_WIDTH
    row = lax.broadcasted_iota(I32, (c, w), 0)
    b = g
    sh = 1
    while sh < c:
        if rev:
            b = b + jnp.where(row < c - sh, pltpu.roll(b, c - sh, 0), 0.0)
        else:
            b = b + jnp.where(row >= sh, pltpu.roll(b, sh, 0), 0.0)
        sh *= 2
    qf = q.astype(F32)
    kf = k.astype(F32)
    ti = lax.broadcasted_iota(I32, (c, c), 0)
    si = lax.broadcasted_iota(I32, (c, c), 1)
    att = [jnp.where(ti == si, _nt(q[:, h * HEAD_DIM:(h + 1) * HEAD_DIM], k[:, h * HEAD_DIM:(h + 1) * HEAD_DIM]), 0.0)
           for h in range(HG_HEADS)]
    for blk in _hgrn_levels():
        half = blk // 2
        ref_row = half if rev else half - 1
        pos = row % blk
        if blk >= 8:
            r = jnp.concatenate(
                [jnp.broadcast_to(b[s0 + ref_row:s0 + ref_row + 1, :], (blk, w)) for s0 in range(0, c, blk)], axis=0)
        else:
            r = b
            for d in range(-ref_row, blk - ref_row):
                if d != 0:
                    r = jnp.where(pos - ref_row == d, pltpu.roll(b, d % c, 0), r)
        e = jnp.exp(-jnp.abs(b - r))
        is_q = (pos < half) if rev else (pos >= half)
        ql = jnp.where(is_q, qf * e, 0.0).astype(BF16)
        kl = jnp.where(is_q, 0.0, kf * e).astype(BF16)
        same = (ti // blk) == (si // blk)
        for h in range(HG_HEADS):
            sl = slice(h * HEAD_DIM, (h + 1) * HEAD_DIM)
            att[h] = att[h] + jnp.where(same, _nt(ql[:, sl], kl[:, sl]), 0.0)
    b_end = b[0:1, :] if rev else b[c - 1:c, :]
    qd = (qf * jnp.exp(b)).astype(BF16)
    kd = (kf * jnp.exp(b_end - b)).astype(BF16)
    dec = jnp.exp(b_end)
    outs = []
    for h in range(HG_HEADS):
        sl = slice(h * HEAD_DIM, (h + 1) * HEAD_DIM)
        st = state_ref[h]
        o = _nn(att[h].astype(BF16), v[:, sl]) + _nt(qd[:, sl], st.astype(BF16))
        state_ref[h] = dec[:, sl] * st + _nn(vt[sl, :], kd[:, sl])
        outs.append(o)
    return jnp.concatenate(outs, axis=1)


def _hgrn_fwd_kernel(q_ref, k_ref, g_ref, v_ref, vt_ref, o_ref, state_ref, *, nch):
    @pl.when(pl.program_id(0) == 0)
    def _():
        state_ref[...] = jnp.zeros_like(state_ref)

    def body(i, carry):
        r0 = pl.multiple_of(i * HG_CHUNK, HG_CHUNK)
        rs = pl.ds(r0, HG_CHUNK)
        o_ref[rs, :] = _hgrn_chunk(q_ref[rs, :], k_ref[rs, :], g_ref[rs, :], v_ref[rs, :], vt_ref[:, rs], state_ref, False)
        return carry

    lax.fori_loop(0, nch, body, 0)


def _hgrn_bwd_kernel(q_ref, k_ref, g_ref, v_ref, vt_ref, of_ref, og_ref, gout_ref, o_ref, state_ref, *, nch):
    @pl.when(pl.program_id(0) == 0)
    def _():
        state_ref[...] = jnp.zeros_like(state_ref)

    def body(i, carry):
        r0 = pl.multiple_of((nch - 1 - i) * HG_CHUNK, HG_CHUNK)
        rs = pl.ds(r0, HG_CHUNK)
        o = _hgrn_chunk(q_ref[rs, :], k_ref[rs, :], g_ref[rs, :], v_ref[rs, :], vt_ref[:, rs], state_ref, True)
        o = o + of_ref[rs, :]
        og = og_ref[rs, :].astype(F32)
        for h in range(HG_HEADS):
            sl = slice(h * HEAD_DIM, (h + 1) * HEAD_DIM)
            o_ref[rs, sl] = (_rms(o[:, sl], gout_ref[...]) * og[:, sl]).astype(BF16)
        return carry

    lax.fori_loop(0, nch, body, 0)


def _hgrn(hq, kf, gf, kb, gb, vi, vit, og, g_out, tb):
    n = hq.shape[0]
    nblk = n // tb
    nch = tb // HG_CHUNK
    state = pltpu.VMEM((HG_HEADS, HEAD_DIM, HEAD_DIM), F32)
    fr = lambda i: (i, 0)
    fc = lambda i: (0, i)
    o_f = pl.pallas_call(
        functools.partial(_hgrn_fwd_kernel, nch=nch),
        grid=(nblk,),
        in_specs=[pl.BlockSpec((tb, HG_WIDTH), fr)] * 4 + [pl.BlockSpec((HG_WIDTH, tb), fc)],
        out_specs=pl.BlockSpec((tb, HG_WIDTH), fr),
        out_shape=jax.ShapeDtypeStruct((n, HG_WIDTH), F32),
        scratch_shapes=[state],
        name="hgrn_fwd",
        compiler_params=_cparams(("arbitrary",)),
    )(hq, kf, gf, vi, vit)
    br = lambda i: (nblk - 1 - i, 0)
    bc = lambda i: (0, nblk - 1 - i)
    return pl.pallas_call(
        functools.partial(_hgrn_bwd_kernel, nch=nch),
        grid=(nblk,),
        in_specs=[pl.BlockSpec((tb, HG_WIDTH), br)] * 4 + [pl.BlockSpec((HG_WIDTH, tb), bc)]
        + [pl.BlockSpec((tb, HG_WIDTH), br)] * 2 + [pl.BlockSpec(g_out.shape, lambda i: (0, 0))],
        out_specs=pl.BlockSpec((tb, HG_WIDTH), br),
        out_shape=jax.ShapeDtypeStruct((n, HG_WIDTH), BF16),
        scratch_shapes=[state],
        name="hgrn_bwd",
        compiler_params=_cparams(("arbitrary",)),
    )(hq, kb, gb, vi, vit, o_f, og, g_out)


def _outproj_kernel(x_ref, attn_ref, hgrn_ref, mq_ref, km_ref, vm_ref, w_ref, gffn_ref, wrt_ref,
                    y_ref, h2_ref, aff_ref, mix_ref):
    n_attn = ATTN_HEADS * HEAD_DIM
    mix_ref[:, 0:n_attn] = attn_ref[...]
    mix_ref[:, n_attn:n_attn + HG_WIDTH] = hgrn_ref[...]
    for hh in range(MEM_HEADS):
        sl = slice(hh * HEAD_DIM, (hh + 1) * HEAD_DIM)
        s = _nt(mq_ref[:, sl], km_ref[:, sl])
        p = jnp.exp(s - jnp.max(s, axis=-1, keepdims=True))
        p = p / jnp.sum(p, axis=-1, keepdims=True)
        c0 = n_attn + HG_WIDTH + hh * HEAD_DIM
        mix_ref[:, c0:c0 + HEAD_DIM] = _nn(p.astype(BF16), vm_ref[:, sl]).astype(BF16)
    y = x_ref[...] + _nn(mix_ref[...], w_ref[...])
    y_ref[...] = y
    h2 = _rms(y, gffn_ref[...])
    half = D_MODEL // 2
    lo = lax.bitcast_convert_type(h2[:, :half].astype(BF16).astype(F32), U32) >> 16
    hi = lax.bitcast_convert_type(h2[:, half:].astype(BF16).astype(F32), U32) & jnp.uint32(0xFFFF0000)
    h2_ref[...] = hi | lo
    logits = lax.dot_general(wrt_ref[...], h2, (((1,), (1,)), ((), ())), preferred_element_type=F32,
                             precision=lax.Precision.HIGHEST)
    e = jnp.exp(logits - jnp.max(logits, axis=0, keepdims=True))
    aff_ref[...] = e / jnp.sum(e, axis=0, keepdims=True)


def _out_proj(x, attn, hgrn, mq, km, vm, w_out_bf, g_ffn, w_router_t, tm):
    n = x.shape[0]
    row = lambda w: pl.BlockSpec((tm, w), lambda i: (i, 0))
    full = lambda a: pl.BlockSpec(a.shape, lambda i: (0,) * a.ndim)
    res = lambda a: pl.BlockSpec(a.shape, lambda i: (0,) * a.ndim, pipeline_mode=pl.Buffered(1))
    sd = jax.ShapeDtypeStruct
    return pl.pallas_call(
        _outproj_kernel,
        grid=(n // tm,),
        in_specs=[row(D_MODEL), row(1024), row(512), row(512), full(km), full(vm), res(w_out_bf), full(g_ffn),
                  full(w_router_t)],
        out_specs=(row(D_MODEL), row(D_MODEL // 2), pl.BlockSpec((N_EXPERTS, tm), lambda i: (0, i))),
        out_shape=(sd((n, D_MODEL), F32), sd((n, D_MODEL // 2), U32), sd((N_EXPERTS, n), F32)),
        scratch_shapes=[pltpu.VMEM((tm, D_MODEL), BF16)],
        name="out_proj",
        compiler_params=_cparams(("parallel",)),
    )(x, attn, hgrn, mq, km, vm, w_out_bf, g_ffn, w_router_t)


def _split3(x):
    hi = x.astype(BF16)
    r1 = x - hi.astype(F32)
    mid = r1.astype(BF16)
    lo = (r1 - mid.astype(F32)).astype(BF16)
    return hi, mid, lo


def _topk_kernel(aff_ref, afft_ref, idx_ref, gate_ref, pos_ref, off_ref, cl_scr, offb_scr, totb_scr, *, cap):
    e_n, nb, _ = aff_ref.shape
    rows = e_n * nb
    aff = aff_ref[...]
    keys = lax.bitcast_convert_type(aff, I32)

    def count(mask):
        s = jnp.sum(mask.astype(F32), axis=1, keepdims=True)
        return jnp.sum(s, axis=2, keepdims=True)

    def bis(i, t):
        cand = t | (jnp.int32(1) << (30 - i))
        return jnp.where(count(keys >= cand) >= cap, cand, t)

    thr = lax.fori_loop(0, 31, bis, jnp.zeros((e_n, 1, 1), I32))
    gt = keys > thr
    eq = keys == thr
    need = cap - count(gt)

    upper = _ind(lax.broadcasted_iota(I32, (LANES, LANES), 0) <= lax.broadcasted_iota(I32, (LANES, LANES), 1))
    ones = jnp.ones((LANES, LANES), BF16)
    bi = lax.broadcasted_iota(I32, (nb, nb), 0)
    bj = lax.broadcasted_iota(I32, (nb, nb), 1)
    strict_lower = _ind(bj < bi)

    def prefix(mask):
        m2 = _ind(mask).reshape(rows, LANES)
        cl = _nn(m2, upper).reshape(e_n, nb, LANES)
        tot = _nn(m2, ones).reshape(e_n, nb, LANES)
        off = jnp.stack([_nn(strict_lower, tot[e].astype(BF16)) for e in range(e_n)], axis=0)
        return cl, off, tot

    cl, off, _ = prefix(eq)
    rank_eq = off + cl - eq.astype(F32)
    sel = gt | (eq & (rank_eq < need))
    cl, off, tot = prefix(sel)
    pos_ref[...] = jnp.where(sel, off + cl - 1.0, -1.0).astype(I32)
    off_ref[...] = off.astype(I32)
    cl_scr[...] = cl
    offb_scr[...] = off
    totb_scr[...] = tot

    lower_incl = _ind(lax.broadcasted_iota(I32, (LANES, LANES), 1) <= lax.broadcasted_iota(I32, (LANES, LANES), 0))
    reps = cap // LANES
    s_row = lax.broadcasted_iota(I32, (nb, cap), 1).astype(F32)
    b_col = lax.broadcasted_iota(I32, (nb, cap), 0).astype(F32)
    j_col = lax.broadcasted_iota(I32, (LANES, cap), 0).astype(F32)

    def per_expert(e, carry):
        sel_e = _ind(pos_ref[e] >= 0)
        clt = _nt(lower_incl, sel_e)
        offt = jnp.concatenate([offb_scr[e]] * reps, axis=1)
        endt = offt + jnp.concatenate([totb_scr[e]] * reps, axis=1)
        hit = (offt <= s_row) & (s_row < endt)
        onehot = _ind(hit)
        g_cnt = _nn(clt.astype(BF16), onehot)
        local = s_row[0:1, :] - jnp.sum(jnp.where(hit, offt, 0.0), axis=0, keepdims=True)
        j_row = jnp.sum((g_cnt <= local).astype(F32), axis=0, keepdims=True)
        b_row = jnp.sum(jnp.where(hit, b_col, 0.0), axis=0, keepdims=True)
        idx_ref[e] = (b_row * LANES + j_row).astype(I32)
        hi, mid, lo = _split3(afft_ref[e])
        g_aff = _nn(hi, onehot) + _nn(mid, onehot) + _nn(lo, onehot)
        gate_ref[e] = jnp.sum(jnp.where(j_col == j_row, g_aff, 0.0), axis=0, keepdims=True)
        return carry

    lax.fori_loop(0, e_n, per_expert, 0)


def _topk(aff_t, cap):
    e_n, n = aff_t.shape
    nb = n // LANES
    aff3 = aff_t.reshape(e_n, nb, LANES)
    afft3 = jnp.swapaxes(aff3, 1, 2)
    sd = jax.ShapeDtypeStruct
    return pl.pallas_call(
        functools.partial(_topk_kernel, cap=cap),
        out_shape=(sd((e_n, 1, cap), I32), sd((e_n, 1, cap), F32), sd((e_n, nb, LANES), I32), sd((e_n, nb, LANES), I32)),
        scratch_shapes=[pltpu.VMEM((e_n, nb, LANES), F32)] * 3,
        name="topk",
        compiler_params=_cparams(None),
    )(aff3, afft3)


GATHER_RING = 128


GATHER_ROWS = 512


def _gather_kernel(idx_ref, src_ref, out_ref, sem):
    rows = out_ref.shape[0]

    def copy(s, k):
        return pltpu.make_async_copy(src_ref.at[pl.ds(idx_ref[0, 0, s], 1)], out_ref.at[pl.ds(s, 1)], sem.at[k])

    def group(gi, carry):
        for k in range(GATHER_RING):
            s = gi * GATHER_RING + k

            @pl.when(gi > 0)
            def _():
                copy(s - GATHER_RING, k).wait()

            copy(s, k).start()
        return carry

    lax.fori_loop(0, rows // GATHER_RING, group, 0)
    for k in range(GATHER_RING):
        copy(rows - GATHER_RING + k, k).wait()


def _gather(h2w, idx, cap):
    total = N_EXPERTS * cap
    rows = min(GATHER_ROWS, cap)
    assert total % rows == 0 and rows % GATHER_RING == 0
    width = h2w.shape[1]
    out = pl.pallas_call(
        _gather_kernel,
        grid=(total // rows,),
        in_specs=[pl.BlockSpec((1, 1, rows), lambda i: (i, 0, 0), memory_space=pltpu.SMEM),
                  pl.BlockSpec(memory_space=pl.ANY)],
        out_specs=pl.BlockSpec((rows, width), lambda i: (i, 0)),
        out_shape=jax.ShapeDtypeStruct((total, width), U32),
        scratch_shapes=[pltpu.SemaphoreType.DMA((GATHER_RING,))],
        name="gather",
        compiler_params=_cparams(("arbitrary",)),
    )(idx.reshape(total // rows, 1, rows), h2w)
    return out.reshape(N_EXPERTS, cap, width)


def _ffn_kernel(x_ref, gate_ref, wg_ref, wu_ref, wd_ref, o_ref, acc_ref, xb_ref):
    f = pl.program_id(2)

    @pl.when(f == 0)
    def _():
        w = x_ref[0]
        half = w.shape[1]
        xb_ref[:, :half] = lax.bitcast_convert_type(w << 16, F32).astype(BF16)
        xb_ref[:, half:] = lax.bitcast_convert_type(w & jnp.uint32(0xFFFF0000), F32).astype(BF16)

    x = xb_ref[...]
    hid = jax.nn.silu(_nn(x, wg_ref[0].astype(BF16))) * _nn(x, wu_ref[0].astype(BF16))
    part = _nn(hid.astype(BF16), wd_ref[0].astype(BF16))

    @pl.when(f == 0)
    def _():
        acc_ref[...] = part

    @pl.when(f > 0)
    def _():
        acc_ref[...] += part

    @pl.when(f == pl.num_programs(2) - 1)
    def _():
        o_ref[0] = (acc_ref[...] * gate_ref[0]).astype(BF16)


def _ffn(xs, gate_col, wg, wu, wd, ts, tf):
    e_n, cap, dw = xs.shape
    d = 2 * dw
    return pl.pallas_call(
        _ffn_kernel,
        grid=(e_n, cap // ts, D_EXPERT // tf),
        in_specs=[pl.BlockSpec((1, ts, dw), lambda e, s, f: (e, s, 0)),
                  pl.BlockSpec((1, ts, 1), lambda e, s, f: (e, s, 0)),
                  pl.BlockSpec((1, d, tf), lambda e, s, f: (e, 0, f)),
                  pl.BlockSpec((1, d, tf), lambda e, s, f: (e, 0, f)),
                  pl.BlockSpec((1, tf, d), lambda e, s, f: (e, f, 0))],
        out_specs=pl.BlockSpec((1, ts, d), lambda e, s, f: (e, s, 0)),
        out_shape=jax.ShapeDtypeStruct((e_n, cap, d), BF16),
        scratch_shapes=[pltpu.VMEM((ts, d), F32), pltpu.VMEM((ts, d), BF16)],
        name="ffn",
        compiler_params=_cparams(("parallel", "parallel", "arbitrary")),
    )(xs, gate_col, wg, wu, wd)


ROW_GROUP = 16
WIN_GROUPS = LANES // ROW_GROUP + 1
WIN = WIN_GROUPS * ROW_GROUP


def _combine_kernel(off_ref, y1_ref, post_ref, rows_ref, o_ref, buf, sem, *, cap):
    b = pl.program_id(0)
    nb = pl.num_programs(0)

    def first_group(blk, e):
        return jnp.minimum(off_ref[e, blk] // ROW_GROUP, cap // ROW_GROUP - WIN_GROUPS)

    def copy(blk, slot, e):
        return pltpu.make_async_copy(rows_ref.at[e, pl.ds(first_group(blk, e), WIN_GROUPS)],
                                     buf.at[slot, pl.ds(e * WIN_GROUPS, WIN_GROUPS)], sem.at[slot, e])

    def start(blk, slot):
        for e in range(N_EXPERTS):
            copy(blk, slot, e).start()

    def wait(blk, slot):
        for e in range(N_EXPERTS):
            copy(blk, slot, e).wait()

    slot = b % 2

    @pl.when(b == 0)
    def _():
        start(0, 0)

    @pl.when(b + 1 < nb)
    def _():
        start(b + 1, 1 - slot)

    post = post_ref[...]
    lane = lax.broadcasted_iota(I32, (LANES, LANES), 1)

    def rel(e):
        return jnp.broadcast_to(post[:, e:e + 1], (LANES, LANES)) - first_group(b, e) * ROW_GROUP

    pieces = []
    for t in range(N_EXPERTS * WIN // LANES):
        e0 = t * LANES // WIN
        miss = rel(e0) - (lane + (t * LANES - e0 * WIN))
        if (e0 + 1) * WIN < (t + 1) * LANES:
            j1 = lane + (t * LANES - (e0 + 1) * WIN)
            miss = jnp.where(j1 >= 0, rel(e0 + 1) - j1, miss)
        pieces.append(_ind(miss == 0))
    w = jnp.concatenate(pieces, axis=1)
    wait(b, slot)
    o_ref[...] = y1_ref[...] + _nn(w, buf[slot].reshape(N_EXPERTS * WIN, o_ref.shape[1]))


def _combine(y1, pos_t, off, rows, cap):
    n = y1.shape[0]
    nb = n // LANES
    d = y1.shape[1]
    grid_spec = pltpu.PrefetchScalarGridSpec(
        num_scalar_prefetch=1,
        grid=(nb,),
        in_specs=[pl.BlockSpec((LANES, d), lambda b, off: (b, 0)),
                  pl.BlockSpec((LANES, N_EXPERTS), lambda b, off: (b, 0)),
                  pl.BlockSpec(memory_space=pl.ANY)],
        out_specs=pl.BlockSpec((LANES, d), lambda b, off: (b, 0)),
        scratch_shapes=[pltpu.VMEM((2, N_EXPERTS * WIN_GROUPS, ROW_GROUP, d), BF16),
                        pltpu.SemaphoreType.DMA((2, N_EXPERTS))],
    )
    assert cap % ROW_GROUP == 0 and cap >= WIN
    rows = rows.reshape(N_EXPERTS, cap // ROW_GROUP, ROW_GROUP, d)
    return pl.pallas_call(
        functools.partial(_combine_kernel, cap=cap),
        grid_spec=grid_spec,
        out_shape=jax.ShapeDtypeStruct((n, d), F32),
        name="combine",
        compiler_params=_cparams(("arbitrary",)),
    )(off, y1, pos_t, rows)


def _rope_tables(seq_len):
    t = jnp.arange(seq_len)
    row = (t // GRID_W).astype(F32)
    col = (t % GRID_W).astype(F32)
    inv = ROPE_THETA ** (-jnp.arange(ROPE_PAIRS, dtype=F32) / ROPE_PAIRS)
    ang = jnp.stack([row[:, None] * inv, col[:, None] * inv], axis=1)
    ang = jnp.broadcast_to(ang[:, :, None, :], (seq_len, 2, 2, ROPE_PAIRS)).reshape(seq_len, HEAD_DIM)
    first_half = (jnp.arange(HEAD_DIM) % (2 * ROPE_PAIRS)) < ROPE_PAIRS
    return jnp.cos(ang), jnp.where(first_half[None, :], -jnp.sin(ang), jnp.sin(ang))


def _pick(n, pref):
    t = min(n, pref)
    assert n % t == 0, (n, pref)
    return t


def _layer(x, mem, p):
    n = x.shape[0]
    cos, sin_signed = _rope_tables(n)
    km, vm = _mem_kv(mem, p["g_mem"], p["w_mem_kv"], p["g_mk"])
    (q, k, vt, hq, kf, gf, kb, gb, vi, vit, og, mq) = _in_proj(
        x, p["g_mix"], p["w_in"], p["wvt"], p["wvit"], cos, sin_signed, p["g_q"], p["g_k"], p["g_mq"],
        p["lb_fwd"], p["lb_bwd"], _pick(n, 256))
    attn = _attention(q, k, vt, _pick(n, 256), _pick(n, 1024))
    hgrn = _hgrn(hq, kf, gf, kb, gb, vi, vit, og, p["g_hg_out"], _pick(n, 512))
    y1, h2, aff_t = _out_proj(x, attn, hgrn, mq, km, vm, p["w_out"], p["g_ffn"], p["w_router_t"], _pick(n, 512))
    cap = EXPERT_CAPACITY_FACTOR * n // N_EXPERTS
    idx, gate, pos, off = _topk(aff_t, cap)
    xs = _gather(h2, idx, cap)
    gate_col = gate.reshape(N_EXPERTS, cap, 1)
    rows = _ffn(xs, gate_col, p["w_gate"], p["w_up"], p["w_down"], _pick(cap, 1024), 256)
    pos_t = pos.reshape(N_EXPERTS, n).T
    off_s = off[:, :, 0]
    return _combine(y1, pos_t, off_s, rows, cap)


def kernel(x_prompt, x_sample, mem_prompt, mem_sample, g_mix, w_in, g_q, g_k, g_hg_out, lb_fwd, lb_bwd, g_mem, w_mem_kv,
           g_mq, g_mk, w_out, g_ffn, w_router, w_gate, w_up, w_down):
    assert g_mix.shape[0] == 1 and lb_fwd.shape[0] == 2, "single layer: lower bound is the first cumulative-softmax row"
    w_in_bf = w_in[0].astype(BF16)
    p = {
        "g_mix": g_mix, "w_in": w_in_bf,
        "wvt": w_in_bf[:, C_V:C_HQ].T, "wvit": w_in_bf[:, C_HI:C_HG].T,
        "g_q": g_q, "g_k": g_k, "g_hg_out": g_hg_out, "lb_fwd": lb_fwd, "lb_bwd": lb_bwd,
        "g_mem": g_mem, "w_mem_kv": w_mem_kv[0].astype(BF16), "g_mq": g_mq, "g_mk": g_mk,
        "w_out": w_out[0].astype(BF16), "g_ffn": g_ffn, "w_router_t": w_router[0].T,
        "w_gate": w_gate[0], "w_up": w_up[0], "w_down": w_down[0],
    }
    y_prompt = _layer(x_prompt[0], mem_prompt[0], p)
    y_sample = _layer(x_sample[0], mem_sample[0], p)
    return (y_prompt[None], y_sample[None])
```

```python
import functools

import jax
import jax.numpy as jnp
from jax import lax
from jax.experimental import pallas as pl
from jax.experimental.pallas import tpu as pltpu

F32 = jnp.float32
BF16 = jnp.bfloat16
I32 = jnp.int32
U32 = jnp.uint32

D_MODEL = 2048
HEAD_DIM = 128
ATTN_HEADS = 8
ATTN_KV_HEADS = 2
KV_GROUP = ATTN_HEADS // ATTN_KV_HEADS
HG_HEADS = 4
HG_WIDTH = HG_HEADS * HEAD_DIM
MEM_HEADS = 4
GRID_W = 64
ROPE_THETA = 10000.0
ROPE_PAIRS = HEAD_DIM // 4
N_EXPERTS = 16
EXPERT_CAPACITY_FACTOR = 2
D_EXPERT = 2048
EPS = 1e-6
HG_CHUNK = 128
LANES = 128
NEG_BIG = -1e30
LOG2E = 1.4426950408889634

C_Q, C_K, C_V, C_HQ, C_ZF, C_ZB, C_HI, C_HG, C_MQ, C_END = 0, 1024, 1280, 1536, 2048, 2560, 3072, 3584, 4096, 4608

VMEM_LIMIT = 56 * 1024 * 1024


def _cparams(sem, vmem=VMEM_LIMIT):
    return pltpu.CompilerParams(dimension_semantics=sem, vmem_limit_bytes=vmem)


def _nt(a, b):
    return lax.dot_general(a, b, (((1,), (1,)), ((), ())), preferred_element_type=F32)


def _nn(a, b):
    return jnp.dot(a, b, preferred_element_type=F32)


def _ind(mask):
    return jnp.where(mask, 1.0, 0.0).astype(BF16)


def _rms(x, g):
    return x * lax.rsqrt(jnp.mean(x * x, axis=-1, keepdims=True) + EPS) * g


def _memkv_kernel(mem_ref, gmem_ref, w_ref, gmk_ref, km_ref, vm_ref):
    h = _rms(mem_ref[...], gmem_ref[...]).astype(BF16)
    kv = _nn(h, w_ref[...])
    for hh in range(MEM_HEADS):
        sl = slice(hh * HEAD_DIM, (hh + 1) * HEAD_DIM)
        km_ref[:, sl] = _rms(kv[:, sl], gmk_ref[...]).astype(BF16)
    vm_ref[...] = kv[:, MEM_HEADS * HEAD_DIM:].astype(BF16)


def _mem_kv(mem, g_mem, w_mem_kv_bf, g_mk):
    m = mem.shape[0]
    width = MEM_HEADS * HEAD_DIM
    return pl.pallas_call(
        _memkv_kernel,
        out_shape=(jax.ShapeDtypeStruct((m, width), BF16), jax.ShapeDtypeStruct((m, width), BF16)),
        name="mem_kv",
        compiler_params=_cparams(None),
    )(mem, g_mem, w_mem_kv_bf, g_mk)


def _norm_rope(a, g, cos, sin_signed, first_half):
    y = _rms(a, g)
    rot = jnp.where(first_half, pltpu.roll(y, HEAD_DIM - ROPE_PAIRS, 1), pltpu.roll(y, ROPE_PAIRS, 1))
    return y * cos + rot * sin_signed


def _inproj_kernel(x_ref, gmix_ref, w_ref, wvt_ref, wvit_ref, cos_ref, sin_ref, gq_ref, gk_ref, gmq_ref, lbf_ref, lbb_ref,
                   q_ref, k_ref, vt_ref, hq_ref, kf_ref, gf_ref, kb_ref, gb_ref, vi_ref, vit_ref, og_ref, mq_ref, h_scr):
    tm = x_ref.shape[0]
    h_scr[...] = _rms(x_ref[...], gmix_ref[...]).astype(BF16)
    h = h_scr[...]
    cos = cos_ref[...]
    sin = sin_ref[...]
    lane = lax.broadcasted_iota(I32, (tm, HEAD_DIM), 1)
    first_half = (lane % (2 * ROPE_PAIRS)) < ROPE_PAIRS
    scale = HEAD_DIM ** -0.5

    def proj(c0, c1):
        return _nn(h, w_ref[:, c0:c1])

    for c in range(ATTN_HEADS // 4):
        a = proj(C_Q + 512 * c, C_Q + 512 * (c + 1))
        for hh in range(4):
            sl = slice(hh * HEAD_DIM, (hh + 1) * HEAD_DIM)
            y = _norm_rope(a[:, sl], gq_ref[...], cos, sin, first_half) * (scale * LOG2E)
            q_ref[:, 512 * c + hh * HEAD_DIM:512 * c + (hh + 1) * HEAD_DIM] = y.astype(BF16)
    a = proj(C_K, C_V)
    for hh in range(ATTN_KV_HEADS):
        sl = slice(hh * HEAD_DIM, (hh + 1) * HEAD_DIM)
        k_ref[:, sl] = _norm_rope(a[:, sl], gk_ref[...], cos, sin, first_half).astype(BF16)
    vt_ref[...] = _nt(wvt_ref[...], h).astype(BF16)

    hq_ref[...] = jax.nn.silu(proj(C_HQ, C_ZF)).astype(BF16)

    def forget(z, lb_ref, k_out, g_out):
        a2 = lb_ref[...]
        e2 = jnp.exp(a2 - jnp.max(a2, axis=0, keepdims=True))
        lb = e2[0:1, :] / jnp.sum(e2, axis=0, keepdims=True)
        f = lb + (1.0 - lb) * jax.nn.sigmoid(z)
        g_out[...] = jnp.log(f) * LOG2E
        k_out[...] = ((1.0 - lb) * jax.nn.sigmoid(-z)).astype(BF16)

    forget(proj(C_ZF, C_ZB), lbf_ref, kf_ref, gf_ref)
    forget(proj(C_ZB, C_HI), lbb_ref, kb_ref, gb_ref)
    vi_ref[...] = proj(C_HI, C_HG).astype(BF16)
    vit_ref[...] = _nt(wvit_ref[...], h).astype(BF16)
    og_ref[...] = jax.nn.silu(proj(C_HG, C_MQ)).astype(BF16)
    a = proj(C_MQ, C_END)
    for hh in range(MEM_HEADS):
        sl = slice(hh * HEAD_DIM, (hh + 1) * HEAD_DIM)
        mq_ref[:, sl] = (_rms(a[:, sl], gmq_ref[...]) * scale).astype(BF16)


def _in_proj(x, g_mix, w_bf, wvt, wvit, cos, sin_signed, g_q, g_k, g_mq, lb_f, lb_b, tm):
    n = x.shape[0]
    grid = (n // tm,)
    row = lambda w: pl.BlockSpec((tm, w), lambda i: (i, 0))
    col = lambda h: pl.BlockSpec((h, tm), lambda i: (0, i))
    full = lambda a: pl.BlockSpec(a.shape, lambda i: (0,) * a.ndim)
    res = lambda a: pl.BlockSpec(a.shape, lambda i: (0,) * a.ndim, pipeline_mode=pl.Buffered(1))
    sd = jax.ShapeDtypeStruct
    out_shape = (sd((n, 1024), BF16), sd((n, 256), BF16), sd((256, n), BF16),
                 sd((n, 512), BF16), sd((n, 512), BF16), sd((n, 512), F32), sd((n, 512), BF16), sd((n, 512), F32),
                 sd((n, 512), BF16), sd((512, n), BF16), sd((n, 512), BF16), sd((n, 512), BF16))
    out_specs = (row(1024), row(256), col(256), row(512), row(512), row(512), row(512), row(512),
                 row(512), col(512), row(512), row(512))
    return pl.pallas_call(
        _inproj_kernel,
        grid=grid,
        in_specs=[row(D_MODEL), full(g_mix), res(w_bf), res(wvt), res(wvit), row(HEAD_DIM), row(HEAD_DIM),
                  full(g_q), full(g_k), full(g_mq), full(lb_f), full(lb_b)],
        out_specs=out_specs,
        out_shape=out_shape,
        scratch_shapes=[pltpu.VMEM((tm, D_MODEL), BF16)],
        name="in_proj",
        compiler_params=_cparams(("parallel",)),
    )(x, g_mix, w_bf, wvt, wvit, cos, sin_signed, g_q, g_k, g_mq, lb_f, lb_b)


ONES_ROWS = 16


def _attn_kernel(q_ref, k_ref, vt_ref, o_ref, acc_ref, s_ref, *, tk, nk):
    tq = q_ref.shape[0]
    acc_ref[...] = jnp.zeros_like(acc_ref)
    ones = jnp.ones((ONES_ROWS, tk), BF16)

    def scores(hh, off):
        s_ref[hh] = _nt(k_ref[pl.ds(off, tk), :], q_ref[:, hh * HEAD_DIM:(hh + 1) * HEAD_DIM])

    lead = 2
    for hh in range(lead):
        scores(hh, 0)

    def body(i, ms):
        off = pl.multiple_of(i * tk, tk)
        off_next = pl.multiple_of(jnp.minimum(i + 1, nk - 1) * tk, tk)
        vx = jnp.concatenate([vt_ref[:, pl.ds(off, tk)], ones], axis=0)
        out = []
        for hh in range(KV_GROUP):
            nxt = hh + lead
            scores(nxt % KV_GROUP, off if nxt < KV_GROUP else off_next)
            s = s_ref[hh]
            m_new = jnp.maximum(ms[hh], jnp.max(s, axis=0, keepdims=True))
            p = jnp.exp2((s - m_new).astype(BF16))
            alpha = jnp.exp2(ms[hh] - m_new)
            acc_ref[hh] = alpha * acc_ref[hh] + _nn(vx, p)
            out.append(m_new)
        return tuple(out)

    init = tuple(jnp.full((1, tq), NEG_BIG, F32) for _ in range(KV_GROUP))
    lax.fori_loop(0, nk, body, init, unroll=8)
    for hh in range(KV_GROUP):
        a = acc_ref[hh]
        o_ref[:, hh * HEAD_DIM:(hh + 1) * HEAD_DIM] = (a[0:HEAD_DIM] / a[HEAD_DIM:HEAD_DIM + 1]).T.astype(BF16)


def _attention(q, k, vt, tq, tk):
    s = q.shape[0]
    width = KV_GROUP * HEAD_DIM
    return pl.pallas_call(
        functools.partial(_attn_kernel, tk=tk, nk=s // tk),
        grid=(ATTN_KV_HEADS, s // tq),
        in_specs=[pl.BlockSpec((tq, width), lambda g, i: (i, g)),
                  pl.BlockSpec((s, HEAD_DIM), lambda g, i: (0, g)),
                  pl.BlockSpec((HEAD_DIM, s), lambda g, i: (g, 0))],
        out_specs=pl.BlockSpec((tq, width), lambda g, i: (i, g)),
        out_shape=jax.ShapeDtypeStruct((s, ATTN_HEADS * HEAD_DIM), BF16),
        scratch_shapes=[pltpu.VMEM((KV_GROUP, HEAD_DIM + ONES_ROWS, tq), F32), pltpu.VMEM((KV_GROUP, tk, tq), F32)],
        name="attention",
        compiler_params=_cparams(("parallel", "parallel")),
    )(q, k, vt)


def _hgrn_levels():
    b, out = HG_CHUNK, []
    while b >= 2:
        out.append(b)
        b //= 2
    return out


def _hgrn_pair_masks(rev):
    c = HG_CHUNK
    ti = lax.broadcasted_iota(I32, (c, c), 0)
    si = lax.broadcasted_iota(I32, (c, c), 1)
    pairs = []
    for blk in _hgrn_levels():
        half = blk // 2
        same = (ti // blk) == (si // blk)
        if rev:
            m = same & (ti % blk < half) & (si % blk >= half)
        else:
            m = same & (ti % blk >= half) & (si % blk < half)
        pairs.append(jnp.where(m, 1.0, 0.0))
    return jnp.where(ti == si, 1.0, 0.0), pairs


def _hgrn_chunk(q, k, g, v, vt, state_ref, rev, masks):
    c = HG_CHUNK
    w = HG_WIDTH
    row = lax.broadcasted_iota(I32, (c, w), 0)
    b = g
    sh = 1
    while sh < c:
        if rev:
            b = b + jnp.where(row < c - sh, pltpu.roll(b, c - sh, 0), 0.0)
        else:
            b = b + jnp.where(row >= sh, pltpu.roll(b, sh, 0), 0.0)
        sh *= 2
    qf = q.astype(F32)
    kf = k.astype(F32)
    eye, pair_masks = masks
    att = [eye * _nt(q[:, h * HEAD_DIM:(h + 1) * HEAD_DIM], k[:, h * HEAD_DIM:(h + 1) * HEAD_DIM])
           for h in range(HG_HEADS)]
    for blk, pair in zip(_hgrn_levels(), pair_masks):
        half = blk // 2
        ref_row = half if rev else half - 1
        pos = row % blk
        if blk >= 8:
            r = jnp.concatenate(
                [jnp.broadcast_to(b[s0 + ref_row:s0 + ref_row + 1, :], (blk, w)) for s0 in range(0, c, blk)], axis=0)
        else:
            r = b
            for d in range(-ref_row, blk - ref_row):
                if d != 0:
                    r = jnp.where(pos - ref_row == d, pltpu.roll(b, d % c, 0), r)
        e = jnp.exp2(-jnp.abs(b - r))
        is_q = (pos < half) if rev else (pos >= half)
        x = (jnp.where(is_q, qf, kf) * e).astype(BF16)
        for h in range(HG_HEADS):
            sl = slice(h * HEAD_DIM, (h + 1) * HEAD_DIM)
            att[h] = att[h] + pair * _nt(x[:, sl], x[:, sl])
    b_end = b[0:1, :] if rev else b[c - 1:c, :]
    qd = (qf * jnp.exp2(b)).astype(BF16)
    kd = (kf * jnp.exp2(b_end - b)).astype(BF16)
    dec = jnp.exp2(b_end)
    outs = []
    for h in range(HG_HEADS):
        sl = slice(h * HEAD_DIM, (h + 1) * HEAD_DIM)
        st = state_ref[h]
        o = _nn(att[h].astype(BF16), v[:, sl]) + _nt(qd[:, sl], st.astype(BF16))
        state_ref[h] = dec[:, sl] * st + _nn(vt[sl, :], kd[:, sl])
        outs.append(o)
    return jnp.concatenate(outs, axis=1)


def _hgrn_fwd_kernel(q_ref, k_ref, g_ref, v_ref, vt_ref, o_ref, state_ref, *, nch):
    @pl.when(pl.program_id(0) == 0)
    def _():
        state_ref[...] = jnp.zeros_like(state_ref)

    masks = _hgrn_pair_masks(False)

    def body(i, carry):
        r0 = pl.multiple_of(i * HG_CHUNK, HG_CHUNK)
        rs = pl.ds(r0, HG_CHUNK)
        o_ref[rs, :] = _hgrn_chunk(q_ref[rs, :], k_ref[rs, :], g_ref[rs, :], v_ref[rs, :], vt_ref[:, rs], state_ref, False,
                                   masks)
        return carry

    lax.fori_loop(0, nch, body, 0)


def _hgrn_bwd_kernel(q_ref, k_ref, g_ref, v_ref, vt_ref, of_ref, og_ref, gout_ref, o_ref, state_ref, *, nch):
    @pl.when(pl.program_id(0) == 0)
    def _():
        state_ref[...] = jnp.zeros_like(state_ref)

    masks = _hgrn_pair_masks(True)

    def body(i, carry):
        r0 = pl.multiple_of((nch - 1 - i) * HG_CHUNK, HG_CHUNK)
        rs = pl.ds(r0, HG_CHUNK)
        o = _hgrn_chunk(q_ref[rs, :], k_ref[rs, :], g_ref[rs, :], v_ref[rs, :], vt_ref[:, rs], state_ref, True, masks)
        o = o + of_ref[rs, :]
        og = og_ref[rs, :].astype(F32)
        for h in range(HG_HEADS):
            sl = slice(h * HEAD_DIM, (h + 1) * HEAD_DIM)
            o_ref[rs, sl] = (_rms(o[:, sl], gout_ref[...]) * og[:, sl]).astype(BF16)
        return carry

    lax.fori_loop(0, nch, body, 0)


def _hgrn(hq, kf, gf, kb, gb, vi, vit, og, g_out, tb):
    n = hq.shape[0]
    nblk = n // tb
    nch = tb // HG_CHUNK
    state = pltpu.VMEM((HG_HEADS, HEAD_DIM, HEAD_DIM), F32)
    fr = lambda i: (i, 0)
    fc = lambda i: (0, i)
    o_f = pl.pallas_call(
        functools.partial(_hgrn_fwd_kernel, nch=nch),
        grid=(nblk,),
        in_specs=[pl.BlockSpec((tb, HG_WIDTH), fr)] * 4 + [pl.BlockSpec((HG_WIDTH, tb), fc)],
        out_specs=pl.BlockSpec((tb, HG_WIDTH), fr),
        out_shape=jax.ShapeDtypeStruct((n, HG_WIDTH), F32),
        scratch_shapes=[state],
        name="hgrn_fwd",
        compiler_params=_cparams(("arbitrary",)),
    )(hq, kf, gf, vi, vit)
    br = lambda i: (nblk - 1 - i, 0)
    bc = lambda i: (0, nblk - 1 - i)
    return pl.pallas_call(
        functools.partial(_hgrn_bwd_kernel, nch=nch),
        grid=(nblk,),
        in_specs=[pl.BlockSpec((tb, HG_WIDTH), br)] * 4 + [pl.BlockSpec((HG_WIDTH, tb), bc)]
        + [pl.BlockSpec((tb, HG_WIDTH), br)] * 2 + [pl.BlockSpec(g_out.shape, lambda i: (0, 0))],
        out_specs=pl.BlockSpec((tb, HG_WIDTH), br),
        out_shape=jax.ShapeDtypeStruct((n, HG_WIDTH), BF16),
        scratch_shapes=[state],
        name="hgrn_bwd",
        compiler_params=_cparams(("arbitrary",)),
    )(hq, kb, gb, vi, vit, o_f, og, g_out)


def _outproj_kernel(x_ref, attn_ref, hgrn_ref, mq_ref, km_ref, vm_ref, w_ref, gffn_ref, wrt_ref,
                    y_ref, h2_ref, aff_ref, mix_ref):
    n_attn = ATTN_HEADS * HEAD_DIM
    mix_ref[:, 0:n_attn] = attn_ref[...]
    mix_ref[:, n_attn:n_attn + HG_WIDTH] = hgrn_ref[...]
    for hh in range(MEM_HEADS):
        sl = slice(hh * HEAD_DIM, (hh + 1) * HEAD_DIM)
        s = _nt(mq_ref[:, sl], km_ref[:, sl])
        p = jnp.exp(s - jnp.max(s, axis=-1, keepdims=True))
        p = p / jnp.sum(p, axis=-1, keepdims=True)
        c0 = n_attn + HG_WIDTH + hh * HEAD_DIM
        mix_ref[:, c0:c0 + HEAD_DIM] = _nn(p.astype(BF16), vm_ref[:, sl]).astype(BF16)
    y = x_ref[...] + _nn(mix_ref[...], w_ref[...])
    y_ref[...] = y
    h2 = _rms(y, gffn_ref[...])
    half = D_MODEL // 2
    lo = lax.bitcast_convert_type(h2[:, :half].astype(BF16).astype(F32), U32) >> 16
    hi = lax.bitcast_convert_type(h2[:, half:].astype(BF16).astype(F32), U32) & jnp.uint32(0xFFFF0000)
    h2_ref[...] = hi | lo
    h_hi = h2.astype(BF16)
    h_lo = (h2 - h_hi.astype(F32)).astype(BF16)
    wr = wrt_ref[...]
    w_hi = wr.astype(BF16)
    w_mid = (wr - w_hi.astype(F32)).astype(BF16)
    two = _nt(jnp.concatenate([w_hi, w_mid], axis=0), h_hi)
    logits = two[:N_EXPERTS] + two[N_EXPERTS:] + _nt(w_hi, h_lo)
    e = jnp.exp(logits - jnp.max(logits, axis=0, keepdims=True))
    aff_ref[...] = e / jnp.sum(e, axis=0, keepdims=True)


def _out_proj(x, attn, hgrn, mq, km, vm, w_out_bf, g_ffn, w_router_t, tm):
    n = x.shape[0]
    row = lambda w: pl.BlockSpec((tm, w), lambda i: (i, 0))
    full = lambda a: pl.BlockSpec(a.shape, lambda i: (0,) * a.ndim)
    res = lambda a: pl.BlockSpec(a.shape, lambda i: (0,) * a.ndim, pipeline_mode=pl.Buffered(1))
    sd = jax.ShapeDtypeStruct
    return pl.pallas_call(
        _outproj_kernel,
        grid=(n // tm,),
        in_specs=[row(D_MODEL), row(1024), row(512), row(512), full(km), full(vm), res(w_out_bf), full(g_ffn),
                  full(w_router_t)],
        out_specs=(row(D_MODEL), row(D_MODEL // 2), pl.BlockSpec((N_EXPERTS, tm), lambda i: (0, i))),
        out_shape=(sd((n, D_MODEL), F32), sd((n, D_MODEL // 2), U32), sd((N_EXPERTS, n), F32)),
        scratch_shapes=[pltpu.VMEM((tm, D_MODEL), BF16)],
        name="out_proj",
        compiler_params=_cparams(("parallel",)),
    )(x, attn, hgrn, mq, km, vm, w_out_bf, g_ffn, w_router_t)


def _split3(x):
    hi = x.astype(BF16)
    r1 = x - hi.astype(F32)
    mid = r1.astype(BF16)
    lo = (r1 - mid.astype(F32)).astype(BF16)
    return hi, mid, lo


def _topk_kernel(aff_ref, afft_ref, idx_ref, gate_ref, pos_ref, off_ref, cl_scr, offb_scr, totb_scr, *, cap):
    e_n, nb, _ = aff_ref.shape
    rows = e_n * nb
    aff = aff_ref[...]
    keys = lax.bitcast_convert_type(aff, I32)

    def count(mask):
        s = jnp.sum(mask.astype(F32), axis=1, keepdims=True)
        return jnp.sum(s, axis=2, keepdims=True)

    def bis(i, t):
        cand = t | (jnp.int32(1) << (30 - i))
        return jnp.where(count(keys >= cand) >= cap, cand, t)

    thr = lax.fori_loop(0, 31, bis, jnp.zeros((e_n, 1, 1), I32))
    gt = keys > thr
    eq = keys == thr
    need = cap - count(gt)

    upper = _ind(lax.broadcasted_iota(I32, (LANES, LANES), 0) <= lax.broadcasted_iota(I32, (LANES, LANES), 1))
    ones = jnp.ones((LANES, LANES), BF16)
    bi = lax.broadcasted_iota(I32, (nb, nb), 0)
    bj = lax.broadcasted_iota(I32, (nb, nb), 1)
    strict_lower = _ind(bj < bi)

    def prefix(mask):
        m2 = _ind(mask).reshape(rows, LANES)
        cl = _nn(m2, upper).reshape(e_n, nb, LANES)
        tot = _nn(m2, ones).reshape(e_n, nb, LANES)
        off = jnp.stack([_nn(strict_lower, tot[e].astype(BF16)) for e in range(e_n)], axis=0)
        return cl, off, tot

    cl, off, _ = prefix(eq)
    rank_eq = off + cl - eq.astype(F32)
    sel = gt | (eq & (rank_eq < need))
    cl, off, tot = prefix(sel)
    pos_ref[...] = jnp.where(sel, off + cl - 1.0, -1.0).astype(I32)
    off_ref[...] = off.astype(I32)
    cl_scr[...] = cl
    offb_scr[...] = off
    totb_scr[...] = tot

    lower_incl = _ind(lax.broadcasted_iota(I32, (LANES, LANES), 1) <= lax.broadcasted_iota(I32, (LANES, LANES), 0))
    reps = cap // LANES
    s_row = lax.broadcasted_iota(I32, (nb, cap), 1).astype(F32)
    b_col = lax.broadcasted_iota(I32, (nb, cap), 0).astype(F32)
    j_col = lax.broadcasted_iota(I32, (LANES, cap), 0).astype(F32)

    def per_expert(e, carry):
        sel_e = _ind(pos_ref[e] >= 0)
        clt = _nt(lower_incl, sel_e)
        offt = jnp.concatenate([offb_scr[e]] * reps, axis=1)
        endt = offt + jnp.concatenate([totb_scr[e]] * reps, axis=1)
        hit = (offt <= s_row) & (s_row < endt)
        onehot = _ind(hit)
        g_cnt = _nn(clt.astype(BF16), onehot)
        local = s_row[0:1, :] - jnp.sum(jnp.where(hit, offt, 0.0), axis=0, keepdims=True)
        j_row = jnp.sum((g_cnt <= local).astype(F32), axis=0, keepdims=True)
        b_row = jnp.sum(jnp.where(hit, b_col, 0.0), axis=0, keepdims=True)
        idx_ref[e] = (b_row * LANES + j_row).astype(I32)
        hi, mid, lo = _split3(afft_ref[e])
        g_aff = _nn(hi, onehot) + _nn(mid, onehot) + _nn(lo, onehot)
        gate_ref[e] = jnp.sum(jnp.where(j_col == j_row, g_aff, 0.0), axis=0, keepdims=True)
        return carry

    lax.fori_loop(0, e_n, per_expert, 0)


def _topk(aff_t, cap):
    e_n, n = aff_t.shape
    nb = n // LANES
    aff3 = aff_t.reshape(e_n, nb, LANES)
    afft3 = jnp.swapaxes(aff3, 1, 2)
    sd = jax.ShapeDtypeStruct
    return pl.pallas_call(
        functools.partial(_topk_kernel, cap=cap),
        out_shape=(sd((e_n, 1, cap), I32), sd((e_n, 1, cap), F32), sd((e_n, nb, LANES), I32), sd((e_n, nb, LANES), I32)),
        scratch_shapes=[pltpu.VMEM((e_n, nb, LANES), F32)] * 3,
        name="topk",
        compiler_params=_cparams(None),
    )(aff3, afft3)


GATHER_RING = 128


GATHER_ROWS = 512


def _gather_kernel(idx_ref, src_ref, out_ref, sem):
    rows = out_ref.shape[0]

    def copy(s, k):
        return pltpu.make_async_copy(src_ref.at[pl.ds(idx_ref[0, 0, s], 1)], out_ref.at[pl.ds(s, 1)], sem.at[k])

    def group(gi, carry):
        for k in range(GATHER_RING):
            s = gi * GATHER_RING + k

            @pl.when(gi > 0)
            def _():
                copy(s - GATHER_RING, k).wait()

            copy(s, k).start()
        return carry

    lax.fori_loop(0, rows // GATHER_RING, group, 0)
    for k in range(GATHER_RING):
        copy(rows - GATHER_RING + k, k).wait()


def _gather(h2w, idx, cap):
    total = N_EXPERTS * cap
    rows = min(GATHER_ROWS, cap)
    assert total % rows == 0 and rows % GATHER_RING == 0
    width = h2w.shape[1]
    out = pl.pallas_call(
        _gather_kernel,
        grid=(total // rows,),
        in_specs=[pl.BlockSpec((1, 1, rows), lambda i: (i, 0, 0), memory_space=pltpu.SMEM),
                  pl.BlockSpec(memory_space=pl.ANY)],
        out_specs=pl.BlockSpec((rows, width), lambda i: (i, 0)),
        out_shape=jax.ShapeDtypeStruct((total, width), U32),
        scratch_shapes=[pltpu.SemaphoreType.DMA((GATHER_RING,))],
        name="gather",
        compiler_params=_cparams(("arbitrary",)),
    )(idx.reshape(total // rows, 1, rows), h2w)
    return out.reshape(N_EXPERTS, cap, width)


def _ffn_kernel(x_ref, gate_ref, wg_ref, wu_ref, wd_ref, o_ref, xb_ref, hid_ref, *, nf, tf, tn):
    j = pl.program_id(2)

    @pl.when(j == 0)
    def _():
        w = x_ref[0]
        half = w.shape[1]
        xb_ref[:, :half] = lax.bitcast_convert_type(w << 16, F32).astype(BF16)
        xb_ref[:, half:] = lax.bitcast_convert_type(w & jnp.uint32(0xFFFF0000), F32).astype(BF16)

    @pl.when(j < nf)
    def _():
        x = xb_ref[...]
        hid = jax.nn.silu(_nn(x, wg_ref[0].astype(BF16))) * _nn(x, wu_ref[0].astype(BF16))
        hid_ref[:, pl.ds(pl.multiple_of(j * tf, tf), tf)] = hid.astype(BF16)

    @pl.when(j >= nf)
    def _():
        out = _nn(hid_ref[...], wd_ref[0].astype(BF16)) * gate_ref[0]
        o_ref[0, :, pl.ds(pl.multiple_of((j - nf) * tn, tn), tn)] = out.astype(BF16)


def _ffn(xs, gate_col, wg, wu, wd, ts, tf, tn):
    e_n, cap, dw = xs.shape
    d = 2 * dw
    nf = D_EXPERT // tf
    return pl.pallas_call(
        functools.partial(_ffn_kernel, nf=nf, tf=tf, tn=tn),
        grid=(e_n, cap // ts, nf + d // tn),
        in_specs=[pl.BlockSpec((1, ts, dw), lambda e, s, j: (e, s, 0)),
                  pl.BlockSpec((1, ts, 1), lambda e, s, j: (e, s, 0)),
                  pl.BlockSpec((1, d, tf), lambda e, s, j: (e, 0, jnp.minimum(j, nf - 1))),
                  pl.BlockSpec((1, d, tf), lambda e, s, j: (e, 0, jnp.minimum(j, nf - 1))),
                  pl.BlockSpec((1, D_EXPERT, tn), lambda e, s, j: (e, 0, jnp.maximum(j - nf, 0)))],
        out_specs=pl.BlockSpec((1, ts, d), lambda e, s, j: (e, s, 0)),
        out_shape=jax.ShapeDtypeStruct((e_n, cap, d), BF16),
        scratch_shapes=[pltpu.VMEM((ts, d), BF16), pltpu.VMEM((ts, D_EXPERT), BF16)],
        name="ffn",
        compiler_params=_cparams(("parallel", "parallel", "arbitrary")),
    )(xs, gate_col, wg, wu, wd)


ROW_GROUP = 16
WIN_GROUPS = 3
WIN = WIN_GROUPS * ROW_GROUP
FAR_GROUPS = LANES // ROW_GROUP + 1 - WIN_GROUPS
FAR = FAR_GROUPS * ROW_GROUP


def _combine_kernel(off_ref, tot_ref, y1_ref, post_ref, rows_ref, o_ref, buf, far, sem, far_sem, *, cap):
    b = pl.program_id(0)
    nb = pl.num_programs(0)
    n_groups = cap // ROW_GROUP

    def first_group(blk, e):
        return jnp.minimum(off_ref[e, blk] // ROW_GROUP, n_groups - WIN_GROUPS)

    def copy(blk, slot, e):
        return pltpu.make_async_copy(rows_ref.at[e, pl.ds(first_group(blk, e), WIN_GROUPS)],
                                     buf.at[slot, pl.ds(e * WIN_GROUPS, WIN_GROUPS)], sem.at[slot, e])

    def start(blk, slot):
        for e in range(N_EXPERTS):
            copy(blk, slot, e).start()

    def wait(blk, slot):
        for e in range(N_EXPERTS):
            copy(blk, slot, e).wait()

    slot = b % 2

    @pl.when(b == 0)
    def _():
        far[...] = jnp.zeros_like(far)
        start(0, 0)

    @pl.when(b + 1 < nb)
    def _():
        start(b + 1, 1 - slot)

    post = post_ref[...]
    lane = lax.broadcasted_iota(I32, (LANES, LANES), 1)

    def token_row(e):
        return jnp.broadcast_to(post[:, e:e + 1], (LANES, LANES))

    pieces = []
    for t in range(N_EXPERTS * WIN // LANES):
        miss = jnp.ones((LANES, LANES), I32)
        for e in range(t * LANES // WIN, min(N_EXPERTS, ((t + 1) * LANES - 1) // WIN + 1)):
            j = lane + (t * LANES - e * WIN)
            miss = jnp.where((j >= 0) & (j < WIN), token_row(e) - first_group(b, e) * ROW_GROUP - j, miss)
        pieces.append(_ind(miss == 0))
    w = jnp.concatenate(pieces, axis=1)
    wait(b, slot)
    o_ref[...] = y1_ref[...] + _nn(w, buf[slot].reshape(N_EXPERTS * WIN, o_ref.shape[1]))

    for e in range(N_EXPERTS):
        covered = (first_group(b, e) + WIN_GROUPS) * ROW_GROUP

        @pl.when(off_ref[e, b] + tot_ref[e, b] > covered)
        def _():
            g0 = jnp.minimum(first_group(b, e) + WIN_GROUPS, n_groups - FAR_GROUPS)
            cp = pltpu.make_async_copy(rows_ref.at[e, pl.ds(g0, FAR_GROUPS)], far.at[pl.ds(0, FAR_GROUPS)], far_sem)
            cp.start()
            cp.wait()
            row = token_row(e)
            hit = (row - g0 * ROW_GROUP == lane) & (row >= covered) & (lane < FAR)
            o_ref[...] += _nn(_ind(hit), far[...].reshape(LANES, o_ref.shape[1]))


def _combine(y1, pos_t, off, tot, rows, cap):
    n = y1.shape[0]
    nb = n // LANES
    d = y1.shape[1]
    grid_spec = pltpu.PrefetchScalarGridSpec(
        num_scalar_prefetch=2,
        grid=(nb,),
        in_specs=[pl.BlockSpec((LANES, d), lambda b, off, tot: (b, 0)),
                  pl.BlockSpec((LANES, N_EXPERTS), lambda b, off, tot: (b, 0)),
                  pl.BlockSpec(memory_space=pl.ANY)],
        out_specs=pl.BlockSpec((LANES, d), lambda b, off, tot: (b, 0)),
        scratch_shapes=[pltpu.VMEM((2, N_EXPERTS * WIN_GROUPS, ROW_GROUP, d), BF16),
                        pltpu.VMEM((LANES // ROW_GROUP, ROW_GROUP, d), BF16),
                        pltpu.SemaphoreType.DMA((2, N_EXPERTS)), pltpu.SemaphoreType.DMA(())],
    )
    assert cap % ROW_GROUP == 0 and cap >= WIN + FAR and (N_EXPERTS * WIN) % LANES == 0
    rows = rows.reshape(N_EXPERTS, cap // ROW_GROUP, ROW_GROUP, d)
    return pl.pallas_call(
        functools.partial(_combine_kernel, cap=cap),
        grid_spec=grid_spec,
        out_shape=jax.ShapeDtypeStruct((n, d), F32),
        name="combine",
        compiler_params=_cparams(("arbitrary",)),
    )(off, tot, y1, pos_t, rows)


def _rope_tables(seq_len):
    t = jnp.arange(seq_len)
    row = (t // GRID_W).astype(F32)
    col = (t % GRID_W).astype(F32)
    inv = ROPE_THETA ** (-jnp.arange(ROPE_PAIRS, dtype=F32) / ROPE_PAIRS)
    ang = jnp.stack([row[:, None] * inv, col[:, None] * inv], axis=1)
    ang = jnp.broadcast_to(ang[:, :, None, :], (seq_len, 2, 2, ROPE_PAIRS)).reshape(seq_len, HEAD_DIM)
    first_half = (jnp.arange(HEAD_DIM) % (2 * ROPE_PAIRS)) < ROPE_PAIRS
    return jnp.cos(ang), jnp.where(first_half[None, :], -jnp.sin(ang), jnp.sin(ang))


def _pick(n, pref):
    t = min(n, pref)
    assert n % t == 0, (n, pref)
    return t


def _layer(x, mem, p):
    n = x.shape[0]
    cos, sin_signed = _rope_tables(n)
    km, vm = _mem_kv(mem, p["g_mem"], p["w_mem_kv"], p["g_mk"])
    (q, k, vt, hq, kf, gf, kb, gb, vi, vit, og, mq) = _in_proj(
        x, p["g_mix"], p["w_in"], p["wvt"], p["wvit"], cos, sin_signed, p["g_q"], p["g_k"], p["g_mq"],
        p["lb_fwd"], p["lb_bwd"], _pick(n, 256))
    attn = _attention(q, k, vt, _pick(n, 256), _pick(n, 1024))
    hgrn = _hgrn(hq, kf, gf, kb, gb, vi, vit, og, p["g_hg_out"], _pick(n, 512))
    y1, h2, aff_t = _out_proj(x, attn, hgrn, mq, km, vm, p["w_out"], p["g_ffn"], p["w_router_t"], _pick(n, 512))
    cap = EXPERT_CAPACITY_FACTOR * n // N_EXPERTS
    idx, gate, pos, off = _topk(aff_t, cap)
    xs = _gather(h2, idx, cap)
    gate_col = gate.reshape(N_EXPERTS, cap, 1)
    rows = _ffn(xs, gate_col, p["w_gate"], p["w_up"], p["w_down"], _pick(cap, 1024), 256, 512)
    pos_t = pos.reshape(N_EXPERTS, n).T
    off_s = off[:, :, 0]
    tot_s = jnp.diff(off_s, axis=1, append=jnp.full((N_EXPERTS, 1), cap, I32))
    return _combine(y1, pos_t, off_s, tot_s, rows, cap)


def kernel(x_prompt, x_sample, mem_prompt, mem_sample, g_mix, w_in, g_q, g_k, g_hg_out, lb_fwd, lb_bwd, g_mem, w_mem_kv,
           g_mq, g_mk, w_out, g_ffn, w_router, w_gate, w_up, w_down):
    assert g_mix.shape[0] == 1 and lb_fwd.shape[0] == 2, "single layer: lower bound is the first cumulative-softmax row"
    w_in_bf = w_in[0].astype(BF16)
    p = {
        "g_mix": g_mix, "w_in": w_in_bf,
        "wvt": w_in_bf[:, C_V:C_HQ].T, "wvit": w_in_bf[:, C_HI:C_HG].T,
        "g_q": g_q, "g_k": g_k, "g_hg_out": g_hg_out, "lb_fwd": lb_fwd, "lb_bwd": lb_bwd,
        "g_mem": g_mem, "w_mem_kv": w_mem_kv[0].astype(BF16), "g_mq": g_mq, "g_mk": g_mk,
        "w_out": w_out[0].astype(BF16), "g_ffn": g_ffn, "w_router_t": w_router[0].T,
        "w_gate": w_gate[0], "w_up": w_up[0], "w_down": w_down[0],
    }
    y_prompt = _layer(x_prompt[0], mem_prompt[0], p)
    y_sample = _layer(x_sample[0], mem_sample[0], p)
    return (y_prompt[None], y_sample[None])
```

```python
import functools

import jax
import jax.numpy as jnp
from jax import lax
from jax.experimental import pallas as pl
from jax.experimental.pallas import tpu as pltpu

F32 = jnp.float32
BF16 = jnp.bfloat16
I32 = jnp.int32
U32 = jnp.uint32

D_MODEL = 2048
HEAD_DIM = 128
ATTN_HEADS = 8
ATTN_KV_HEADS = 2
KV_GROUP = ATTN_HEADS // ATTN_KV_HEADS
HG_HEADS = 4
HG_WIDTH = HG_HEADS * HEAD_DIM
MEM_HEADS = 4
GRID_W = 64
ROPE_THETA = 10000.0
ROPE_PAIRS = HEAD_DIM // 4
N_EXPERTS = 16
EXPERT_CAPACITY_FACTOR = 2
D_EXPERT = 2048
EPS = 1e-6
HG_CHUNK = 128
LANES = 128
NEG_BIG = -1e30
LOG2E = 1.4426950408889634
TOKEN_TILE_ROWS = D_MODEL // 2 // LANES

C_Q, C_K, C_V, C_HQ, C_ZF, C_ZB, C_HI, C_HG, C_MQ, C_END = 0, 1024, 1280, 1536, 2048, 2560, 3072, 3584, 4096, 4608

VMEM_LIMIT = 56 * 1024 * 1024


def _cparams(sem, vmem=VMEM_LIMIT):
    return pltpu.CompilerParams(dimension_semantics=sem, vmem_limit_bytes=vmem)


def _nt(a, b):
    return lax.dot_general(a, b, (((1,), (1,)), ((), ())), preferred_element_type=F32)


def _nn(a, b):
    return jnp.dot(a, b, preferred_element_type=F32)


def _ind(mask):
    return jnp.where(mask, 1.0, 0.0).astype(BF16)


def _rms(x, g):
    return x * lax.rsqrt(jnp.mean(x * x, axis=-1, keepdims=True) + EPS) * g


def _memkv_kernel(mem_ref, gmem_ref, w_ref, gmk_ref, km_ref, vm_ref):
    h = _rms(mem_ref[...], gmem_ref[...]).astype(BF16)
    kv = _nn(h, w_ref[...])
    for hh in range(MEM_HEADS):
        sl = slice(hh * HEAD_DIM, (hh + 1) * HEAD_DIM)
        km_ref[:, sl] = _rms(kv[:, sl], gmk_ref[...]).astype(BF16)
    vm_ref[...] = kv[:, MEM_HEADS * HEAD_DIM:].astype(BF16)


def _mem_kv(mem, g_mem, w_mem_kv_bf, g_mk):
    m = mem.shape[0]
    width = MEM_HEADS * HEAD_DIM
    return pl.pallas_call(
        _memkv_kernel,
        out_shape=(jax.ShapeDtypeStruct((m, width), BF16), jax.ShapeDtypeStruct((m, width), BF16)),
        name="mem_kv",
        compiler_params=_cparams(None),
    )(mem, g_mem, w_mem_kv_bf, g_mk)


def _norm_rope(a, g, cos, sin_signed, first_half):
    y = _rms(a, g)
    rot = jnp.where(first_half, pltpu.roll(y, HEAD_DIM - ROPE_PAIRS, 1), pltpu.roll(y, ROPE_PAIRS, 1))
    return y * cos + rot * sin_signed


def _inproj_kernel(x_ref, gmix_ref, w_ref, wvt_ref, wvit_ref, cos_ref, sin_ref, gq_ref, gk_ref, gmq_ref, lbf_ref, lbb_ref,
                   q_ref, k_ref, vt_ref, hq_ref, kf_ref, gf_ref, kb_ref, gb_ref, vi_ref, vit_ref, og_ref, mq_ref, h_scr):
    tm = x_ref.shape[0]
    h_scr[...] = _rms(x_ref[...], gmix_ref[...]).astype(BF16)
    h = h_scr[...]
    cos = cos_ref[...]
    sin = sin_ref[...]
    lane = lax.broadcasted_iota(I32, (tm, HEAD_DIM), 1)
    first_half = (lane % (2 * ROPE_PAIRS)) < ROPE_PAIRS
    scale = HEAD_DIM ** -0.5

    def proj(c0, c1):
        return _nn(h, w_ref[:, c0:c1])

    for c in range(ATTN_HEADS // 4):
        a = proj(C_Q + 512 * c, C_Q + 512 * (c + 1))
        for hh in range(4):
            sl = slice(hh * HEAD_DIM, (hh + 1) * HEAD_DIM)
            y = _norm_rope(a[:, sl], gq_ref[...], cos, sin, first_half) * (scale * LOG2E)
            q_ref[:, 512 * c + hh * HEAD_DIM:512 * c + (hh + 1) * HEAD_DIM] = y.astype(BF16)
    a = proj(C_K, C_V)
    for hh in range(ATTN_KV_HEADS):
        sl = slice(hh * HEAD_DIM, (hh + 1) * HEAD_DIM)
        k_ref[:, sl] = _norm_rope(a[:, sl], gk_ref[...], cos, sin, first_half).astype(BF16)
    vt_ref[...] = _nt(wvt_ref[...], h).astype(BF16)

    hq_ref[...] = jax.nn.silu(proj(C_HQ, C_ZF)).astype(BF16)

    def forget(z, lb_ref, k_out, g_out):
        a2 = lb_ref[...]
        e2 = jnp.exp(a2 - jnp.max(a2, axis=0, keepdims=True))
        lb = e2[0:1, :] / jnp.sum(e2, axis=0, keepdims=True)
        f = lb + (1.0 - lb) * jax.nn.sigmoid(z)
        g_out[...] = jnp.log(f) * LOG2E
        k_out[...] = ((1.0 - lb) * jax.nn.sigmoid(-z)).astype(BF16)

    forget(proj(C_ZF, C_ZB), lbf_ref, kf_ref, gf_ref)
    forget(proj(C_ZB, C_HI), lbb_ref, kb_ref, gb_ref)
    vi_ref[...] = proj(C_HI, C_HG).astype(BF16)
    vit_ref[...] = _nt(wvit_ref[...], h).astype(BF16)
    og_ref[...] = jax.nn.silu(proj(C_HG, C_MQ)).astype(BF16)
    a = proj(C_MQ, C_END)
    for hh in range(MEM_HEADS):
        sl = slice(hh * HEAD_DIM, (hh + 1) * HEAD_DIM)
        mq_ref[:, sl] = (_rms(a[:, sl], gmq_ref[...]) * scale).astype(BF16)


def _in_proj(x, g_mix, w_bf, wvt, wvit, cos, sin_signed, g_q, g_k, g_mq, lb_f, lb_b, tm):
    n = x.shape[0]
    grid = (n // tm,)
    row = lambda w: pl.BlockSpec((tm, w), lambda i: (i, 0))
    col = lambda h: pl.BlockSpec((h, tm), lambda i: (0, i))
    full = lambda a: pl.BlockSpec(a.shape, lambda i: (0,) * a.ndim)
    res = lambda a: pl.BlockSpec(a.shape, lambda i: (0,) * a.ndim, pipeline_mode=pl.Buffered(1))
    sd = jax.ShapeDtypeStruct
    out_shape = (sd((n, 1024), BF16), sd((n, 256), BF16), sd((256, n), BF16),
                 sd((n, 512), BF16), sd((n, 512), BF16), sd((n, 512), F32), sd((n, 512), BF16), sd((n, 512), F32),
                 sd((n, 512), BF16), sd((512, n), BF16), sd((n, 512), BF16), sd((n, 512), BF16))
    out_specs = (row(1024), row(256), col(256), row(512), row(512), row(512), row(512), row(512),
                 row(512), col(512), row(512), row(512))
    return pl.pallas_call(
        _inproj_kernel,
        grid=grid,
        in_specs=[row(D_MODEL), full(g_mix), res(w_bf), res(wvt), res(wvit), row(HEAD_DIM), row(HEAD_DIM),
                  full(g_q), full(g_k), full(g_mq), full(lb_f), full(lb_b)],
        out_specs=out_specs,
        out_shape=out_shape,
        scratch_shapes=[pltpu.VMEM((tm, D_MODEL), BF16)],
        name="in_proj",
        compiler_params=_cparams(("parallel",)),
    )(x, g_mix, w_bf, wvt, wvit, cos, sin_signed, g_q, g_k, g_mq, lb_f, lb_b)


ONES_ROWS = 16


def _attn_kernel(q_ref, k_ref, vt_ref, o_ref, acc_ref, s_ref, *, tk, nk):
    tq = q_ref.shape[0]
    acc_ref[...] = jnp.zeros_like(acc_ref)
    ones = jnp.ones((ONES_ROWS, tk), BF16)

    def scores(hh, off):
        s_ref[hh] = _nt(k_ref[pl.ds(off, tk), :], q_ref[:, hh * HEAD_DIM:(hh + 1) * HEAD_DIM])

    lead = 2
    for hh in range(lead):
        scores(hh, 0)

    def body(i, ms):
        off = pl.multiple_of(i * tk, tk)
        off_next = pl.multiple_of(jnp.minimum(i + 1, nk - 1) * tk, tk)
        vx = jnp.concatenate([vt_ref[:, pl.ds(off, tk)], ones], axis=0)
        out = []
        for hh in range(KV_GROUP):
            nxt = hh + lead
            scores(nxt % KV_GROUP, off if nxt < KV_GROUP else off_next)
            s = s_ref[hh]
            m_new = jnp.maximum(ms[hh], jnp.max(s, axis=0, keepdims=True))
            p = jnp.exp2((s - m_new).astype(BF16))
            alpha = jnp.exp2(ms[hh] - m_new)
            acc_ref[hh] = alpha * acc_ref[hh] + _nn(vx, p)
            out.append(m_new)
        return tuple(out)

    init = tuple(jnp.full((1, tq), NEG_BIG, F32) for _ in range(KV_GROUP))
    lax.fori_loop(0, nk, body, init, unroll=max(1, min(8, nk // 2)))
    for hh in range(KV_GROUP):
        a = acc_ref[hh]
        o_ref[:, hh * HEAD_DIM:(hh + 1) * HEAD_DIM] = (a[0:HEAD_DIM] / a[HEAD_DIM:HEAD_DIM + 1]).T.astype(BF16)


def _attention(q, k, vt, tq, tk):
    s = q.shape[0]
    width = KV_GROUP * HEAD_DIM
    return pl.pallas_call(
        functools.partial(_attn_kernel, tk=tk, nk=s // tk),
        grid=(ATTN_KV_HEADS, s // tq),
        in_specs=[pl.BlockSpec((tq, width), lambda g, i: (i, g)),
                  pl.BlockSpec((s, HEAD_DIM), lambda g, i: (0, g)),
                  pl.BlockSpec((HEAD_DIM, s), lambda g, i: (g, 0))],
        out_specs=pl.BlockSpec((tq, width), lambda g, i: (i, g)),
        out_shape=jax.ShapeDtypeStruct((s, ATTN_HEADS * HEAD_DIM), BF16),
        scratch_shapes=[pltpu.VMEM((KV_GROUP, HEAD_DIM + ONES_ROWS, tq), F32), pltpu.VMEM((KV_GROUP, tk, tq), F32)],
        name="attention",
        compiler_params=_cparams(("parallel", "parallel")),
    )(q, k, vt)


def _hgrn_levels():
    b, out = HG_CHUNK, []
    while b >= 2:
        out.append(b)
        b //= 2
    return out


def _hgrn_pair_masks(rev):
    c = HG_CHUNK
    ti = lax.broadcasted_iota(I32, (c, c), 0)
    si = lax.broadcasted_iota(I32, (c, c), 1)
    pairs = []
    for blk in _hgrn_levels():
        half = blk // 2
        same = (ti // blk) == (si // blk)
        if rev:
            m = same & (ti % blk < half) & (si % blk >= half)
        else:
            m = same & (ti % blk >= half) & (si % blk < half)
        pairs.append(jnp.where(m, 1.0, 0.0))
    return jnp.where(ti == si, 1.0, 0.0), pairs


def _hgrn_chunk(q, k, g, v, vt, state_ref, rev, masks):
    c = HG_CHUNK
    w = HG_WIDTH
    row = lax.broadcasted_iota(I32, (c, w), 0)
    b = g
    sh = 1
    while sh < c:
        if rev:
            b = b + jnp.where(row < c - sh, pltpu.roll(b, c - sh, 0), 0.0)
        else:
            b = b + jnp.where(row >= sh, pltpu.roll(b, sh, 0), 0.0)
        sh *= 2
    qf = q.astype(F32)
    kf = k.astype(F32)
    eye, pair_masks = masks
    att = [eye * _nt(q[:, h * HEAD_DIM:(h + 1) * HEAD_DIM], k[:, h * HEAD_DIM:(h + 1) * HEAD_DIM])
           for h in range(HG_HEADS)]
    for blk, pair in zip(_hgrn_levels(), pair_masks):
        half = blk // 2
        ref_row = half if rev else half - 1
        pos = row % blk
        if blk >= 8:
            r = jnp.concatenate(
                [jnp.broadcast_to(b[s0 + ref_row:s0 + ref_row + 1, :], (blk, w)) for s0 in range(0, c, blk)], axis=0)
        else:
            r = b
            for d in range(-ref_row, blk - ref_row):
                if d != 0:
                    r = jnp.where(pos - ref_row == d, pltpu.roll(b, d % c, 0), r)
        e = jnp.exp2(-jnp.abs(b - r))
        is_q = (pos < half) if rev else (pos >= half)
        x = (jnp.where(is_q, qf, kf) * e).astype(BF16)
        for h in range(HG_HEADS):
            sl = slice(h * HEAD_DIM, (h + 1) * HEAD_DIM)
            att[h] = att[h] + pair * _nt(x[:, sl], x[:, sl])
    b_end = b[0:1, :] if rev else b[c - 1:c, :]
    qd = (qf * jnp.exp2(b)).astype(BF16)
    kd = (kf * jnp.exp2(b_end - b)).astype(BF16)
    dec = jnp.exp2(b_end)
    outs = []
    for h in range(HG_HEADS):
        sl = slice(h * HEAD_DIM, (h + 1) * HEAD_DIM)
        st = state_ref[h]
        o = _nn(att[h].astype(BF16), v[:, sl]) + _nt(qd[:, sl], st.astype(BF16))
        state_ref[h] = dec[:, sl] * st + _nn(vt[sl, :], kd[:, sl])
        outs.append(o)
    return jnp.concatenate(outs, axis=1)


def _hgrn_fwd_kernel(q_ref, k_ref, g_ref, v_ref, vt_ref, o_ref, state_ref, *, nch):
    @pl.when(pl.program_id(0) == 0)
    def _():
        state_ref[...] = jnp.zeros_like(state_ref)

    masks = _hgrn_pair_masks(False)

    def body(i, carry):
        r0 = pl.multiple_of(i * HG_CHUNK, HG_CHUNK)
        rs = pl.ds(r0, HG_CHUNK)
        o_ref[rs, :] = _hgrn_chunk(q_ref[rs, :], k_ref[rs, :], g_ref[rs, :], v_ref[rs, :], vt_ref[:, rs], state_ref, False,
                                   masks)
        return carry

    lax.fori_loop(0, nch, body, 0)


def _hgrn_bwd_kernel(q_ref, k_ref, g_ref, v_ref, vt_ref, of_ref, og_ref, gout_ref, o_ref, state_ref, *, nch):
    @pl.when(pl.program_id(0) == 0)
    def _():
        state_ref[...] = jnp.zeros_like(state_ref)

    masks = _hgrn_pair_masks(True)

    def body(i, carry):
        r0 = pl.multiple_of((nch - 1 - i) * HG_CHUNK, HG_CHUNK)
        rs = pl.ds(r0, HG_CHUNK)
        o = _hgrn_chunk(q_ref[rs, :], k_ref[rs, :], g_ref[rs, :], v_ref[rs, :], vt_ref[:, rs], state_ref, True, masks)
        o = o + of_ref[rs, :]
        og = og_ref[rs, :].astype(F32)
        for h in range(HG_HEADS):
            sl = slice(h * HEAD_DIM, (h + 1) * HEAD_DIM)
            o_ref[rs, sl] = (_rms(o[:, sl], gout_ref[...]) * og[:, sl]).astype(BF16)
        return carry

    lax.fori_loop(0, nch, body, 0)


def _hgrn(hq, kf, gf, kb, gb, vi, vit, og, g_out, tb):
    n = hq.shape[0]
    nblk = n // tb
    nch = tb // HG_CHUNK
    state = pltpu.VMEM((HG_HEADS, HEAD_DIM, HEAD_DIM), F32)
    fr = lambda i: (i, 0)
    fc = lambda i: (0, i)
    o_f = pl.pallas_call(
        functools.partial(_hgrn_fwd_kernel, nch=nch),
        grid=(nblk,),
        in_specs=[pl.BlockSpec((tb, HG_WIDTH), fr)] * 4 + [pl.BlockSpec((HG_WIDTH, tb), fc)],
        out_specs=pl.BlockSpec((tb, HG_WIDTH), fr),
        out_shape=jax.ShapeDtypeStruct((n, HG_WIDTH), F32),
        scratch_shapes=[state],
        name="hgrn_fwd",
        compiler_params=_cparams(("arbitrary",)),
    )(hq, kf, gf, vi, vit)
    br = lambda i: (nblk - 1 - i, 0)
    bc = lambda i: (0, nblk - 1 - i)
    return pl.pallas_call(
        functools.partial(_hgrn_bwd_kernel, nch=nch),
        grid=(nblk,),
        in_specs=[pl.BlockSpec((tb, HG_WIDTH), br)] * 4 + [pl.BlockSpec((HG_WIDTH, tb), bc)]
        + [pl.BlockSpec((tb, HG_WIDTH), br)] * 2 + [pl.BlockSpec(g_out.shape, lambda i: (0, 0))],
        out_specs=pl.BlockSpec((tb, HG_WIDTH), br),
        out_shape=jax.ShapeDtypeStruct((n, HG_WIDTH), BF16),
        scratch_shapes=[state],
        name="hgrn_bwd",
        compiler_params=_cparams(("arbitrary",)),
    )(hq, kb, gb, vi, vit, o_f, og, g_out)


def _outproj_kernel(x_ref, attn_ref, hgrn_ref, mq_ref, km_ref, vm_ref, w_ref, gffn_ref, wrt_ref,
                    y_ref, h2_ref, aff_ref, mix_ref):
    n_attn = ATTN_HEADS * HEAD_DIM
    mix_ref[:, 0:n_attn] = attn_ref[...]
    mix_ref[:, n_attn:n_attn + HG_WIDTH] = hgrn_ref[...]
    for hh in range(MEM_HEADS):
        sl = slice(hh * HEAD_DIM, (hh + 1) * HEAD_DIM)
        s = _nt(mq_ref[:, sl], km_ref[:, sl])
        p = jnp.exp(s - jnp.max(s, axis=-1, keepdims=True))
        p = p / jnp.sum(p, axis=-1, keepdims=True)
        c0 = n_attn + HG_WIDTH + hh * HEAD_DIM
        mix_ref[:, c0:c0 + HEAD_DIM] = _nn(p.astype(BF16), vm_ref[:, sl]).astype(BF16)
    y = x_ref[...] + _nn(mix_ref[...], w_ref[...])
    y_ref[...] = y
    h2 = _rms(y, gffn_ref[...])
    half = D_MODEL // 2
    lo = lax.bitcast_convert_type(h2[:, :half].astype(BF16).astype(F32), U32) >> 16
    hi = lax.bitcast_convert_type(h2[:, half:].astype(BF16).astype(F32), U32) & jnp.uint32(0xFFFF0000)
    word = hi | lo
    tm = word.shape[0]
    for k in range(half // LANES):
        h2_ref[pl.ds(k, tm, stride=half // LANES), :] = word[:, k * LANES:(k + 1) * LANES]
    h_hi = h2.astype(BF16)
    h_lo = (h2 - h_hi.astype(F32)).astype(BF16)
    wr = wrt_ref[...]
    w_hi = wr.astype(BF16)
    w_mid = (wr - w_hi.astype(F32)).astype(BF16)
    two = _nt(jnp.concatenate([w_hi, w_mid], axis=0), h_hi)
    logits = two[:N_EXPERTS] + two[N_EXPERTS:] + _nt(w_hi, h_lo)
    e = jnp.exp(logits - jnp.max(logits, axis=0, keepdims=True))
    aff_ref[...] = e / jnp.sum(e, axis=0, keepdims=True)


def _out_proj(x, attn, hgrn, mq, km, vm, w_out_bf, g_ffn, w_router_t, tm):
    n = x.shape[0]
    row = lambda w: pl.BlockSpec((tm, w), lambda i: (i, 0))
    full = lambda a: pl.BlockSpec(a.shape, lambda i: (0,) * a.ndim)
    res = lambda a: pl.BlockSpec(a.shape, lambda i: (0,) * a.ndim, pipeline_mode=pl.Buffered(1))
    sd = jax.ShapeDtypeStruct
    return pl.pallas_call(
        _outproj_kernel,
        grid=(n // tm,),
        in_specs=[row(D_MODEL), row(1024), row(512), row(512), full(km), full(vm), res(w_out_bf), full(g_ffn),
                  full(w_router_t)],
        out_specs=(row(D_MODEL), pl.BlockSpec((tm * TOKEN_TILE_ROWS, LANES), lambda i: (i, 0)),
                   pl.BlockSpec((N_EXPERTS, tm), lambda i: (0, i))),
        out_shape=(sd((n, D_MODEL), F32), sd((n * TOKEN_TILE_ROWS, LANES), U32), sd((N_EXPERTS, n), F32)),
        scratch_shapes=[pltpu.VMEM((tm, D_MODEL), BF16)],
        name="out_proj",
        compiler_params=_cparams(("parallel",)),
    )(x, attn, hgrn, mq, km, vm, w_out_bf, g_ffn, w_router_t)


def _split3(x):
    hi = x.astype(BF16)
    r1 = x - hi.astype(F32)
    mid = r1.astype(BF16)
    lo = (r1 - mid.astype(F32)).astype(BF16)
    return hi, mid, lo


def _topk_kernel(aff_ref, afft_ref, idx_ref, gate_ref, pos_ref, off_ref, cl_scr, offb_scr, totb_scr, *, cap):
    e_n, nb, _ = aff_ref.shape
    rows = e_n * nb
    aff = aff_ref[...]
    keys = lax.bitcast_convert_type(aff, I32)

    def count(mask):
        s = jnp.sum(mask.astype(F32), axis=1, keepdims=True)
        return jnp.sum(s, axis=2, keepdims=True)

    def bis(i, t):
        cand = t | (jnp.int32(1) << (30 - i))
        return jnp.where(count(keys >= cand) >= cap, cand, t)

    thr = lax.fori_loop(0, 31, bis, jnp.zeros((e_n, 1, 1), I32))
    gt = keys > thr
    eq = keys == thr
    need = cap - count(gt)

    upper = _ind(lax.broadcasted_iota(I32, (LANES, LANES), 0) <= lax.broadcasted_iota(I32, (LANES, LANES), 1))
    ones = jnp.ones((LANES, LANES), BF16)
    bi = lax.broadcasted_iota(I32, (nb, nb), 0)
    bj = lax.broadcasted_iota(I32, (nb, nb), 1)
    strict_lower = _ind(bj < bi)

    def prefix(mask):
        m2 = _ind(mask).reshape(rows, LANES)
        cl = _nn(m2, upper).reshape(e_n, nb, LANES)
        tot = _nn(m2, ones).reshape(e_n, nb, LANES)
        off = jnp.stack([_nn(strict_lower, tot[e].astype(BF16)) for e in range(e_n)], axis=0)
        return cl, off, tot

    cl, off, _ = prefix(eq)
    rank_eq = off + cl - eq.astype(F32)
    sel = gt | (eq & (rank_eq < need))
    cl, off, tot = prefix(sel)
    pos_ref[...] = jnp.where(sel, off + cl - 1.0, -1.0).astype(I32)
    off_ref[...] = off.astype(I32)
    cl_scr[...] = cl
    offb_scr[...] = off
    totb_scr[...] = tot

    lower_incl = _ind(lax.broadcasted_iota(I32, (LANES, LANES), 1) <= lax.broadcasted_iota(I32, (LANES, LANES), 0))
    reps = cap // LANES
    s_row = lax.broadcasted_iota(I32, (nb, cap), 1).astype(F32)
    b_col = lax.broadcasted_iota(I32, (nb, cap), 0).astype(F32)
    j_col = lax.broadcasted_iota(I32, (LANES, cap), 0).astype(F32)

    def per_expert(e, carry):
        sel_e = _ind(pos_ref[e] >= 0)
        clt = _nt(lower_incl, sel_e)
        offt = jnp.concatenate([offb_scr[e]] * reps, axis=1)
        endt = offt + jnp.concatenate([totb_scr[e]] * reps, axis=1)
        hit = (offt <= s_row) & (s_row < endt)
        onehot = _ind(hit)
        g_cnt = _nn(clt.astype(BF16), onehot)
        local = s_row[0:1, :] - jnp.sum(jnp.where(hit, offt, 0.0), axis=0, keepdims=True)
        j_row = jnp.sum((g_cnt <= local).astype(F32), axis=0, keepdims=True)
        b_row = jnp.sum(jnp.where(hit, b_col, 0.0), axis=0, keepdims=True)
        idx_ref[e] = (b_row * LANES + j_row).astype(I32)
        hi, mid, lo = _split3(afft_ref[e])
        g_aff = _nn(hi, onehot) + _nn(mid, onehot) + _nn(lo, onehot)
        gate_ref[e] = jnp.sum(jnp.where(j_col == j_row, g_aff, 0.0), axis=0, keepdims=True)
        return carry

    lax.fori_loop(0, e_n, per_expert, 0)


def _topk(aff_t, cap):
    e_n, n = aff_t.shape
    nb = n // LANES
    aff3 = aff_t.reshape(e_n, nb, LANES)
    afft3 = jnp.swapaxes(aff3, 1, 2)
    sd = jax.ShapeDtypeStruct
    return pl.pallas_call(
        functools.partial(_topk_kernel, cap=cap),
        out_shape=(sd((e_n, 1, cap), I32), sd((e_n, 1, cap), F32), sd((e_n, nb, LANES), I32), sd((e_n, nb, LANES), I32)),
        scratch_shapes=[pltpu.VMEM((e_n, nb, LANES), F32)] * 3,
        name="topk",
        compiler_params=_cparams(None),
    )(aff3, afft3)


GATHER_RING = 128


GATHER_ROWS = 512


def _gather_kernel(idx_ref, src_ref, out_ref, sem):
    rows = out_ref.shape[0]

    def copy(s, k):
        return pltpu.make_async_copy(src_ref.at[idx_ref[0, 0, s]], out_ref.at[s], sem.at[k])

    def group(gi, carry):
        for k in range(GATHER_RING):
            s = gi * GATHER_RING + k

            @pl.when(gi > 0)
            def _():
                copy(s - GATHER_RING, k).wait()

            copy(s, k).start()
        return carry

    lax.fori_loop(0, rows // GATHER_RING, group, 0)
    for k in range(GATHER_RING):
        copy(rows - GATHER_RING + k, k).wait()


def _gather(h2w, idx, cap):
    total = N_EXPERTS * cap
    rows = min(GATHER_ROWS, cap)
    assert total % rows == 0 and rows % GATHER_RING == 0
    n = h2w.shape[0] // TOKEN_TILE_ROWS
    out = pl.pallas_call(
        _gather_kernel,
        grid=(total // rows,),
        in_specs=[pl.BlockSpec((1, 1, rows), lambda i: (i, 0, 0), memory_space=pltpu.SMEM),
                  pl.BlockSpec(memory_space=pl.ANY)],
        out_specs=pl.BlockSpec((rows, TOKEN_TILE_ROWS, LANES), lambda i: (i, 0, 0)),
        out_shape=jax.ShapeDtypeStruct((total, TOKEN_TILE_ROWS, LANES), U32),
        scratch_shapes=[pltpu.SemaphoreType.DMA((GATHER_RING,))],
        name="gather",
        compiler_params=_cparams(("arbitrary",)),
    )(idx.reshape(total // rows, 1, rows), h2w.reshape(n, TOKEN_TILE_ROWS, LANES))
    return out.reshape(N_EXPERTS, cap * TOKEN_TILE_ROWS, LANES)


def _ffn_kernel(x_ref, gate_ref, wg_ref, wu_ref, wd_ref, o_ref, xb_ref, hid_ref, *, nf, tf, tn):
    j = pl.program_id(2)

    @pl.when(j == 0)
    def _():
        ts, d = xb_ref.shape
        for k in range(TOKEN_TILE_ROWS):
            w = x_ref[0, pl.ds(k, ts, stride=TOKEN_TILE_ROWS), :]
            xb_ref[:, k * LANES:(k + 1) * LANES] = lax.bitcast_convert_type(w << 16, F32).astype(BF16)
            xb_ref[:, d // 2 + k * LANES:d // 2 + (k + 1) * LANES] = lax.bitcast_convert_type(
                w & jnp.uint32(0xFFFF0000), F32).astype(BF16)

    @pl.when(j < nf)
    def _():
        x = xb_ref[...]
        hid = jax.nn.silu(_nn(x, wg_ref[0].astype(BF16))) * _nn(x, wu_ref[0].astype(BF16))
        hid_ref[:, pl.ds(pl.multiple_of(j * tf, tf), tf)] = hid.astype(BF16)

    @pl.when(j >= nf)
    def _():
        out = _nn(hid_ref[...], wd_ref[0].astype(BF16)) * gate_ref[0]
        o_ref[0, :, pl.ds(pl.multiple_of((j - nf) * tn, tn), tn)] = out.astype(BF16)


def _ffn(xs, gate_col, wg, wu, wd, ts, tf, tn):
    e_n, cap = gate_col.shape[:2]
    d = D_MODEL
    nf = D_EXPERT // tf
    return pl.pallas_call(
        functools.partial(_ffn_kernel, nf=nf, tf=tf, tn=tn),
        grid=(e_n, cap // ts, nf + d // tn),
        in_specs=[pl.BlockSpec((1, ts * TOKEN_TILE_ROWS, LANES), lambda e, s, j: (e, s, 0)),
                  pl.BlockSpec((1, ts, 1), lambda e, s, j: (e, s, 0)),
                  pl.BlockSpec((1, d, tf), lambda e, s, j: (e, 0, jnp.minimum(j, nf - 1))),
                  pl.BlockSpec((1, d, tf), lambda e, s, j: (e, 0, jnp.minimum(j, nf - 1))),
                  pl.BlockSpec((1, D_EXPERT, tn), lambda e, s, j: (e, 0, jnp.maximum(j - nf, 0)))],
        out_specs=pl.BlockSpec((1, ts, d), lambda e, s, j: (e, s, 0)),
        out_shape=jax.ShapeDtypeStruct((e_n, cap, d), BF16),
        scratch_shapes=[pltpu.VMEM((ts, d), BF16), pltpu.VMEM((ts, D_EXPERT), BF16)],
        name="ffn",
        compiler_params=_cparams(("parallel", "parallel", "arbitrary")),
    )(xs, gate_col, wg, wu, wd)


ROW_GROUP = 16
WIN_GROUPS = 3
WIN = WIN_GROUPS * ROW_GROUP
FAR_GROUPS = LANES // ROW_GROUP + 1 - WIN_GROUPS
FAR = FAR_GROUPS * ROW_GROUP


def _combine_kernel(off_ref, tot_ref, y1_ref, post_ref, rows_ref, o_ref, buf, far, sem, far_sem, *, cap):
    b = pl.program_id(0)
    nb = pl.num_programs(0)
    n_groups = cap // ROW_GROUP

    def first_group(blk, e):
        return jnp.minimum(off_ref[e, blk] // ROW_GROUP, n_groups - WIN_GROUPS)

    def copy(blk, slot, e):
        return pltpu.make_async_copy(rows_ref.at[e, pl.ds(first_group(blk, e), WIN_GROUPS)],
                                     buf.at[slot, pl.ds(e * WIN_GROUPS, WIN_GROUPS)], sem.at[slot, e])

    def start(blk, slot):
        for e in range(N_EXPERTS):
            copy(blk, slot, e).start()

    def wait(blk, slot):
        for e in range(N_EXPERTS):
            copy(blk, slot, e).wait()

    slot = b % 2

    @pl.when(b == 0)
    def _():
        far[...] = jnp.zeros_like(far)
        start(0, 0)

    @pl.when(b + 1 < nb)
    def _():
        start(b + 1, 1 - slot)

    post = post_ref[...]
    lane = lax.broadcasted_iota(I32, (LANES, LANES), 1)

    def token_row(e):
        return jnp.broadcast_to(post[:, e:e + 1], (LANES, LANES))

    pieces = []
    for t in range(N_EXPERTS * WIN // LANES):
        miss = jnp.ones((LANES, LANES), I32)
        for e in range(t * LANES // WIN, min(N_EXPERTS, ((t + 1) * LANES - 1) // WIN + 1)):
            j = lane + (t * LANES - e * WIN)
            miss = jnp.where((j >= 0) & (j < WIN), token_row(e) - first_group(b, e) * ROW_GROUP - j, miss)
        pieces.append(_ind(miss == 0))
    w = jnp.concatenate(pieces, axis=1)
    wait(b, slot)
    o_ref[...] = y1_ref[...] + _nn(w, buf[slot].reshape(N_EXPERTS * WIN, o_ref.shape[1]))

    for e in range(N_EXPERTS):
        covered = (first_group(b, e) + WIN_GROUPS) * ROW_GROUP

        @pl.when(off_ref[e, b] + tot_ref[e, b] > covered)
        def _():
            g0 = jnp.minimum(first_group(b, e) + WIN_GROUPS, n_groups - FAR_GROUPS)
            cp = pltpu.make_async_copy(rows_ref.at[e, pl.ds(g0, FAR_GROUPS)], far.at[pl.ds(0, FAR_GROUPS)], far_sem)
            cp.start()
            cp.wait()
            row = token_row(e)
            hit = (row - g0 * ROW_GROUP == lane) & (row >= covered) & (lane < FAR)
            o_ref[...] += _nn(_ind(hit), far[...].reshape(LANES, o_ref.shape[1]))


def _combine(y1, pos_t, off, tot, rows, cap):
    n = y1.shape[0]
    nb = n // LANES
    d = y1.shape[1]
    grid_spec = pltpu.PrefetchScalarGridSpec(
        num_scalar_prefetch=2,
        grid=(nb,),
        in_specs=[pl.BlockSpec((LANES, d), lambda b, off, tot: (b, 0)),
                  pl.BlockSpec((LANES, N_EXPERTS), lambda b, off, tot: (b, 0)),
                  pl.BlockSpec(memory_space=pl.ANY)],
        out_specs=pl.BlockSpec((LANES, d), lambda b, off, tot: (b, 0)),
        scratch_shapes=[pltpu.VMEM((2, N_EXPERTS * WIN_GROUPS, ROW_GROUP, d), BF16),
                        pltpu.VMEM((LANES // ROW_GROUP, ROW_GROUP, d), BF16),
                        pltpu.SemaphoreType.DMA((2, N_EXPERTS)), pltpu.SemaphoreType.DMA(())],
    )
    assert cap % ROW_GROUP == 0 and cap >= WIN + FAR and (N_EXPERTS * WIN) % LANES == 0
    rows = rows.reshape(N_EXPERTS, cap // ROW_GROUP, ROW_GROUP, d)
    return pl.pallas_call(
        functools.partial(_combine_kernel, cap=cap),
        grid_spec=grid_spec,
        out_shape=jax.ShapeDtypeStruct((n, d), F32),
        name="combine",
        compiler_params=_cparams(("arbitrary",)),
    )(off, tot, y1, pos_t, rows)


def _rope_tables(seq_len):
    t = jnp.arange(seq_len)
    row = (t // GRID_W).astype(F32)
    col = (t % GRID_W).astype(F32)
    inv = ROPE_THETA ** (-jnp.arange(ROPE_PAIRS, dtype=F32) / ROPE_PAIRS)
    ang = jnp.stack([row[:, None] * inv, col[:, None] * inv], axis=1)
    ang = jnp.broadcast_to(ang[:, :, None, :], (seq_len, 2, 2, ROPE_PAIRS)).reshape(seq_len, HEAD_DIM)
    first_half = (jnp.arange(HEAD_DIM) % (2 * ROPE_PAIRS)) < ROPE_PAIRS
    return jnp.cos(ang), jnp.where(first_half[None, :], -jnp.sin(ang), jnp.sin(ang))


def _pick(n, pref):
    t = min(n, pref)
    assert n % t == 0, (n, pref)
    return t


def _layer(x, mem, p):
    n = x.shape[0]
    cos, sin_signed = _rope_tables(n)
    km, vm = _mem_kv(mem, p["g_mem"], p["w_mem_kv"], p["g_mk"])
    (q, k, vt, hq, kf, gf, kb, gb, vi, vit, og, mq) = _in_proj(
        x, p["g_mix"], p["w_in"], p["wvt"], p["wvit"], cos, sin_signed, p["g_q"], p["g_k"], p["g_mq"],
        p["lb_fwd"], p["lb_bwd"], _pick(n, 256))
    attn = _attention(q, k, vt, _pick(n, 256), _pick(n, 1024))
    hgrn = _hgrn(hq, kf, gf, kb, gb, vi, vit, og, p["g_hg_out"], _pick(n, 512))
    y1, h2, aff_t = _out_proj(x, attn, hgrn, mq, km, vm, p["w_out"], p["g_ffn"], p["w_router_t"], _pick(n, 512))
    cap = EXPERT_CAPACITY_FACTOR * n // N_EXPERTS
    idx, gate, pos, off = _topk(aff_t, cap)
    xs = _gather(h2, idx, cap)
    gate_col = gate.reshape(N_EXPERTS, cap, 1)
    rows = _ffn(xs, gate_col, p["w_gate"], p["w_up"], p["w_down"], _pick(cap, 1024), 256, 512)
    pos_t = pos.reshape(N_EXPERTS, n).T
    off_s = off[:, :, 0]
    tot_s = jnp.diff(off_s, axis=1, append=jnp.full((N_EXPERTS, 1), cap, I32))
    return _combine(y1, pos_t, off_s, tot_s, rows, cap)


def kernel(x_prompt, x_sample, mem_prompt, mem_sample, g_mix, w_in, g_q, g_k, g_hg_out, lb_fwd, lb_bwd, g_mem, w_mem_kv,
           g_mq, g_mk, w_out, g_ffn, w_router, w_gate, w_up, w_down):
    assert g_mix.shape[0] == 1 and lb_fwd.shape[0] == 2, "single layer: lower bound is the first cumulative-softmax row"
    w_in_bf = w_in[0].astype(BF16)
    p = {
        "g_mix": g_mix, "w_in": w_in_bf,
        "wvt": w_in_bf[:, C_V:C_HQ].T, "wvit": w_in_bf[:, C_HI:C_HG].T,
        "g_q": g_q, "g_k": g_k, "g_hg_out": g_hg_out, "lb_fwd": lb_fwd, "lb_bwd": lb_bwd,
        "g_mem": g_mem, "w_mem_kv": w_mem_kv[0].astype(BF16), "g_mq": g_mq, "g_mk": g_mk,
        "w_out": w_out[0].astype(BF16), "g_ffn": g_ffn, "w_router_t": w_router[0].T,
        "w_gate": w_gate[0], "w_up": w_up[0], "w_down": w_down[0],
    }
    y_prompt = _layer(x_prompt[0], mem_prompt[0], p)
    y_sample = _layer(x_sample[0], mem_sample[0], p)
    return (y_prompt[None], y_sample[None])
```

```python
import functools

import jax
import jax.numpy as jnp
from jax import lax
from jax.experimental import pallas as pl
from jax.experimental.pallas import tpu as pltpu

F32 = jnp.float32
BF16 = jnp.bfloat16
I32 = jnp.int32
U32 = jnp.uint32

D_MODEL = 2048
HEAD_DIM = 128
ATTN_HEADS = 8
ATTN_KV_HEADS = 2
KV_GROUP = ATTN_HEADS // ATTN_KV_HEADS
HG_HEADS = 4
HG_WIDTH = HG_HEADS * HEAD_DIM
MEM_HEADS = 4
GRID_W = 64
ROPE_THETA = 10000.0
ROPE_PAIRS = HEAD_DIM // 4
N_EXPERTS = 16
EXPERT_CAPACITY_FACTOR = 2
D_EXPERT = 2048
EPS = 1e-6
HG_CHUNK = 128
LANES = 128
NEG_BIG = -1e30
LOG2E = 1.4426950408889634
TOKEN_TILE_ROWS = D_MODEL // 2 // LANES

C_Q, C_K, C_V, C_HQ, C_ZF, C_ZB, C_HI, C_HG, C_MQ, C_END = 0, 1024, 1280, 1536, 2048, 2560, 3072, 3584, 4096, 4608

VMEM_LIMIT = 56 * 1024 * 1024


def _cparams(sem, vmem=VMEM_LIMIT):
    return pltpu.CompilerParams(dimension_semantics=sem, vmem_limit_bytes=vmem)


def _nt(a, b):
    return lax.dot_general(a, b, (((1,), (1,)), ((), ())), preferred_element_type=F32)


def _nn(a, b):
    return jnp.dot(a, b, preferred_element_type=F32)


def _ind(mask):
    return jnp.where(mask, 1.0, 0.0).astype(BF16)


def _rms(x, g):
    return x * lax.rsqrt(jnp.mean(x * x, axis=-1, keepdims=True) + EPS) * g


def _memkv_kernel(mem_ref, gmem_ref, w_ref, gmk_ref, km_ref, vm_ref):
    h = _rms(mem_ref[...], gmem_ref[...]).astype(BF16)
    kv = _nn(h, w_ref[...])
    for hh in range(MEM_HEADS):
        sl = slice(hh * HEAD_DIM, (hh + 1) * HEAD_DIM)
        km_ref[:, sl] = _rms(kv[:, sl], gmk_ref[...]).astype(BF16)
    vm_ref[...] = kv[:, MEM_HEADS * HEAD_DIM:].astype(BF16)


def _mem_kv(mem, g_mem, w_mem_kv_bf, g_mk):
    m = mem.shape[0]
    width = MEM_HEADS * HEAD_DIM
    return pl.pallas_call(
        _memkv_kernel,
        out_shape=(jax.ShapeDtypeStruct((m, width), BF16), jax.ShapeDtypeStruct((m, width), BF16)),
        name="mem_kv",
        compiler_params=_cparams(None),
    )(mem, g_mem, w_mem_kv_bf, g_mk)


def _norm_rope(a, g, cos, sin_signed, first_half):
    y = _rms(a, g)
    rot = jnp.where(first_half, pltpu.roll(y, HEAD_DIM - ROPE_PAIRS, 1), pltpu.roll(y, ROPE_PAIRS, 1))
    return y * cos + rot * sin_signed


def _inproj_kernel(x_ref, gmix_ref, w_ref, wvt_ref, wvit_ref, cos_ref, sin_ref, gq_ref, gk_ref, gmq_ref, lbf_ref, lbb_ref,
                   q_ref, k_ref, vt_ref, hq_ref, kf_ref, gf_ref, kb_ref, gb_ref, vi_ref, vit_ref, og_ref, mq_ref, h_scr):
    tm = x_ref.shape[0]
    h_scr[...] = _rms(x_ref[...], gmix_ref[...]).astype(BF16)
    h = h_scr[...]
    cos = cos_ref[...]
    sin = sin_ref[...]
    lane = lax.broadcasted_iota(I32, (tm, HEAD_DIM), 1)
    first_half = (lane % (2 * ROPE_PAIRS)) < ROPE_PAIRS
    scale = HEAD_DIM ** -0.5

    def proj(c0, c1):
        return _nn(h, w_ref[:, c0:c1])

    for c in range(ATTN_HEADS // 4):
        a = proj(C_Q + 512 * c, C_Q + 512 * (c + 1))
        for hh in range(4):
            sl = slice(hh * HEAD_DIM, (hh + 1) * HEAD_DIM)
            y = _norm_rope(a[:, sl], gq_ref[...], cos, sin, first_half) * (scale * LOG2E)
            q_ref[:, 512 * c + hh * HEAD_DIM:512 * c + (hh + 1) * HEAD_DIM] = y.astype(BF16)
    a = proj(C_K, C_V)
    for hh in range(ATTN_KV_HEADS):
        sl = slice(hh * HEAD_DIM, (hh + 1) * HEAD_DIM)
        k_ref[:, sl] = _norm_rope(a[:, sl], gk_ref[...], cos, sin, first_half).astype(BF16)
    vt_ref[...] = _nt(wvt_ref[...], h).astype(BF16)

    hq_ref[...] = jax.nn.silu(proj(C_HQ, C_ZF)).astype(BF16)

    def forget(z, lb_ref, k_out, g_out):
        a2 = lb_ref[...]
        e2 = jnp.exp(a2 - jnp.max(a2, axis=0, keepdims=True))
        lb = e2[0:1, :] / jnp.sum(e2, axis=0, keepdims=True)
        f = lb + (1.0 - lb) * jax.nn.sigmoid(z)
        g_out[...] = jnp.log(f) * LOG2E
        k_out[...] = ((1.0 - lb) * jax.nn.sigmoid(-z)).astype(BF16)

    forget(proj(C_ZF, C_ZB), lbf_ref, kf_ref, gf_ref)
    forget(proj(C_ZB, C_HI), lbb_ref, kb_ref, gb_ref)
    vi_ref[...] = proj(C_HI, C_HG).astype(BF16)
    vit_ref[...] = _nt(wvit_ref[...], h).astype(BF16)
    og_ref[...] = jax.nn.silu(proj(C_HG, C_MQ)).astype(BF16)
    a = proj(C_MQ, C_END)
    for hh in range(MEM_HEADS):
        sl = slice(hh * HEAD_DIM, (hh + 1) * HEAD_DIM)
        mq_ref[:, sl] = (_rms(a[:, sl], gmq_ref[...]) * scale).astype(BF16)


def _in_proj(x, g_mix, w_bf, wvt, wvit, cos, sin_signed, g_q, g_k, g_mq, lb_f, lb_b, tm):
    n = x.shape[0]
    grid = (n // tm,)
    row = lambda w: pl.BlockSpec((tm, w), lambda i: (i, 0))
    col = lambda h: pl.BlockSpec((h, tm), lambda i: (0, i))
    full = lambda a: pl.BlockSpec(a.shape, lambda i: (0,) * a.ndim)
    res = lambda a: pl.BlockSpec(a.shape, lambda i: (0,) * a.ndim, pipeline_mode=pl.Buffered(1))
    sd = jax.ShapeDtypeStruct
    out_shape = (sd((n, 1024), BF16), sd((n, 256), BF16), sd((256, n), BF16),
                 sd((n, 512), BF16), sd((n, 512), BF16), sd((n, 512), F32), sd((n, 512), BF16), sd((n, 512), F32),
                 sd((n, 512), BF16), sd((512, n), BF16), sd((n, 512), BF16), sd((n, 512), BF16))
    out_specs = (row(1024), row(256), col(256), row(512), row(512), row(512), row(512), row(512),
                 row(512), col(512), row(512), row(512))
    return pl.pallas_call(
        _inproj_kernel,
        grid=grid,
        in_specs=[row(D_MODEL), full(g_mix), res(w_bf), res(wvt), res(wvit), row(HEAD_DIM), row(HEAD_DIM),
                  full(g_q), full(g_k), full(g_mq), full(lb_f), full(lb_b)],
        out_specs=out_specs,
        out_shape=out_shape,
        scratch_shapes=[pltpu.VMEM((tm, D_MODEL), BF16)],
        name="in_proj",
        compiler_params=_cparams(("parallel",)),
    )(x, g_mix, w_bf, wvt, wvit, cos, sin_signed, g_q, g_k, g_mq, lb_f, lb_b)


ONES_ROWS = 16


def _attn_kernel(q_ref, k_ref, vt_ref, o_ref, acc_ref, s_ref, *, tk, nk):
    tq = q_ref.shape[0]
    acc_ref[...] = jnp.zeros_like(acc_ref)
    ones = jnp.ones((ONES_ROWS, tk), BF16)

    def scores(hh, off):
        s_ref[hh] = _nt(k_ref[pl.ds(off, tk), :], q_ref[:, hh * HEAD_DIM:(hh + 1) * HEAD_DIM])

    lead = 2
    for hh in range(lead):
        scores(hh, 0)

    def body(i, ms):
        off = pl.multiple_of(i * tk, tk)
        off_next = pl.multiple_of(jnp.minimum(i + 1, nk - 1) * tk, tk)
        vx = jnp.concatenate([vt_ref[:, pl.ds(off, tk)], ones], axis=0)
        out = []
        for hh in range(KV_GROUP):
            nxt = hh + lead
            scores(nxt % KV_GROUP, off if nxt < KV_GROUP else off_next)
            s = s_ref[hh]
            m_new = jnp.maximum(ms[hh], jnp.max(s, axis=0, keepdims=True))
            p = jnp.exp2((s - m_new).astype(BF16))
            alpha = jnp.exp2(ms[hh] - m_new)
            acc_ref[hh] = alpha * acc_ref[hh] + _nn(vx, p)
            out.append(m_new)
        return tuple(out)

    init = tuple(jnp.full((1, tq), NEG_BIG, F32) for _ in range(KV_GROUP))
    lax.fori_loop(0, nk, body, init, unroll=max(1, min(8, nk // 2)))
    for hh in range(KV_GROUP):
        a = acc_ref[hh]
        o_ref[:, hh * HEAD_DIM:(hh + 1) * HEAD_DIM] = (a[0:HEAD_DIM] / a[HEAD_DIM:HEAD_DIM + 1]).T.astype(BF16)


def _attention(q, k, vt, tq, tk):
    s = q.shape[0]
    width = KV_GROUP * HEAD_DIM
    return pl.pallas_call(
        functools.partial(_attn_kernel, tk=tk, nk=s // tk),
        grid=(ATTN_KV_HEADS, s // tq),
        in_specs=[pl.BlockSpec((tq, width), lambda g, i: (i, g)),
                  pl.BlockSpec((s, HEAD_DIM), lambda g, i: (0, g)),
                  pl.BlockSpec((HEAD_DIM, s), lambda g, i: (g, 0))],
        out_specs=pl.BlockSpec((tq, width), lambda g, i: (i, g)),
        out_shape=jax.ShapeDtypeStruct((s, ATTN_HEADS * HEAD_DIM), BF16),
        scratch_shapes=[pltpu.VMEM((KV_GROUP, HEAD_DIM + ONES_ROWS, tq), F32), pltpu.VMEM((KV_GROUP, tk, tq), F32)],
        name="attention",
        compiler_params=_cparams(("parallel", "parallel")),
    )(q, k, vt)


def _hgrn_levels():
    b, out = HG_CHUNK, []
    while b >= 2:
        out.append(b)
        b //= 2
    return out


def _hgrn_pair_masks(rev):
    c = HG_CHUNK
    ti = lax.broadcasted_iota(I32, (c, c), 0)
    si = lax.broadcasted_iota(I32, (c, c), 1)
    pairs = []
    for blk in _hgrn_levels():
        half = blk // 2
        same = (ti // blk) == (si // blk)
        if rev:
            m = same & (ti % blk < half) & (si % blk >= half)
        else:
            m = same & (ti % blk >= half) & (si % blk < half)
        pairs.append(jnp.where(m, 1.0, 0.0))
    return jnp.where(ti == si, 1.0, 0.0), pairs


def _hgrn_chunk(q, k, g, v, vt, state_ref, rev, masks):
    c = HG_CHUNK
    w = HG_WIDTH
    row = lax.broadcasted_iota(I32, (c, w), 0)
    b = g
    sh = 1
    while sh < c:
        if rev:
            b = b + jnp.where(row < c - sh, pltpu.roll(b, c - sh, 0), 0.0)
        else:
            b = b + jnp.where(row >= sh, pltpu.roll(b, sh, 0), 0.0)
        sh *= 2
    qf = q.astype(F32)
    kf = k.astype(F32)
    eye, pair_masks = masks
    att = [eye * _nt(q[:, h * HEAD_DIM:(h + 1) * HEAD_DIM], k[:, h * HEAD_DIM:(h + 1) * HEAD_DIM])
           for h in range(HG_HEADS)]
    for blk, pair in zip(_hgrn_levels(), pair_masks):
        half = blk // 2
        ref_row = half if rev else half - 1
        pos = row % blk
        if blk >= 8:
            r = jnp.concatenate(
                [jnp.broadcast_to(b[s0 + ref_row:s0 + ref_row + 1, :], (blk, w)) for s0 in range(0, c, blk)], axis=0)
        else:
            r = b
            for d in range(-ref_row, blk - ref_row):
                if d != 0:
                    r = jnp.where(pos - ref_row == d, pltpu.roll(b, d % c, 0), r)
        e = jnp.exp2(-jnp.abs(b - r))
        is_q = (pos < half) if rev else (pos >= half)
        x = (jnp.where(is_q, qf, kf) * e).astype(BF16)
        for h in range(HG_HEADS):
            sl = slice(h * HEAD_DIM, (h + 1) * HEAD_DIM)
            att[h] = att[h] + pair * _nt(x[:, sl], x[:, sl])
    b_end = b[0:1, :] if rev else b[c - 1:c, :]
    qd = (qf * jnp.exp2(b)).astype(BF16)
    kd = (kf * jnp.exp2(b_end - b)).astype(BF16)
    dec = jnp.exp2(b_end)
    outs = []
    for h in range(HG_HEADS):
        sl = slice(h * HEAD_DIM, (h + 1) * HEAD_DIM)
        st = state_ref[h]
        o = _nn(att[h].astype(BF16), v[:, sl]) + _nt(qd[:, sl], st.astype(BF16))
        state_ref[h] = dec[:, sl] * st + _nn(vt[sl, :], kd[:, sl])
        outs.append(o)
    return jnp.concatenate(outs, axis=1)


def _hgrn_fwd_kernel(q_ref, k_ref, g_ref, v_ref, vt_ref, o_ref, state_ref, *, nch):
    @pl.when(pl.program_id(0) == 0)
    def _():
        state_ref[...] = jnp.zeros_like(state_ref)

    masks = _hgrn_pair_masks(False)

    def body(i, carry):
        r0 = pl.multiple_of(i * HG_CHUNK, HG_CHUNK)
        rs = pl.ds(r0, HG_CHUNK)
        o_ref[rs, :] = _hgrn_chunk(q_ref[rs, :], k_ref[rs, :], g_ref[rs, :], v_ref[rs, :], vt_ref[:, rs], state_ref, False,
                                   masks)
        return carry

    lax.fori_loop(0, nch, body, 0)


def _hgrn_bwd_kernel(q_ref, k_ref, g_ref, v_ref, vt_ref, of_ref, og_ref, gout_ref, o_ref, state_ref, *, nch):
    @pl.when(pl.program_id(0) == 0)
    def _():
        state_ref[...] = jnp.zeros_like(state_ref)

    masks = _hgrn_pair_masks(True)

    def body(i, carry):
        r0 = pl.multiple_of((nch - 1 - i) * HG_CHUNK, HG_CHUNK)
        rs = pl.ds(r0, HG_CHUNK)
        o = _hgrn_chunk(q_ref[rs, :], k_ref[rs, :], g_ref[rs, :], v_ref[rs, :], vt_ref[:, rs], state_ref, True, masks)
        o = o + of_ref[rs, :]
        og = og_ref[rs, :].astype(F32)
        for h in range(HG_HEADS):
            sl = slice(h * HEAD_DIM, (h + 1) * HEAD_DIM)
            o_ref[rs, sl] = (_rms(o[:, sl], gout_ref[...]) * og[:, sl]).astype(BF16)
        return carry

    lax.fori_loop(0, nch, body, 0)


def _hgrn(hq, kf, gf, kb, gb, vi, vit, og, g_out, tb):
    n = hq.shape[0]
    nblk = n // tb
    nch = tb // HG_CHUNK
    state = pltpu.VMEM((HG_HEADS, HEAD_DIM, HEAD_DIM), F32)
    fr = lambda i: (i, 0)
    fc = lambda i: (0, i)
    o_f = pl.pallas_call(
        functools.partial(_hgrn_fwd_kernel, nch=nch),
        grid=(nblk,),
        in_specs=[pl.BlockSpec((tb, HG_WIDTH), fr)] * 4 + [pl.BlockSpec((HG_WIDTH, tb), fc)],
        out_specs=pl.BlockSpec((tb, HG_WIDTH), fr),
        out_shape=jax.ShapeDtypeStruct((n, HG_WIDTH), F32),
        scratch_shapes=[state],
        name="hgrn_fwd",
        compiler_params=_cparams(("arbitrary",)),
    )(hq, kf, gf, vi, vit)
    br = lambda i: (nblk - 1 - i, 0)
    bc = lambda i: (0, nblk - 1 - i)
    return pl.pallas_call(
        functools.partial(_hgrn_bwd_kernel, nch=nch),
        grid=(nblk,),
        in_specs=[pl.BlockSpec((tb, HG_WIDTH), br)] * 4 + [pl.BlockSpec((HG_WIDTH, tb), bc)]
        + [pl.BlockSpec((tb, HG_WIDTH), br)] * 2 + [pl.BlockSpec(g_out.shape, lambda i: (0, 0))],
        out_specs=pl.BlockSpec((tb, HG_WIDTH), br),
        out_shape=jax.ShapeDtypeStruct((n, HG_WIDTH), BF16),
        scratch_shapes=[state],
        name="hgrn_bwd",
        compiler_params=_cparams(("arbitrary",)),
    )(hq, kb, gb, vi, vit, o_f, og, g_out)


def _outproj_kernel(x_ref, attn_ref, hgrn_ref, mq_ref, km_ref, vm_ref, w_ref, gffn_ref, wrt_ref,
                    y_ref, h2_ref, aff_ref, mix_ref):
    n_attn = ATTN_HEADS * HEAD_DIM
    mix_ref[:, 0:n_attn] = attn_ref[...]
    mix_ref[:, n_attn:n_attn + HG_WIDTH] = hgrn_ref[...]
    for hh in range(MEM_HEADS):
        sl = slice(hh * HEAD_DIM, (hh + 1) * HEAD_DIM)
        s = _nt(mq_ref[:, sl], km_ref[:, sl])
        p = jnp.exp(s - jnp.max(s, axis=-1, keepdims=True))
        p = p / jnp.sum(p, axis=-1, keepdims=True)
        c0 = n_attn + HG_WIDTH + hh * HEAD_DIM
        mix_ref[:, c0:c0 + HEAD_DIM] = _nn(p.astype(BF16), vm_ref[:, sl]).astype(BF16)
    y = x_ref[...] + _nn(mix_ref[...], w_ref[...])
    y_ref[...] = y
    h2 = _rms(y, gffn_ref[...])
    half = D_MODEL // 2
    lo = lax.bitcast_convert_type(h2[:, :half].astype(BF16).astype(F32), U32) >> 16
    hi = lax.bitcast_convert_type(h2[:, half:].astype(BF16).astype(F32), U32) & jnp.uint32(0xFFFF0000)
    word = hi | lo
    tm = word.shape[0]
    for k in range(half // LANES):
        h2_ref[pl.ds(k, tm, stride=half // LANES), :] = word[:, k * LANES:(k + 1) * LANES]
    h_hi = h2.astype(BF16)
    h_lo = (h2 - h_hi.astype(F32)).astype(BF16)
    wr = wrt_ref[...]
    w_hi = wr.astype(BF16)
    w_mid = (wr - w_hi.astype(F32)).astype(BF16)
    two = _nt(jnp.concatenate([w_hi, w_mid], axis=0), h_hi)
    logits = two[:N_EXPERTS] + two[N_EXPERTS:] + _nt(w_hi, h_lo)
    e = jnp.exp(logits - jnp.max(logits, axis=0, keepdims=True))
    aff_ref[...] = e / jnp.sum(e, axis=0, keepdims=True)


def _out_proj(x, attn, hgrn, mq, km, vm, w_out_bf, g_ffn, w_router_t, tm):
    n = x.shape[0]
    row = lambda w: pl.BlockSpec((tm, w), lambda i: (i, 0))
    full = lambda a: pl.BlockSpec(a.shape, lambda i: (0,) * a.ndim)
    res = lambda a: pl.BlockSpec(a.shape, lambda i: (0,) * a.ndim, pipeline_mode=pl.Buffered(1))
    sd = jax.ShapeDtypeStruct
    return pl.pallas_call(
        _outproj_kernel,
        grid=(n // tm,),
        in_specs=[row(D_MODEL), row(1024), row(512), row(512), full(km), full(vm), res(w_out_bf), full(g_ffn),
                  full(w_router_t)],
        out_specs=(row(D_MODEL), pl.BlockSpec((tm * TOKEN_TILE_ROWS, LANES), lambda i: (i, 0)),
                   pl.BlockSpec((N_EXPERTS, tm), lambda i: (0, i))),
        out_shape=(sd((n, D_MODEL), F32), sd((n * TOKEN_TILE_ROWS, LANES), U32), sd((N_EXPERTS, n), F32)),
        scratch_shapes=[pltpu.VMEM((tm, D_MODEL), BF16)],
        name="out_proj",
        compiler_params=_cparams(("parallel",)),
    )(x, attn, hgrn, mq, km, vm, w_out_bf, g_ffn, w_router_t)


def _split3(x):
    hi = x.astype(BF16)
    r1 = x - hi.astype(F32)
    mid = r1.astype(BF16)
    lo = (r1 - mid.astype(F32)).astype(BF16)
    return hi, mid, lo


def _topk_kernel(aff_ref, afft_ref, idx_ref, gate_ref, pos_ref, off_ref, cl_scr, offb_scr, totb_scr, *, cap):
    e_n, nb, _ = aff_ref.shape
    rows = e_n * nb
    aff = aff_ref[...]
    keys = lax.bitcast_convert_type(aff, I32)

    def count(mask):
        s = jnp.sum(mask.astype(F32), axis=1, keepdims=True)
        return jnp.sum(s, axis=2, keepdims=True)

    def bis(i, t):
        cand = t | (jnp.int32(1) << (30 - i))
        return jnp.where(count(keys >= cand) >= cap, cand, t)

    thr = lax.fori_loop(0, 31, bis, jnp.zeros((e_n, 1, 1), I32))
    gt = keys > thr
    eq = keys == thr
    need = cap - count(gt)

    upper = _ind(lax.broadcasted_iota(I32, (LANES, LANES), 0) <= lax.broadcasted_iota(I32, (LANES, LANES), 1))
    ones = jnp.ones((LANES, LANES), BF16)
    bi = lax.broadcasted_iota(I32, (nb, nb), 0)
    bj = lax.broadcasted_iota(I32, (nb, nb), 1)
    strict_lower = _ind(bj < bi)

    def prefix(mask):
        m2 = _ind(mask).reshape(rows, LANES)
        cl = _nn(m2, upper).reshape(e_n, nb, LANES)
        tot = _nn(m2, ones).reshape(e_n, nb, LANES)
        off = jnp.stack([_nn(strict_lower, tot[e].astype(BF16)) for e in range(e_n)], axis=0)
        return cl, off, tot

    cl, off, _ = prefix(eq)
    rank_eq = off + cl - eq.astype(F32)
    sel = gt | (eq & (rank_eq < need))
    cl, off, tot = prefix(sel)
    pos_ref[...] = jnp.where(sel, off + cl - 1.0, -1.0).astype(I32)
    off_ref[...] = off.astype(I32)
    cl_scr[...] = cl
    offb_scr[...] = off
    totb_scr[...] = tot

    lower_incl = _ind(lax.broadcasted_iota(I32, (LANES, LANES), 1) <= lax.broadcasted_iota(I32, (LANES, LANES), 0))
    reps = cap // LANES
    s_row = lax.broadcasted_iota(I32, (nb, cap), 1).astype(F32)
    b_col = lax.broadcasted_iota(I32, (nb, cap), 0).astype(F32)
    j_col = lax.broadcasted_iota(I32, (LANES, cap), 0).astype(F32)

    def per_expert(e, carry):
        sel_e = _ind(pos_ref[e] >= 0)
        clt = _nt(lower_incl, sel_e)
        offt = jnp.concatenate([offb_scr[e]] * reps, axis=1)
        endt = offt + jnp.concatenate([totb_scr[e]] * reps, axis=1)
        hit = (offt <= s_row) & (s_row < endt)
        onehot = _ind(hit)
        g_cnt = _nn(clt.astype(BF16), onehot)
        local = s_row[0:1, :] - jnp.sum(jnp.where(hit, offt, 0.0), axis=0, keepdims=True)
        j_row = jnp.sum((g_cnt <= local).astype(F32), axis=0, keepdims=True)
        b_row = jnp.sum(jnp.where(hit, b_col, 0.0), axis=0, keepdims=True)
        idx_ref[e] = (b_row * LANES + j_row).astype(I32)
        hi, mid, lo = _split3(afft_ref[e])
        g_aff = _nn(hi, onehot) + _nn(mid, onehot) + _nn(lo, onehot)
        gate_ref[e] = jnp.sum(jnp.where(j_col == j_row, g_aff, 0.0), axis=0, keepdims=True)
        return carry

    lax.fori_loop(0, e_n, per_expert, 0)


def _topk(aff_t, cap):
    e_n, n = aff_t.shape
    nb = n // LANES
    aff3 = aff_t.reshape(e_n, nb, LANES)
    afft3 = jnp.swapaxes(aff3, 1, 2)
    sd = jax.ShapeDtypeStruct
    return pl.pallas_call(
        functools.partial(_topk_kernel, cap=cap),
        out_shape=(sd((e_n, 1, cap), I32), sd((e_n, 1, cap), F32), sd((e_n, nb, LANES), I32), sd((e_n, nb, LANES), I32)),
        scratch_shapes=[pltpu.VMEM((e_n, nb, LANES), F32)] * 3,
        name="topk",
        compiler_params=_cparams(None),
    )(aff3, afft3)


GATHER_RING = 128


GATHER_ROWS = 512


def _gather_kernel(idx_ref, src_ref, out_ref, sem):
    rows = out_ref.shape[0]

    def copy(s, k):
        return pltpu.make_async_copy(src_ref.at[idx_ref[0, 0, s]], out_ref.at[s], sem.at[k])

    def group(gi, carry):
        for k in range(GATHER_RING):
            s = gi * GATHER_RING + k

            @pl.when(gi > 0)
            def _():
                copy(s - GATHER_RING, k).wait()

            copy(s, k).start(priority=k % 2)
        return carry

    lax.fori_loop(0, rows // GATHER_RING, group, 0)
    for k in range(GATHER_RING):
        copy(rows - GATHER_RING + k, k).wait()


def _gather(h2w, idx, cap):
    total = N_EXPERTS * cap
    rows = min(GATHER_ROWS, cap)
    assert total % rows == 0 and rows % GATHER_RING == 0
    n = h2w.shape[0] // TOKEN_TILE_ROWS
    out = pl.pallas_call(
        _gather_kernel,
        grid=(total // rows,),
        in_specs=[pl.BlockSpec((1, 1, rows), lambda i: (i, 0, 0), memory_space=pltpu.SMEM),
                  pl.BlockSpec(memory_space=pl.ANY)],
        out_specs=pl.BlockSpec((rows, TOKEN_TILE_ROWS, LANES), lambda i: (i, 0, 0)),
        out_shape=jax.ShapeDtypeStruct((total, TOKEN_TILE_ROWS, LANES), U32),
        scratch_shapes=[pltpu.SemaphoreType.DMA((GATHER_RING,))],
        name="gather",
        compiler_params=_cparams(("arbitrary",)),
    )(idx.reshape(total // rows, 1, rows), h2w.reshape(n, TOKEN_TILE_ROWS, LANES))
    return out.reshape(N_EXPERTS, cap * TOKEN_TILE_ROWS, LANES)


def _ffn_kernel(x_ref, gate_ref, wg_ref, wu_ref, wd_ref, o_ref, xb_ref, hid_ref, *, nf, tf, tn):
    j = pl.program_id(2)

    @pl.when(j == 0)
    def _():
        ts, d = xb_ref.shape
        for k in range(TOKEN_TILE_ROWS):
            w = x_ref[0, pl.ds(k, ts, stride=TOKEN_TILE_ROWS), :]
            xb_ref[:, k * LANES:(k + 1) * LANES] = lax.bitcast_convert_type(w << 16, F32).astype(BF16)
            xb_ref[:, d // 2 + k * LANES:d // 2 + (k + 1) * LANES] = lax.bitcast_convert_type(
                w & jnp.uint32(0xFFFF0000), F32).astype(BF16)

    @pl.when(j < nf)
    def _():
        x = xb_ref[...]
        hid = jax.nn.silu(_nn(x, wg_ref[0].astype(BF16))) * _nn(x, wu_ref[0].astype(BF16))
        hid_ref[:, pl.ds(pl.multiple_of(j * tf, tf), tf)] = hid.astype(BF16)

    @pl.when(j >= nf)
    def _():
        out = _nn(hid_ref[...], wd_ref[0].astype(BF16)) * gate_ref[0]
        o_ref[0, :, pl.ds(pl.multiple_of((j - nf) * tn, tn), tn)] = out.astype(BF16)


def _ffn(xs, gate_col, wg, wu, wd, ts, tf, tn):
    e_n, cap = gate_col.shape[:2]
    d = D_MODEL
    nf = D_EXPERT // tf
    return pl.pallas_call(
        functools.partial(_ffn_kernel, nf=nf, tf=tf, tn=tn),
        grid=(e_n, cap // ts, nf + d // tn),
        in_specs=[pl.BlockSpec((1, ts * TOKEN_TILE_ROWS, LANES), lambda e, s, j: (e, s, 0)),
                  pl.BlockSpec((1, ts, 1), lambda e, s, j: (e, s, 0)),
                  pl.BlockSpec((1, d, tf), lambda e, s, j: (e, 0, jnp.minimum(j, nf - 1))),
                  pl.BlockSpec((1, d, tf), lambda e, s, j: (e, 0, jnp.minimum(j, nf - 1))),
                  pl.BlockSpec((1, D_EXPERT, tn), lambda e, s, j: (e, 0, jnp.maximum(j - nf, 0)))],
        out_specs=pl.BlockSpec((1, ts, d), lambda e, s, j: (e, s, 0)),
        out_shape=jax.ShapeDtypeStruct((e_n, cap, d), BF16),
        scratch_shapes=[pltpu.VMEM((ts, d), BF16), pltpu.VMEM((ts, D_EXPERT), BF16)],
        name="ffn",
        compiler_params=_cparams(("parallel", "parallel", "arbitrary")),
    )(xs, gate_col, wg, wu, wd)


ROW_GROUP = 16
WIN_GROUPS = 3
WIN = WIN_GROUPS * ROW_GROUP
FAR_GROUPS = LANES // ROW_GROUP + 1 - WIN_GROUPS
FAR = FAR_GROUPS * ROW_GROUP


def _combine_kernel(off_ref, tot_ref, y1_ref, post_ref, rows_ref, o_ref, buf, far, sem, far_sem, *, cap):
    b = pl.program_id(0)
    nb = pl.num_programs(0)
    n_groups = cap // ROW_GROUP

    def first_group(blk, e):
        return jnp.minimum(off_ref[e, blk] // ROW_GROUP, n_groups - WIN_GROUPS)

    def copy(blk, slot, e):
        return pltpu.make_async_copy(rows_ref.at[e, pl.ds(first_group(blk, e), WIN_GROUPS)],
                                     buf.at[slot, pl.ds(e * WIN_GROUPS, WIN_GROUPS)], sem.at[slot, e])

    def start(blk, slot):
        for e in range(N_EXPERTS):
            copy(blk, slot, e).start()

    def wait(blk, slot):
        for e in range(N_EXPERTS):
            copy(blk, slot, e).wait()

    slot = b % 2

    @pl.when(b == 0)
    def _():
        far[...] = jnp.zeros_like(far)
        start(0, 0)

    @pl.when(b + 1 < nb)
    def _():
        start(b + 1, 1 - slot)

    post = post_ref[...]
    lane = lax.broadcasted_iota(I32, (LANES, LANES), 1)

    def token_row(e):
        return jnp.broadcast_to(post[:, e:e + 1], (LANES, LANES))

    pieces = []
    for t in range(N_EXPERTS * WIN // LANES):
        miss = jnp.ones((LANES, LANES), I32)
        for e in range(t * LANES // WIN, min(N_EXPERTS, ((t + 1) * LANES - 1) // WIN + 1)):
            j = lane + (t * LANES - e * WIN)
            miss = jnp.where((j >= 0) & (j < WIN), token_row(e) - first_group(b, e) * ROW_GROUP - j, miss)
        pieces.append(_ind(miss == 0))
    w = jnp.concatenate(pieces, axis=1)
    wait(b, slot)
    o_ref[...] = y1_ref[...] + _nn(w, buf[slot].reshape(N_EXPERTS * WIN, o_ref.shape[1]))

    for e in range(N_EXPERTS):
        covered = (first_group(b, e) + WIN_GROUPS) * ROW_GROUP

        @pl.when(off_ref[e, b] + tot_ref[e, b] > covered)
        def _():
            g0 = jnp.minimum(first_group(b, e) + WIN_GROUPS, n_groups - FAR_GROUPS)
            cp = pltpu.make_async_copy(rows_ref.at[e, pl.ds(g0, FAR_GROUPS)], far.at[pl.ds(0, FAR_GROUPS)], far_sem)
            cp.start()
            cp.wait()
            row = token_row(e)
            hit = (row - g0 * ROW_GROUP == lane) & (row >= covered) & (lane < FAR)
            o_ref[...] += _nn(_ind(hit), far[...].reshape(LANES, o_ref.shape[1]))


def _combine(y1, pos_t, off, tot, rows, cap):
    n = y1.shape[0]
    nb = n // LANES
    d = y1.shape[1]
    grid_spec = pltpu.PrefetchScalarGridSpec(
        num_scalar_prefetch=2,
        grid=(nb,),
        in_specs=[pl.BlockSpec((LANES, d), lambda b, off, tot: (b, 0)),
                  pl.BlockSpec((LANES, N_EXPERTS), lambda b, off, tot: (b, 0)),
                  pl.BlockSpec(memory_space=pl.ANY)],
        out_specs=pl.BlockSpec((LANES, d), lambda b, off, tot: (b, 0)),
        scratch_shapes=[pltpu.VMEM((2, N_EXPERTS * WIN_GROUPS, ROW_GROUP, d), BF16),
                        pltpu.VMEM((LANES // ROW_GROUP, ROW_GROUP, d), BF16),
                        pltpu.SemaphoreType.DMA((2, N_EXPERTS)), pltpu.SemaphoreType.DMA(())],
    )
    assert cap % ROW_GROUP == 0 and cap >= WIN + FAR and (N_EXPERTS * WIN) % LANES == 0
    rows = rows.reshape(N_EXPERTS, cap // ROW_GROUP, ROW_GROUP, d)
    return pl.pallas_call(
        functools.partial(_combine_kernel, cap=cap),
        grid_spec=grid_spec,
        out_shape=jax.ShapeDtypeStruct((n, d), F32),
        name="combine",
        compiler_params=_cparams(("arbitrary",)),
    )(off, tot, y1, pos_t, rows)


def _rope_tables(seq_len):
    t = jnp.arange(seq_len)
    row = (t // GRID_W).astype(F32)
    col = (t % GRID_W).astype(F32)
    inv = ROPE_THETA ** (-jnp.arange(ROPE_PAIRS, dtype=F32) / ROPE_PAIRS)
    ang = jnp.stack([row[:, None] * inv, col[:, None] * inv], axis=1)
    ang = jnp.broadcast_to(ang[:, :, None, :], (seq_len, 2, 2, ROPE_PAIRS)).reshape(seq_len, HEAD_DIM)
    first_half = (jnp.arange(HEAD_DIM) % (2 * ROPE_PAIRS)) < ROPE_PAIRS
    return jnp.cos(ang), jnp.where(first_half[None, :], -jnp.sin(ang), jnp.sin(ang))


def _pick(n, pref):
    t = min(n, pref)
    assert n % t == 0, (n, pref)
    return t


def _layer(x, mem, p):
    n = x.shape[0]
    cos, sin_signed = _rope_tables(n)
    km, vm = _mem_kv(mem, p["g_mem"], p["w_mem_kv"], p["g_mk"])
    (q, k, vt, hq, kf, gf, kb, gb, vi, vit, og, mq) = _in_proj(
        x, p["g_mix"], p["w_in"], p["wvt"], p["wvit"], cos, sin_signed, p["g_q"], p["g_k"], p["g_mq"],
        p["lb_fwd"], p["lb_bwd"], _pick(n, 256))
    attn = _attention(q, k, vt, _pick(n, 256), _pick(n, 1024))
    hgrn = _hgrn(hq, kf, gf, kb, gb, vi, vit, og, p["g_hg_out"], _pick(n, 512))
    y1, h2, aff_t = _out_proj(x, attn, hgrn, mq, km, vm, p["w_out"], p["g_ffn"], p["w_router_t"], _pick(n, 512))
    cap = EXPERT_CAPACITY_FACTOR * n // N_EXPERTS
    idx, gate, pos, off = _topk(aff_t, cap)
    xs = _gather(h2, idx, cap)
    gate_col = gate.reshape(N_EXPERTS, cap, 1)
    rows = _ffn(xs, gate_col, p["w_gate"], p["w_up"], p["w_down"], _pick(cap, 1024), 512, 512)
    pos_t = pos.reshape(N_EXPERTS, n).T
    off_s = off[:, :, 0]
    tot_s = jnp.diff(off_s, axis=1, append=jnp.full((N_EXPERTS, 1), cap, I32))
    return _combine(y1, pos_t, off_s, tot_s, rows, cap)


def kernel(x_prompt, x_sample, mem_prompt, mem_sample, g_mix, w_in, g_q, g_k, g_hg_out, lb_fwd, lb_bwd, g_mem, w_mem_kv,
           g_mq, g_mk, w_out, g_ffn, w_router, w_gate, w_up, w_down):
    assert g_mix.shape[0] == 1 and lb_fwd.shape[0] == 2, "single layer: lower bound is the first cumulative-softmax row"
    w_in_bf = w_in[0].astype(BF16)
    wv, wvi = lax.optimization_barrier((w_in[0, :, C_V:C_HQ], w_in[0, :, C_HI:C_HG]))
    p = {
        "g_mix": g_mix, "w_in": w_in_bf,
        "wvt": wv.T.astype(BF16), "wvit": wvi.T.astype(BF16),
        "g_q": g_q, "g_k": g_k, "g_hg_out": g_hg_out, "lb_fwd": lb_fwd, "lb_bwd": lb_bwd,
        "g_mem": g_mem, "w_mem_kv": w_mem_kv[0].astype(BF16), "g_mq": g_mq, "g_mk": g_mk,
        "w_out": w_out[0].astype(BF16), "g_ffn": g_ffn, "w_router_t": w_router[0].T,
        "w_gate": w_gate[0], "w_up": w_up[0], "w_down": w_down[0],
    }
    y_prompt = _layer(x_prompt[0], mem_prompt[0], p)
    y_sample = _layer(x_sample[0], mem_sample[0], p)
    return (y_prompt[None], y_sample[None])
```

```python
import functools

import jax
import jax.numpy as jnp
from jax import lax
from jax.experimental import pallas as pl
from jax.experimental.pallas import tpu as pltpu

F32 = jnp.float32
BF16 = jnp.bfloat16
I32 = jnp.int32
U32 = jnp.uint32

D_MODEL = 2048
HEAD_DIM = 128
ATTN_HEADS = 8
ATTN_KV_HEADS = 2
KV_GROUP = ATTN_HEADS // ATTN_KV_HEADS
HG_HEADS = 4
HG_WIDTH = HG_HEADS * HEAD_DIM
MEM_HEADS = 4
GRID_W = 64
ROPE_THETA = 10000.0
ROPE_PAIRS = HEAD_DIM // 4
N_EXPERTS = 16
EXPERT_CAPACITY_FACTOR = 2
D_EXPERT = 2048
EPS = 1e-6
HG_CHUNK = 128
LANES = 128
NEG_BIG = -1e30
LOG2E = 1.4426950408889634
TOKEN_TILE_ROWS = D_MODEL // 2 // LANES

C_Q, C_K, C_V, C_HQ, C_ZF, C_ZB, C_HI, C_HG, C_MQ, C_END = 0, 1024, 1280, 1536, 2048, 2560, 3072, 3584, 4096, 4608

VMEM_LIMIT = 56 * 1024 * 1024


def _cparams(sem, vmem=VMEM_LIMIT):
    return pltpu.CompilerParams(dimension_semantics=sem, vmem_limit_bytes=vmem)


def _nt(a, b):
    return lax.dot_general(a, b, (((1,), (1,)), ((), ())), preferred_element_type=F32)


def _nn(a, b):
    return jnp.dot(a, b, preferred_element_type=F32)


def _ind(mask):
    return jnp.where(mask, 1.0, 0.0).astype(BF16)


def _rms(x, g):
    return x * lax.rsqrt(jnp.mean(x * x, axis=-1, keepdims=True) + EPS) * g


def _memkv_kernel(mem_ref, gmem_ref, w_ref, gmk_ref, km_ref, vm_ref):
    h = _rms(mem_ref[...], gmem_ref[...]).astype(BF16)
    kv = _nn(h, w_ref[...])
    for hh in range(MEM_HEADS):
        sl = slice(hh * HEAD_DIM, (hh + 1) * HEAD_DIM)
        km_ref[:, sl] = _rms(kv[:, sl], gmk_ref[...]).astype(BF16)
    vm_ref[...] = kv[:, MEM_HEADS * HEAD_DIM:].astype(BF16)


def _mem_kv(mem, g_mem, w_mem_kv_bf, g_mk):
    m = mem.shape[0]
    width = MEM_HEADS * HEAD_DIM
    return pl.pallas_call(
        _memkv_kernel,
        out_shape=(jax.ShapeDtypeStruct((m, width), BF16), jax.ShapeDtypeStruct((m, width), BF16)),
        name="mem_kv",
        compiler_params=_cparams(None),
    )(mem, g_mem, w_mem_kv_bf, g_mk)


def _norm_rope(a, g, cos, sin_signed, first_half):
    y = _rms(a, g)
    rot = jnp.where(first_half, pltpu.roll(y, HEAD_DIM - ROPE_PAIRS, 1), pltpu.roll(y, ROPE_PAIRS, 1))
    return y * cos + rot * sin_signed


def _inproj_kernel(x_ref, gmix_ref, w_ref, wvt_ref, wvit_ref, cos_ref, sin_ref, gq_ref, gk_ref, gmq_ref, lbf_ref, lbb_ref,
                   q_ref, k_ref, vt_ref, hq_ref, kf_ref, gf_ref, kb_ref, gb_ref, vi_ref, vit_ref, og_ref, mq_ref, h_scr):
    tm = x_ref.shape[0]
    h_scr[...] = _rms(x_ref[...], gmix_ref[...]).astype(BF16)
    h = h_scr[...]
    cos = cos_ref[...]
    sin = sin_ref[...]
    lane = lax.broadcasted_iota(I32, (tm, HEAD_DIM), 1)
    first_half = (lane % (2 * ROPE_PAIRS)) < ROPE_PAIRS
    scale = HEAD_DIM ** -0.5

    def proj(c0, c1):
        return _nn(h, w_ref[:, c0:c1])

    for c in range(ATTN_HEADS // 4):
        a = proj(C_Q + 512 * c, C_Q + 512 * (c + 1))
        for hh in range(4):
            sl = slice(hh * HEAD_DIM, (hh + 1) * HEAD_DIM)
            y = _norm_rope(a[:, sl], gq_ref[...], cos, sin, first_half) * (scale * LOG2E)
            q_ref[:, 512 * c + hh * HEAD_DIM:512 * c + (hh + 1) * HEAD_DIM] = y.astype(BF16)
    a = proj(C_K, C_V)
    for hh in range(ATTN_KV_HEADS):
        sl = slice(hh * HEAD_DIM, (hh + 1) * HEAD_DIM)
        k_ref[:, sl] = _norm_rope(a[:, sl], gk_ref[...], cos, sin, first_half).astype(BF16)
    vt_ref[...] = _nt(wvt_ref[...], h).astype(BF16)

    hq_ref[...] = jax.nn.silu(proj(C_HQ, C_ZF)).astype(BF16)

    def forget(z, lb_ref, k_out, g_out):
        a2 = lb_ref[...]
        e2 = jnp.exp(a2 - jnp.max(a2, axis=0, keepdims=True))
        lb = e2[0:1, :] / jnp.sum(e2, axis=0, keepdims=True)
        f = lb + (1.0 - lb) * jax.nn.sigmoid(z)
        g_out[...] = jnp.log(f) * LOG2E
        k_out[...] = ((1.0 - lb) * jax.nn.sigmoid(-z)).astype(BF16)

    forget(proj(C_ZF, C_ZB), lbf_ref, kf_ref, gf_ref)
    forget(proj(C_ZB, C_HI), lbb_ref, kb_ref, gb_ref)
    vi_ref[...] = proj(C_HI, C_HG).astype(BF16)
    vit_ref[...] = _nt(wvit_ref[...], h).astype(BF16)
    og_ref[...] = jax.nn.silu(proj(C_HG, C_MQ)).astype(BF16)
    a = proj(C_MQ, C_END)
    for hh in range(MEM_HEADS):
        sl = slice(hh * HEAD_DIM, (hh + 1) * HEAD_DIM)
        mq_ref[:, sl] = (_rms(a[:, sl], gmq_ref[...]) * scale).astype(BF16)


def _in_proj(x, g_mix, w_bf, wvt, wvit, cos, sin_signed, g_q, g_k, g_mq, lb_f, lb_b, tm):
    n = x.shape[0]
    grid = (n // tm,)
    row = lambda w: pl.BlockSpec((tm, w), lambda i: (i, 0))
    col = lambda h: pl.BlockSpec((h, tm), lambda i: (0, i))
    full = lambda a: pl.BlockSpec(a.shape, lambda i: (0,) * a.ndim)
    res = lambda a: pl.BlockSpec(a.shape, lambda i: (0,) * a.ndim, pipeline_mode=pl.Buffered(1))
    sd = jax.ShapeDtypeStruct
    out_shape = (sd((n, 1024), BF16), sd((n, 256), BF16), sd((256, n), BF16),
                 sd((n, 512), BF16), sd((n, 512), BF16), sd((n, 512), F32), sd((n, 512), BF16), sd((n, 512), F32),
                 sd((n, 512), BF16), sd((512, n), BF16), sd((n, 512), BF16), sd((n, 512), BF16))
    out_specs = (row(1024), row(256), col(256), row(512), row(512), row(512), row(512), row(512),
                 row(512), col(512), row(512), row(512))
    return pl.pallas_call(
        _inproj_kernel,
        grid=grid,
        in_specs=[row(D_MODEL), full(g_mix), res(w_bf), res(wvt), res(wvit), row(HEAD_DIM), row(HEAD_DIM),
                  full(g_q), full(g_k), full(g_mq), full(lb_f), full(lb_b)],
        out_specs=out_specs,
        out_shape=out_shape,
        scratch_shapes=[pltpu.VMEM((tm, D_MODEL), BF16)],
        name="in_proj",
        compiler_params=_cparams(("parallel",)),
    )(x, g_mix, w_bf, wvt, wvit, cos, sin_signed, g_q, g_k, g_mq, lb_f, lb_b)


ONES_ROWS = 16


def _attn_kernel(q_ref, k_ref, vt_ref, o_ref, acc_ref, s_ref, *, tk, nk):
    tq = q_ref.shape[0]
    acc_ref[...] = jnp.zeros_like(acc_ref)
    ones = jnp.ones((ONES_ROWS, tk), BF16)

    def scores(hh, off):
        s_ref[hh] = _nt(k_ref[pl.ds(off, tk), :], q_ref[:, hh * HEAD_DIM:(hh + 1) * HEAD_DIM])

    lead = 2
    for hh in range(lead):
        scores(hh, 0)

    def body(i, ms):
        off = pl.multiple_of(i * tk, tk)
        off_next = pl.multiple_of(jnp.minimum(i + 1, nk - 1) * tk, tk)
        vx = jnp.concatenate([vt_ref[:, pl.ds(off, tk)], ones], axis=0)
        out = []
        for hh in range(KV_GROUP):
            nxt = hh + lead
            scores(nxt % KV_GROUP, off if nxt < KV_GROUP else off_next)
            s = s_ref[hh]
            m_new = jnp.maximum(ms[hh], jnp.max(s, axis=0, keepdims=True))
            p = jnp.exp2((s - m_new).astype(BF16))
            alpha = jnp.exp2(ms[hh] - m_new)
            acc_ref[hh] = alpha * acc_ref[hh] + _nn(vx, p)
            out.append(m_new)
        return tuple(out)

    init = tuple(jnp.full((1, tq), NEG_BIG, F32) for _ in range(KV_GROUP))
    lax.fori_loop(0, nk, body, init, unroll=max(1, min(8, nk // 2)))
    for hh in range(KV_GROUP):
        a = acc_ref[hh]
        o_ref[:, hh * HEAD_DIM:(hh + 1) * HEAD_DIM] = (a[0:HEAD_DIM] / a[HEAD_DIM:HEAD_DIM + 1]).T.astype(BF16)


def _attention(q, k, vt, tq, tk):
    s = q.shape[0]
    width = KV_GROUP * HEAD_DIM
    return pl.pallas_call(
        functools.partial(_attn_kernel, tk=tk, nk=s // tk),
        grid=(ATTN_KV_HEADS, s // tq),
        in_specs=[pl.BlockSpec((tq, width), lambda g, i: (i, g)),
                  pl.BlockSpec((s, HEAD_DIM), lambda g, i: (0, g)),
                  pl.BlockSpec((HEAD_DIM, s), lambda g, i: (g, 0))],
        out_specs=pl.BlockSpec((tq, width), lambda g, i: (i, g)),
        out_shape=jax.ShapeDtypeStruct((s, ATTN_HEADS * HEAD_DIM), BF16),
        scratch_shapes=[pltpu.VMEM((KV_GROUP, HEAD_DIM + ONES_ROWS, tq), F32), pltpu.VMEM((KV_GROUP, tk, tq), F32)],
        name="attention",
        compiler_params=_cparams(("parallel", "parallel")),
    )(q, k, vt)


def _hgrn_levels():
    b, out = HG_CHUNK, []
    while b >= 2:
        out.append(b)
        b //= 2
    return out


def _hgrn_pair_masks(rev):
    c = HG_CHUNK
    ti = lax.broadcasted_iota(I32, (c, c), 0)
    si = lax.broadcasted_iota(I32, (c, c), 1)
    pairs = []
    for blk in _hgrn_levels():
        half = blk // 2
        same = (ti // blk) == (si // blk)
        if rev:
            m = same & (ti % blk < half) & (si % blk >= half)
        else:
            m = same & (ti % blk >= half) & (si % blk < half)
        pairs.append(jnp.where(m, 1.0, 0.0))
    return jnp.where(ti == si, 1.0, 0.0), pairs


def _hgrn_chunk(q, k, g, v, vt, state_ref, rev, masks):
    c = HG_CHUNK
    w = HG_WIDTH
    row = lax.broadcasted_iota(I32, (c, w), 0)
    b = g
    sh = 1
    while sh < c:
        if rev:
            b = b + jnp.where(row < c - sh, pltpu.roll(b, c - sh, 0), 0.0)
        else:
            b = b + jnp.where(row >= sh, pltpu.roll(b, sh, 0), 0.0)
        sh *= 2
    qf = q.astype(F32)
    kf = k.astype(F32)
    eye, pair_masks = masks
    att = [eye * _nt(q[:, h * HEAD_DIM:(h + 1) * HEAD_DIM], k[:, h * HEAD_DIM:(h + 1) * HEAD_DIM])
           for h in range(HG_HEADS)]
    for blk, pair in zip(_hgrn_levels(), pair_masks):
        half = blk // 2
        ref_row = half if rev else half - 1
        pos = row % blk
        if blk >= 8:
            r = jnp.concatenate(
                [jnp.broadcast_to(b[s0 + ref_row:s0 + ref_row + 1, :], (blk, w)) for s0 in range(0, c, blk)], axis=0)
        else:
            r = b
            for d in range(-ref_row, blk - ref_row):
                if d != 0:
                    r = jnp.where(pos - ref_row == d, pltpu.roll(b, d % c, 0), r)
        e = jnp.exp2(-jnp.abs(b - r))
        is_q = (pos < half) if rev else (pos >= half)
        x = (jnp.where(is_q, qf, kf) * e).astype(BF16)
        for h in range(HG_HEADS):
            sl = slice(h * HEAD_DIM, (h + 1) * HEAD_DIM)
            att[h] = att[h] + pair * _nt(x[:, sl], x[:, sl])
    b_end = b[0:1, :] if rev else b[c - 1:c, :]
    qd = (qf * jnp.exp2(b)).astype(BF16)
    kd = (kf * jnp.exp2(b_end - b)).astype(BF16)
    dec = jnp.exp2(b_end)
    outs = []
    for h in range(HG_HEADS):
        sl = slice(h * HEAD_DIM, (h + 1) * HEAD_DIM)
        st = state_ref[h]
        o = _nn(att[h].astype(BF16), v[:, sl]) + _nt(qd[:, sl], st.astype(BF16))
        state_ref[h] = dec[:, sl] * st + _nn(vt[sl, :], kd[:, sl])
        outs.append(o)
    return jnp.concatenate(outs, axis=1)


def _hgrn_fwd_kernel(q_ref, k_ref, g_ref, v_ref, vt_ref, o_ref, state_ref, *, nch):
    @pl.when(pl.program_id(0) == 0)
    def _():
        state_ref[...] = jnp.zeros_like(state_ref)

    masks = _hgrn_pair_masks(False)

    def body(i, carry):
        r0 = pl.multiple_of(i * HG_CHUNK, HG_CHUNK)
        rs = pl.ds(r0, HG_CHUNK)
        o_ref[rs, :] = _hgrn_chunk(q_ref[rs, :], k_ref[rs, :], g_ref[rs, :], v_ref[rs, :], vt_ref[:, rs], state_ref, False,
                                   masks)
        return carry

    lax.fori_loop(0, nch, body, 0)


def _hgrn_bwd_kernel(q_ref, k_ref, g_ref, v_ref, vt_ref, of_ref, og_ref, gout_ref, o_ref, state_ref, *, nch):
    @pl.when(pl.program_id(0) == 0)
    def _():
        state_ref[...] = jnp.zeros_like(state_ref)

    masks = _hgrn_pair_masks(True)

    def body(i, carry):
        r0 = pl.multiple_of((nch - 1 - i) * HG_CHUNK, HG_CHUNK)
        rs = pl.ds(r0, HG_CHUNK)
        o = _hgrn_chunk(q_ref[rs, :], k_ref[rs, :], g_ref[rs, :], v_ref[rs, :], vt_ref[:, rs], state_ref, True, masks)
        o = o + of_ref[rs, :]
        og = og_ref[rs, :].astype(F32)
        for h in range(HG_HEADS):
            sl = slice(h * HEAD_DIM, (h + 1) * HEAD_DIM)
            o_ref[rs, sl] = (_rms(o[:, sl], gout_ref[...]) * og[:, sl]).astype(BF16)
        return carry

    lax.fori_loop(0, nch, body, 0)


def _hgrn(hq, kf, gf, kb, gb, vi, vit, og, g_out, tb):
    n = hq.shape[0]
    nblk = n // tb
    nch = tb // HG_CHUNK
    state = pltpu.VMEM((HG_HEADS, HEAD_DIM, HEAD_DIM), F32)
    fr = lambda i: (i, 0)
    fc = lambda i: (0, i)
    o_f = pl.pallas_call(
        functools.partial(_hgrn_fwd_kernel, nch=nch),
        grid=(nblk,),
        in_specs=[pl.BlockSpec((tb, HG_WIDTH), fr)] * 4 + [pl.BlockSpec((HG_WIDTH, tb), fc)],
        out_specs=pl.BlockSpec((tb, HG_WIDTH), fr),
        out_shape=jax.ShapeDtypeStruct((n, HG_WIDTH), F32),
        scratch_shapes=[state],
        name="hgrn_fwd",
        compiler_params=_cparams(("arbitrary",)),
    )(hq, kf, gf, vi, vit)
    br = lambda i: (nblk - 1 - i, 0)
    bc = lambda i: (0, nblk - 1 - i)
    return pl.pallas_call(
        functools.partial(_hgrn_bwd_kernel, nch=nch),
        grid=(nblk,),
        in_specs=[pl.BlockSpec((tb, HG_WIDTH), br)] * 4 + [pl.BlockSpec((HG_WIDTH, tb), bc)]
        + [pl.BlockSpec((tb, HG_WIDTH), br)] * 2 + [pl.BlockSpec(g_out.shape, lambda i: (0, 0))],
        out_specs=pl.BlockSpec((tb, HG_WIDTH), br),
        out_shape=jax.ShapeDtypeStruct((n, HG_WIDTH), BF16),
        scratch_shapes=[state],
        name="hgrn_bwd",
        compiler_params=_cparams(("arbitrary",)),
    )(hq, kb, gb, vi, vit, o_f, og, g_out)


def _outproj_kernel(x_ref, attn_ref, hgrn_ref, mq_ref, km_ref, vm_ref, w_ref, gffn_ref, wrt_ref,
                    y_ref, h2_ref, aff_ref, mix_ref):
    n_attn = ATTN_HEADS * HEAD_DIM
    mix_ref[:, 0:n_attn] = attn_ref[...]
    mix_ref[:, n_attn:n_attn + HG_WIDTH] = hgrn_ref[...]
    for hh in range(MEM_HEADS):
        sl = slice(hh * HEAD_DIM, (hh + 1) * HEAD_DIM)
        s = _nt(mq_ref[:, sl], km_ref[:, sl])
        p = jnp.exp(s - jnp.max(s, axis=-1, keepdims=True))
        p = p / jnp.sum(p, axis=-1, keepdims=True)
        c0 = n_attn + HG_WIDTH + hh * HEAD_DIM
        mix_ref[:, c0:c0 + HEAD_DIM] = _nn(p.astype(BF16), vm_ref[:, sl]).astype(BF16)
    y = x_ref[...] + _nn(mix_ref[...], w_ref[...])
    y_ref[...] = y
    h2 = _rms(y, gffn_ref[...])
    half = D_MODEL // 2
    lo = lax.bitcast_convert_type(h2[:, :half].astype(BF16).astype(F32), U32) >> 16
    hi = lax.bitcast_convert_type(h2[:, half:].astype(BF16).astype(F32), U32) & jnp.uint32(0xFFFF0000)
    word = hi | lo
    tm = word.shape[0]
    for k in range(half // LANES):
        h2_ref[pl.ds(k, tm, stride=half // LANES), :] = word[:, k * LANES:(k + 1) * LANES]
    h_hi = h2.astype(BF16)
    h_lo = (h2 - h_hi.astype(F32)).astype(BF16)
    wr = wrt_ref[...]
    w_hi = wr.astype(BF16)
    w_mid = (wr - w_hi.astype(F32)).astype(BF16)
    two = _nt(jnp.concatenate([w_hi, w_mid], axis=0), h_hi)
    logits = two[:N_EXPERTS] + two[N_EXPERTS:] + _nt(w_hi, h_lo)
    e = jnp.exp(logits - jnp.max(logits, axis=0, keepdims=True))
    aff_ref[...] = e / jnp.sum(e, axis=0, keepdims=True)


def _out_proj(x, attn, hgrn, mq, km, vm, w_out_bf, g_ffn, w_router_t, tm):
    n = x.shape[0]
    row = lambda w: pl.BlockSpec((tm, w), lambda i: (i, 0))
    full = lambda a: pl.BlockSpec(a.shape, lambda i: (0,) * a.ndim)
    res = lambda a: pl.BlockSpec(a.shape, lambda i: (0,) * a.ndim, pipeline_mode=pl.Buffered(1))
    sd = jax.ShapeDtypeStruct
    return pl.pallas_call(
        _outproj_kernel,
        grid=(n // tm,),
        in_specs=[row(D_MODEL), row(1024), row(512), row(512), full(km), full(vm), res(w_out_bf), full(g_ffn),
                  full(w_router_t)],
        out_specs=(row(D_MODEL), pl.BlockSpec((tm * TOKEN_TILE_ROWS, LANES), lambda i: (i, 0)),
                   pl.BlockSpec((N_EXPERTS, tm), lambda i: (0, i))),
        out_shape=(sd((n, D_MODEL), F32), sd((n * TOKEN_TILE_ROWS, LANES), U32), sd((N_EXPERTS, n), F32)),
        scratch_shapes=[pltpu.VMEM((tm, D_MODEL), BF16)],
        name="out_proj",
        compiler_params=_cparams(("parallel",)),
    )(x, attn, hgrn, mq, km, vm, w_out_bf, g_ffn, w_router_t)


def _split3(x):
    hi = x.astype(BF16)
    r1 = x - hi.astype(F32)
    mid = r1.astype(BF16)
    lo = (r1 - mid.astype(F32)).astype(BF16)
    return hi, mid, lo


def _topk_kernel(aff_ref, afft_ref, idx_ref, gate_ref, pos_ref, off_ref, cl_scr, offb_scr, totb_scr, *, cap):
    e_n, nb, _ = aff_ref.shape
    rows = e_n * nb
    aff = aff_ref[...]
    keys = lax.bitcast_convert_type(aff, I32)

    def count(mask):
        s = jnp.sum(mask.astype(F32), axis=1, keepdims=True)
        return jnp.sum(s, axis=2, keepdims=True)

    def bis(i, t):
        cand = t | (jnp.int32(1) << (30 - i))
        return jnp.where(count(keys >= cand) >= cap, cand, t)

    thr = lax.fori_loop(0, 31, bis, jnp.zeros((e_n, 1, 1), I32))
    gt = keys > thr
    eq = keys == thr
    need = cap - count(gt)

    upper = _ind(lax.broadcasted_iota(I32, (LANES, LANES), 0) <= lax.broadcasted_iota(I32, (LANES, LANES), 1))
    ones = jnp.ones((LANES, LANES), BF16)
    bi = lax.broadcasted_iota(I32, (nb, nb), 0)
    bj = lax.broadcasted_iota(I32, (nb, nb), 1)
    strict_lower = _ind(bj < bi)

    def prefix(mask):
        m2 = _ind(mask).reshape(rows, LANES)
        cl = _nn(m2, upper).reshape(e_n, nb, LANES)
        tot = _nn(m2, ones).reshape(e_n, nb, LANES)
        off = jnp.stack([_nn(strict_lower, tot[e].astype(BF16)) for e in range(e_n)], axis=0)
        return cl, off, tot

    cl, off, _ = prefix(eq)
    rank_eq = off + cl - eq.astype(F32)
    sel = gt | (eq & (rank_eq < need))
    cl, off, tot = prefix(sel)
    pos_ref[...] = jnp.where(sel, off + cl - 1.0, -1.0).astype(I32)
    off_ref[...] = off.astype(I32)
    cl_scr[...] = cl
    offb_scr[...] = off
    totb_scr[...] = tot

    lower_incl = _ind(lax.broadcasted_iota(I32, (LANES, LANES), 1) <= lax.broadcasted_iota(I32, (LANES, LANES), 0))
    reps = cap // LANES
    s_row = lax.broadcasted_iota(I32, (nb, cap), 1).astype(F32)
    b_col = lax.broadcasted_iota(I32, (nb, cap), 0).astype(F32)
    j_col = lax.broadcasted_iota(I32, (LANES, cap), 0).astype(F32)

    def per_expert(e, carry):
        sel_e = _ind(pos_ref[e] >= 0)
        clt = _nt(lower_incl, sel_e)
        offt = jnp.concatenate([offb_scr[e]] * reps, axis=1)
        endt = offt + jnp.concatenate([totb_scr[e]] * reps, axis=1)
        hit = (offt <= s_row) & (s_row < endt)
        onehot = _ind(hit)
        g_cnt = _nn(clt.astype(BF16), onehot)
        local = s_row[0:1, :] - jnp.sum(jnp.where(hit, offt, 0.0), axis=0, keepdims=True)
        j_row = jnp.sum((g_cnt <= local).astype(F32), axis=0, keepdims=True)
        b_row = jnp.sum(jnp.where(hit, b_col, 0.0), axis=0, keepdims=True)
        idx_ref[e] = (b_row * LANES + j_row).astype(I32)
        hi, mid, lo = _split3(afft_ref[e])
        g_aff = _nn(hi, onehot) + _nn(mid, onehot) + _nn(lo, onehot)
        gate_ref[e] = jnp.sum(jnp.where(j_col == j_row, g_aff, 0.0), axis=0, keepdims=True)
        return carry

    lax.fori_loop(0, e_n, per_expert, 0)


def _topk(aff_t, cap):
    e_n, n = aff_t.shape
    nb = n // LANES
    aff3 = aff_t.reshape(e_n, nb, LANES)
    afft3 = jnp.swapaxes(aff3, 1, 2)
    sd = jax.ShapeDtypeStruct
    return pl.pallas_call(
        functools.partial(_topk_kernel, cap=cap),
        out_shape=(sd((e_n, 1, cap), I32), sd((e_n, 1, cap), F32), sd((e_n, nb, LANES), I32), sd((e_n, nb, LANES), I32)),
        scratch_shapes=[pltpu.VMEM((e_n, nb, LANES), F32)] * 3,
        name="topk",
        compiler_params=_cparams(None),
    )(aff3, afft3)


GATHER_RING = 512


GATHER_ROWS = 1024


def _gather_kernel(idx_ref, src_ref, out_ref, sem):
    rows = out_ref.shape[0]
    ring = sem.shape[0]

    def copy(s, k):
        return pltpu.make_async_copy(src_ref.at[idx_ref[0, 0, s]], out_ref.at[s], sem.at[k])

    def group(gi, carry):
        for k in range(ring):
            s = gi * ring + k

            @pl.when(gi > 0)
            def _():
                copy(s - ring, k).wait()

            copy(s, k).start(priority=k % 2)
        return carry

    lax.fori_loop(0, rows // ring, group, 0)
    for k in range(ring):
        copy(rows - ring + k, k).wait()


def _gather(h2w, idx, cap):
    total = N_EXPERTS * cap
    rows = min(GATHER_ROWS, cap)
    ring = min(GATHER_RING, rows)
    assert total % rows == 0 and rows % ring == 0
    n = h2w.shape[0] // TOKEN_TILE_ROWS
    out = pl.pallas_call(
        _gather_kernel,
        grid=(total // rows,),
        in_specs=[pl.BlockSpec((1, 1, rows), lambda i: (i, 0, 0), memory_space=pltpu.SMEM),
                  pl.BlockSpec(memory_space=pl.ANY)],
        out_specs=pl.BlockSpec((rows, TOKEN_TILE_ROWS, LANES), lambda i: (i, 0, 0)),
        out_shape=jax.ShapeDtypeStruct((total, TOKEN_TILE_ROWS, LANES), U32),
        scratch_shapes=[pltpu.SemaphoreType.DMA((ring,))],
        name="gather",
        compiler_params=_cparams(("arbitrary",)),
    )(idx.reshape(total // rows, 1, rows), h2w.reshape(n, TOKEN_TILE_ROWS, LANES))
    return out.reshape(N_EXPERTS, cap * TOKEN_TILE_ROWS, LANES)


def _ffn_kernel(x_ref, gate_ref, wg_ref, wu_ref, wd_ref, o_ref, xb_ref, hid_ref, *, nf, tf):
    j = pl.program_id(1)
    t = pl.program_id(2)

    @pl.when(j == 0)
    def _():
        _, ts, d = xb_ref.shape
        for k in range(TOKEN_TILE_ROWS):
            w = x_ref[0, pl.ds(k, ts, stride=TOKEN_TILE_ROWS), :]
            xb_ref[t, :, k * LANES:(k + 1) * LANES] = lax.bitcast_convert_type(w << 16, F32).astype(BF16)
            xb_ref[t, :, d // 2 + k * LANES:d // 2 + (k + 1) * LANES] = lax.bitcast_convert_type(
                w & jnp.uint32(0xFFFF0000), F32).astype(BF16)

    @pl.when(j < nf)
    def _():
        x = xb_ref[t]
        hid = jax.nn.silu(_nn(x, wg_ref[0].astype(BF16))) * _nn(x, wu_ref[0].astype(BF16))
        hid_ref[t, :, pl.ds(pl.multiple_of(j * tf, tf), tf)] = hid.astype(BF16)

    @pl.when(j >= nf)
    def _():
        o_ref[0] = (_nn(hid_ref[t], wd_ref[0].astype(BF16)) * gate_ref[0]).astype(BF16)


def _ffn(xs, gate_col, wg, wu, wd, ts, tf, tn):
    e_n, cap = gate_col.shape[:2]
    d = D_MODEL
    nf = D_EXPERT // tf
    nt = cap // ts
    x_map = lambda e, j, t: (e, jnp.where(j == 0, t, nt - 1), 0)
    o_map = lambda e, j, t: (e, jnp.where(j < nf, 0, t), jnp.maximum(j - nf, 0))
    return pl.pallas_call(
        functools.partial(_ffn_kernel, nf=nf, tf=tf),
        grid=(e_n, nf + d // tn, nt),
        in_specs=[pl.BlockSpec((1, ts * TOKEN_TILE_ROWS, LANES), x_map),
                  pl.BlockSpec((1, ts, 1), lambda e, j, t: (e, jnp.where(j < nf, 0, t), 0)),
                  pl.BlockSpec((1, d, tf), lambda e, j, t: (e, 0, jnp.minimum(j, nf - 1))),
                  pl.BlockSpec((1, d, tf), lambda e, j, t: (e, 0, jnp.minimum(j, nf - 1))),
                  pl.BlockSpec((1, D_EXPERT, tn), lambda e, j, t: (e, 0, jnp.maximum(j - nf, 0)))],
        out_specs=pl.BlockSpec((1, ts, tn), o_map),
        out_shape=jax.ShapeDtypeStruct((e_n, cap, d), BF16),
        scratch_shapes=[pltpu.VMEM((nt, ts, d), BF16), pltpu.VMEM((nt, ts, D_EXPERT), BF16)],
        name="ffn",
        compiler_params=_cparams(("parallel", "arbitrary", "arbitrary")),
    )(xs, gate_col, wg, wu, wd)


ROW_GROUP = 16
WIN_GROUPS = 3
WIN = WIN_GROUPS * ROW_GROUP
FAR_GROUPS = LANES // ROW_GROUP + 1 - WIN_GROUPS
FAR = FAR_GROUPS * ROW_GROUP


def _combine_kernel(off_ref, tot_ref, y1_ref, post_ref, rows_ref, o_ref, buf, far, sem, far_sem, *, cap):
    b = pl.program_id(0)
    nb = pl.num_programs(0)
    n_groups = cap // ROW_GROUP

    def first_group(blk, e):
        return jnp.minimum(off_ref[e, blk] // ROW_GROUP, n_groups - WIN_GROUPS)

    def copy(blk, slot, e):
        return pltpu.make_async_copy(rows_ref.at[e, pl.ds(first_group(blk, e), WIN_GROUPS)],
                                     buf.at[slot, pl.ds(e * WIN_GROUPS, WIN_GROUPS)], sem.at[slot, e])

    def start(blk, slot):
        for e in range(N_EXPERTS):
            copy(blk, slot, e).start()

    def wait(blk, slot):
        for e in range(N_EXPERTS):
            copy(blk, slot, e).wait()

    slot = b % 2

    @pl.when(b == 0)
    def _():
        far[...] = jnp.zeros_like(far)
        start(0, 0)

    @pl.when(b + 1 < nb)
    def _():
        start(b + 1, 1 - slot)

    post = post_ref[...]
    lane = lax.broadcasted_iota(I32, (LANES, LANES), 1)

    def token_row(e):
        return jnp.broadcast_to(post[:, e:e + 1], (LANES, LANES))

    pieces = []
    for t in range(N_EXPERTS * WIN // LANES):
        miss = jnp.ones((LANES, LANES), I32)
        for e in range(t * LANES // WIN, min(N_EXPERTS, ((t + 1) * LANES - 1) // WIN + 1)):
            j = lane + (t * LANES - e * WIN)
            miss = jnp.where((j >= 0) & (j < WIN), token_row(e) - first_group(b, e) * ROW_GROUP - j, miss)
        pieces.append(_ind(miss == 0))
    w = jnp.concatenate(pieces, axis=1)
    wait(b, slot)
    o_ref[...] = y1_ref[...] + _nn(w, buf[slot].reshape(N_EXPERTS * WIN, o_ref.shape[1]))

    for e in range(N_EXPERTS):
        covered = (first_group(b, e) + WIN_GROUPS) * ROW_GROUP

        @pl.when(off_ref[e, b] + tot_ref[e, b] > covered)
        def _():
            g0 = jnp.minimum(first_group(b, e) + WIN_GROUPS, n_groups - FAR_GROUPS)
            cp = pltpu.make_async_copy(rows_ref.at[e, pl.ds(g0, FAR_GROUPS)], far.at[pl.ds(0, FAR_GROUPS)], far_sem)
            cp.start()
            cp.wait()
            row = token_row(e)
            hit = (row - g0 * ROW_GROUP == lane) & (row >= covered) & (lane < FAR)
            o_ref[...] += _nn(_ind(hit), far[...].reshape(LANES, o_ref.shape[1]))


def _combine(y1, pos_t, off, tot, rows, cap):
    n = y1.shape[0]
    nb = n // LANES
    d = y1.shape[1]
    grid_spec = pltpu.PrefetchScalarGridSpec(
        num_scalar_prefetch=2,
        grid=(nb,),
        in_specs=[pl.BlockSpec((LANES, d), lambda b, off, tot: (b, 0)),
                  pl.BlockSpec((LANES, N_EXPERTS), lambda b, off, tot: (b, 0)),
                  pl.BlockSpec(memory_space=pl.ANY)],
        out_specs=pl.BlockSpec((LANES, d), lambda b, off, tot: (b, 0)),
        scratch_shapes=[pltpu.VMEM((2, N_EXPERTS * WIN_GROUPS, ROW_GROUP, d), BF16),
                        pltpu.VMEM((LANES // ROW_GROUP, ROW_GROUP, d), BF16),
                        pltpu.SemaphoreType.DMA((2, N_EXPERTS)), pltpu.SemaphoreType.DMA(())],
    )
    assert cap % ROW_GROUP == 0 and cap >= WIN + FAR and (N_EXPERTS * WIN) % LANES == 0
    rows = rows.reshape(N_EXPERTS, cap // ROW_GROUP, ROW_GROUP, d)
    return pl.pallas_call(
        functools.partial(_combine_kernel, cap=cap),
        grid_spec=grid_spec,
        out_shape=jax.ShapeDtypeStruct((n, d), F32),
        name="combine",
        compiler_params=_cparams(("arbitrary",)),
    )(off, tot, y1, pos_t, rows)


def _rope_tables(seq_len):
    t = jnp.arange(seq_len)
    row = (t // GRID_W).astype(F32)
    col = (t % GRID_W).astype(F32)
    inv = ROPE_THETA ** (-jnp.arange(ROPE_PAIRS, dtype=F32) / ROPE_PAIRS)
    ang = jnp.stack([row[:, None] * inv, col[:, None] * inv], axis=1)
    ang = jnp.broadcast_to(ang[:, :, None, :], (seq_len, 2, 2, ROPE_PAIRS)).reshape(seq_len, HEAD_DIM)
    first_half = (jnp.arange(HEAD_DIM) % (2 * ROPE_PAIRS)) < ROPE_PAIRS
    return jnp.cos(ang), jnp.where(first_half[None, :], -jnp.sin(ang), jnp.sin(ang))


def _pick(n, pref):
    t = min(n, pref)
    assert n % t == 0, (n, pref)
    return t


def _layer(x, mem, p):
    n = x.shape[0]
    cos, sin_signed = _rope_tables(n)
    km, vm = _mem_kv(mem, p["g_mem"], p["w_mem_kv"], p["g_mk"])
    (q, k, vt, hq, kf, gf, kb, gb, vi, vit, og, mq) = _in_proj(
        x, p["g_mix"], p["w_in"], p["wvt"], p["wvit"], cos, sin_signed, p["g_q"], p["g_k"], p["g_mq"],
        p["lb_fwd"], p["lb_bwd"], _pick(n, 256))
    attn = _attention(q, k, vt, _pick(n, 256), _pick(n, 1024))
    hgrn = _hgrn(hq, kf, gf, kb, gb, vi, vit, og, p["g_hg_out"], _pick(n, 512))
    y1, h2, aff_t = _out_proj(x, attn, hgrn, mq, km, vm, p["w_out"], p["g_ffn"], p["w_router_t"], _pick(n, 512))
    cap = EXPERT_CAPACITY_FACTOR * n // N_EXPERTS
    idx, gate, pos, off = _topk(aff_t, cap)
    xs = _gather(h2, idx, cap)
    gate_col = gate.reshape(N_EXPERTS, cap, 1)
    rows = _ffn(xs, gate_col, p["w_gate"], p["w_up"], p["w_down"], _pick(cap, 1024), 256, 512)
    pos_t = pos.reshape(N_EXPERTS, n).T
    off_s = off[:, :, 0]
    tot_s = jnp.diff(off_s, axis=1, append=jnp.full((N_EXPERTS, 1), cap, I32))
    return _combine(y1, pos_t, off_s, tot_s, rows, cap)


def kernel(x_prompt, x_sample, mem_prompt, mem_sample, g_mix, w_in, g_q, g_k, g_hg_out, lb_fwd, lb_bwd, g_mem, w_mem_kv,
           g_mq, g_mk, w_out, g_ffn, w_router, w_gate, w_up, w_down):
    assert g_mix.shape[0] == 1 and lb_fwd.shape[0] == 2, "single layer: lower bound is the first cumulative-softmax row"
    w_in_bf = w_in[0].astype(BF16)
    wv, wvi = lax.optimization_barrier((w_in[0, :, C_V:C_HQ], w_in[0, :, C_HI:C_HG]))
    p = {
        "g_mix": g_mix, "w_in": w_in_bf,
        "wvt": wv.T.astype(BF16), "wvit": wvi.T.astype(BF16),
        "g_q": g_q, "g_k": g_k, "g_hg_out": g_hg_out, "lb_fwd": lb_fwd, "lb_bwd": lb_bwd,
        "g_mem": g_mem, "w_mem_kv": w_mem_kv[0].astype(BF16), "g_mq": g_mq, "g_mk": g_mk,
        "w_out": w_out[0].astype(BF16), "g_ffn": g_ffn, "w_router_t": w_router[0].T,
        "w_gate": w_gate[0], "w_up": w_up[0], "w_down": w_down[0],
    }
    y_prompt = _layer(x_prompt[0], mem_prompt[0], p)
    y_sample = _layer(x_sample[0], mem_sample[0], p)
    return (y_prompt[None], y_sample[None])
```

```python
import functools

import jax
import jax.numpy as jnp
from jax import lax
from jax.experimental import pallas as pl
from jax.experimental.pallas import tpu as pltpu

F32 = jnp.float32
BF16 = jnp.bfloat16
I32 = jnp.int32
U32 = jnp.uint32

D_MODEL = 2048
HEAD_DIM = 128
ATTN_HEADS = 8
ATTN_KV_HEADS = 2
KV_GROUP = ATTN_HEADS // ATTN_KV_HEADS
HG_HEADS = 4
HG_WIDTH = HG_HEADS * HEAD_DIM
MEM_HEADS = 4
GRID_W = 64
ROPE_THETA = 10000.0
ROPE_PAIRS = HEAD_DIM // 4
N_EXPERTS = 16
EXPERT_CAPACITY_FACTOR = 2
D_EXPERT = 2048
EPS = 1e-6
HG_CHUNK = 128
LANES = 128
NEG_BIG = -1e30
LOG2E = 1.4426950408889634
TOKEN_TILE_ROWS = D_MODEL // 2 // LANES

C_Q, C_K, C_V, C_HQ, C_ZF, C_ZB, C_HI, C_HG, C_MQ, C_END = 0, 1024, 1280, 1536, 2048, 2560, 3072, 3584, 4096, 4608

VMEM_LIMIT = 56 * 1024 * 1024


def _cparams(sem, vmem=VMEM_LIMIT):
    return pltpu.CompilerParams(dimension_semantics=sem, vmem_limit_bytes=vmem)


def _nt(a, b):
    return lax.dot_general(a, b, (((1,), (1,)), ((), ())), preferred_element_type=F32)


def _nn(a, b):
    return jnp.dot(a, b, preferred_element_type=F32)


def _ind(mask):
    return jnp.where(mask, 1.0, 0.0).astype(BF16)


def _rms(x, g):
    return x * lax.rsqrt(jnp.mean(x * x, axis=-1, keepdims=True) + EPS) * g


def _memkv_kernel(mem_ref, gmem_ref, w_ref, gmk_ref, km_ref, vm_ref):
    h = _rms(mem_ref[...], gmem_ref[...]).astype(BF16)
    kv = _nn(h, w_ref[...])
    for hh in range(MEM_HEADS):
        sl = slice(hh * HEAD_DIM, (hh + 1) * HEAD_DIM)
        km_ref[:, sl] = _rms(kv[:, sl], gmk_ref[...]).astype(BF16)
    vm_ref[...] = kv[:, MEM_HEADS * HEAD_DIM:].astype(BF16)


def _mem_kv(mem, g_mem, w_mem_kv_bf, g_mk):
    m = mem.shape[0]
    width = MEM_HEADS * HEAD_DIM
    return pl.pallas_call(
        _memkv_kernel,
        out_shape=(jax.ShapeDtypeStruct((m, width), BF16), jax.ShapeDtypeStruct((m, width), BF16)),
        name="mem_kv",
        compiler_params=_cparams(None),
    )(mem, g_mem, w_mem_kv_bf, g_mk)


def _norm_rope(a, g, cos, sin_signed, first_half):
    y = _rms(a, g)
    rot = jnp.where(first_half, pltpu.roll(y, HEAD_DIM - ROPE_PAIRS, 1), pltpu.roll(y, ROPE_PAIRS, 1))
    return y * cos + rot * sin_signed


def _inproj_kernel(x_ref, gmix_ref, w_ref, wvt_ref, wvit_ref, cos_ref, sin_ref, gq_ref, gk_ref, gmq_ref, lbf_ref, lbb_ref,
                   q_ref, k_ref, vt_ref, hq_ref, kf_ref, gf_ref, kb_ref, gb_ref, vi_ref, vit_ref, og_ref, mq_ref, h_scr):
    tm = x_ref.shape[0]
    h_scr[...] = _rms(x_ref[...], gmix_ref[...]).astype(BF16)
    h = h_scr[...]
    cos = cos_ref[...]
    sin = sin_ref[...]
    lane = lax.broadcasted_iota(I32, (tm, HEAD_DIM), 1)
    first_half = (lane % (2 * ROPE_PAIRS)) < ROPE_PAIRS
    scale = HEAD_DIM ** -0.5

    def proj(c0, c1):
        return _nn(h, w_ref[:, c0:c1])

    for c in range(ATTN_HEADS // 4):
        a = proj(C_Q + 512 * c, C_Q + 512 * (c + 1))
        for hh in range(4):
            sl = slice(hh * HEAD_DIM, (hh + 1) * HEAD_DIM)
            y = _norm_rope(a[:, sl], gq_ref[...], cos, sin, first_half) * (scale * LOG2E)
            q_ref[:, 512 * c + hh * HEAD_DIM:512 * c + (hh + 1) * HEAD_DIM] = y.astype(BF16)
    a = proj(C_K, C_V)
    for hh in range(ATTN_KV_HEADS):
        sl = slice(hh * HEAD_DIM, (hh + 1) * HEAD_DIM)
        k_ref[:, sl] = _norm_rope(a[:, sl], gk_ref[...], cos, sin, first_half).astype(BF16)
    vt_ref[...] = _nt(wvt_ref[...], h).astype(BF16)

    hq_ref[...] = jax.nn.silu(proj(C_HQ, C_ZF)).astype(BF16)

    def forget(z, lb_ref, k_out, g_out):
        a2 = lb_ref[...]
        e2 = jnp.exp(a2 - jnp.max(a2, axis=0, keepdims=True))
        lb = e2[0:1, :] / jnp.sum(e2, axis=0, keepdims=True)
        f = lb + (1.0 - lb) * jax.nn.sigmoid(z)
        g_out[...] = jnp.log(f) * LOG2E
        k_out[...] = ((1.0 - lb) * jax.nn.sigmoid(-z)).astype(BF16)

    forget(proj(C_ZF, C_ZB), lbf_ref, kf_ref, gf_ref)
    forget(proj(C_ZB, C_HI), lbb_ref, kb_ref, gb_ref)
    vi_ref[...] = proj(C_HI, C_HG).astype(BF16)
    vit_ref[...] = _nt(wvit_ref[...], h).astype(BF16)
    og_ref[...] = jax.nn.silu(proj(C_HG, C_MQ)).astype(BF16)
    a = proj(C_MQ, C_END)
    for hh in range(MEM_HEADS):
        sl = slice(hh * HEAD_DIM, (hh + 1) * HEAD_DIM)
        mq_ref[:, sl] = (_rms(a[:, sl], gmq_ref[...]) * scale).astype(BF16)


def _in_proj(x, g_mix, w_bf, wvt, wvit, cos, sin_signed, g_q, g_k, g_mq, lb_f, lb_b, tm):
    n = x.shape[0]
    grid = (n // tm,)
    row = lambda w: pl.BlockSpec((tm, w), lambda i: (i, 0))
    col = lambda h: pl.BlockSpec((h, tm), lambda i: (0, i))
    full = lambda a: pl.BlockSpec(a.shape, lambda i: (0,) * a.ndim)
    res = lambda a: pl.BlockSpec(a.shape, lambda i: (0,) * a.ndim, pipeline_mode=pl.Buffered(1))
    sd = jax.ShapeDtypeStruct
    out_shape = (sd((n, 1024), BF16), sd((n, 256), BF16), sd((256, n), BF16),
                 sd((n, 512), BF16), sd((n, 512), BF16), sd((n, 512), F32), sd((n, 512), BF16), sd((n, 512), F32),
                 sd((n, 512), BF16), sd((512, n), BF16), sd((n, 512), BF16), sd((n, 512), BF16))
    out_specs = (row(1024), row(256), col(256), row(512), row(512), row(512), row(512), row(512),
                 row(512), col(512), row(512), row(512))
    return pl.pallas_call(
        _inproj_kernel,
        grid=grid,
        in_specs=[row(D_MODEL), full(g_mix), res(w_bf), res(wvt), res(wvit), row(HEAD_DIM), row(HEAD_DIM),
                  full(g_q), full(g_k), full(g_mq), full(lb_f), full(lb_b)],
        out_specs=out_specs,
        out_shape=out_shape,
        scratch_shapes=[pltpu.VMEM((tm, D_MODEL), BF16)],
        name="in_proj",
        compiler_params=_cparams(("parallel",)),
    )(x, g_mix, w_bf, wvt, wvit, cos, sin_signed, g_q, g_k, g_mq, lb_f, lb_b)


ONES_ROWS = 16


def _attn_kernel(q_ref, k_ref, vt_ref, o_ref, acc_ref, s_ref, *, tk, nk):
    tq = q_ref.shape[0]
    acc_ref[...] = jnp.zeros_like(acc_ref)
    ones = jnp.ones((ONES_ROWS, tk), BF16)

    def scores(hh, off):
        s_ref[hh] = _nt(k_ref[pl.ds(off, tk), :], q_ref[:, hh * HEAD_DIM:(hh + 1) * HEAD_DIM])

    lead = 2
    for hh in range(lead):
        scores(hh, 0)

    def body(i, ms):
        off = pl.multiple_of(i * tk, tk)
        off_next = pl.multiple_of(jnp.minimum(i + 1, nk - 1) * tk, tk)
        vx = jnp.concatenate([vt_ref[:, pl.ds(off, tk)], ones], axis=0)
        out = []
        for hh in range(KV_GROUP):
            nxt = hh + lead
            scores(nxt % KV_GROUP, off if nxt < KV_GROUP else off_next)
            s = s_ref[hh]
            m_new = jnp.maximum(ms[hh], jnp.max(s, axis=0, keepdims=True))
            p = jnp.exp2((s - m_new).astype(BF16))
            alpha = jnp.exp2(ms[hh] - m_new)
            acc_ref[hh] = alpha * acc_ref[hh] + _nn(vx, p)
            out.append(m_new)
        return tuple(out)

    init = tuple(jnp.full((1, tq), NEG_BIG, F32) for _ in range(KV_GROUP))
    lax.fori_loop(0, nk, body, init, unroll=max(1, min(8, nk // 2)))
    for hh in range(KV_GROUP):
        a = acc_ref[hh]
        o_ref[:, hh * HEAD_DIM:(hh + 1) * HEAD_DIM] = (a[0:HEAD_DIM] / a[HEAD_DIM:HEAD_DIM + 1]).T.astype(BF16)


def _attention(q, k, vt, tq, tk):
    s = q.shape[0]
    width = KV_GROUP * HEAD_DIM
    return pl.pallas_call(
        functools.partial(_attn_kernel, tk=tk, nk=s // tk),
        grid=(ATTN_KV_HEADS, s // tq),
        in_specs=[pl.BlockSpec((tq, width), lambda g, i: (i, g)),
                  pl.BlockSpec((s, HEAD_DIM), lambda g, i: (0, g)),
                  pl.BlockSpec((HEAD_DIM, s), lambda g, i: (g, 0))],
        out_specs=pl.BlockSpec((tq, width), lambda g, i: (i, g)),
        out_shape=jax.ShapeDtypeStruct((s, ATTN_HEADS * HEAD_DIM), BF16),
        scratch_shapes=[pltpu.VMEM((KV_GROUP, HEAD_DIM + ONES_ROWS, tq), F32), pltpu.VMEM((KV_GROUP, tk, tq), F32)],
        name="attention",
        compiler_params=_cparams(("parallel", "parallel")),
    )(q, k, vt)


def _hgrn_levels():
    b, out = HG_CHUNK, []
    while b >= 2:
        out.append(b)
        b //= 2
    return out


def _hgrn_pair_masks(rev):
    c = HG_CHUNK
    ti = lax.broadcasted_iota(I32, (c, c), 0)
    si = lax.broadcasted_iota(I32, (c, c), 1)
    row = lax.broadcasted_iota(I32, (c, HG_WIDTH), 0)
    pairs, signs = [], []
    for blk in _hgrn_levels():
        half = blk // 2
        same = (ti // blk) == (si // blk)
        if rev:
            m = same & (ti % blk < half) & (si % blk >= half)
            is_q = row % blk < half
        else:
            m = same & (ti % blk >= half) & (si % blk < half)
            is_q = row % blk >= half
        pairs.append(jnp.where(m, 1.0, 0.0))
        signs.append(jnp.where(is_q, 1.0, -1.0))
    return jnp.where(ti == si, 1.0, 0.0), pairs, signs


def _hgrn_chunk(q, k, g, v, vt, state_ref, rev, masks):
    c = HG_CHUNK
    w = HG_WIDTH
    row = lax.broadcasted_iota(I32, (c, w), 0)
    b = g
    sh = 1
    while sh < c:
        if rev:
            b = b + jnp.where(row < c - sh, pltpu.roll(b, c - sh, 0), 0.0)
        else:
            b = b + jnp.where(row >= sh, pltpu.roll(b, sh, 0), 0.0)
        sh *= 2
    qf = q.astype(F32)
    kf = k.astype(F32)
    eye, pair_masks, signs = masks
    att = [eye * _nt(q[:, h * HEAD_DIM:(h + 1) * HEAD_DIM], k[:, h * HEAD_DIM:(h + 1) * HEAD_DIM])
           for h in range(HG_HEADS)]
    for blk, pair, sign in zip(_hgrn_levels(), pair_masks, signs):
        half = blk // 2
        ref_row = half if rev else half - 1
        pos = row % blk
        if blk >= 8:
            r = jnp.concatenate(
                [jnp.broadcast_to(b[s0 + ref_row:s0 + ref_row + 1, :], (blk, w)) for s0 in range(0, c, blk)], axis=0)
        else:
            r = b
            for d in range(-ref_row, blk - ref_row):
                if d != 0:
                    r = jnp.where(pos - ref_row == d, pltpu.roll(b, d % c, 0), r)
        e = jnp.exp2((b - r) * sign)
        is_q = (pos < half) if rev else (pos >= half)
        x = (jnp.where(is_q, qf, kf) * e).astype(BF16)
        for h in range(HG_HEADS):
            sl = slice(h * HEAD_DIM, (h + 1) * HEAD_DIM)
            att[h] = att[h] + pair * _nt(x[:, sl], x[:, sl])
    b_end = b[0:1, :] if rev else b[c - 1:c, :]
    qd = (qf * jnp.exp2(b)).astype(BF16)
    kd = (kf * jnp.exp2(b_end - b)).astype(BF16)
    dec = jnp.exp2(b_end)
    outs = []
    for h in range(HG_HEADS):
        sl = slice(h * HEAD_DIM, (h + 1) * HEAD_DIM)
        st = state_ref[h]
        o = _nn(att[h].astype(BF16), v[:, sl]) + _nt(qd[:, sl], st.astype(BF16))
        state_ref[h] = dec[:, sl] * st + _nn(vt[sl, :], kd[:, sl])
        outs.append(o)
    return jnp.concatenate(outs, axis=1)


def _hgrn_fwd_kernel(q_ref, k_ref, g_ref, v_ref, vt_ref, o_ref, state_ref, *, nch):
    @pl.when(pl.program_id(0) == 0)
    def _():
        state_ref[...] = jnp.zeros_like(state_ref)

    masks = _hgrn_pair_masks(False)

    def body(i, carry):
        r0 = pl.multiple_of(i * HG_CHUNK, HG_CHUNK)
        rs = pl.ds(r0, HG_CHUNK)
        o_ref[rs, :] = _hgrn_chunk(q_ref[rs, :], k_ref[rs, :], g_ref[rs, :], v_ref[rs, :], vt_ref[:, rs], state_ref, False,
                                   masks)
        return carry

    lax.fori_loop(0, nch, body, 0)


def _hgrn_bwd_kernel(q_ref, k_ref, g_ref, v_ref, vt_ref, of_ref, og_ref, gout_ref, o_ref, state_ref, *, nch):
    @pl.when(pl.program_id(0) == 0)
    def _():
        state_ref[...] = jnp.zeros_like(state_ref)

    masks = _hgrn_pair_masks(True)

    def body(i, carry):
        r0 = pl.multiple_of((nch - 1 - i) * HG_CHUNK, HG_CHUNK)
        rs = pl.ds(r0, HG_CHUNK)
        o = _hgrn_chunk(q_ref[rs, :], k_ref[rs, :], g_ref[rs, :], v_ref[rs, :], vt_ref[:, rs], state_ref, True, masks)
        o = o + of_ref[rs, :]
        og = og_ref[rs, :].astype(F32)
        for h in range(HG_HEADS):
            sl = slice(h * HEAD_DIM, (h + 1) * HEAD_DIM)
            o_ref[rs, sl] = (_rms(o[:, sl], gout_ref[...]) * og[:, sl]).astype(BF16)
        return carry

    lax.fori_loop(0, nch, body, 0)


def _hgrn(hq, kf, gf, kb, gb, vi, vit, og, g_out, tb):
    n = hq.shape[0]
    nblk = n // tb
    nch = tb // HG_CHUNK
    state = pltpu.VMEM((HG_HEADS, HEAD_DIM, HEAD_DIM), F32)
    fr = lambda i: (i, 0)
    fc = lambda i: (0, i)
    o_f = pl.pallas_call(
        functools.partial(_hgrn_fwd_kernel, nch=nch),
        grid=(nblk,),
        in_specs=[pl.BlockSpec((tb, HG_WIDTH), fr)] * 4 + [pl.BlockSpec((HG_WIDTH, tb), fc)],
        out_specs=pl.BlockSpec((tb, HG_WIDTH), fr),
        out_shape=jax.ShapeDtypeStruct((n, HG_WIDTH), F32),
        scratch_shapes=[state],
        name="hgrn_fwd",
        compiler_params=_cparams(("arbitrary",)),
    )(hq, kf, gf, vi, vit)
    br = lambda i: (nblk - 1 - i, 0)
    bc = lambda i: (0, nblk - 1 - i)
    return pl.pallas_call(
        functools.partial(_hgrn_bwd_kernel, nch=nch),
        grid=(nblk,),
        in_specs=[pl.BlockSpec((tb, HG_WIDTH), br)] * 4 + [pl.BlockSpec((HG_WIDTH, tb), bc)]
        + [pl.BlockSpec((tb, HG_WIDTH), br)] * 2 + [pl.BlockSpec(g_out.shape, lambda i: (0, 0))],
        out_specs=pl.BlockSpec((tb, HG_WIDTH), br),
        out_shape=jax.ShapeDtypeStruct((n, HG_WIDTH), BF16),
        scratch_shapes=[state],
        name="hgrn_bwd",
        compiler_params=_cparams(("arbitrary",)),
    )(hq, kb, gb, vi, vit, o_f, og, g_out)


def _outproj_kernel(x_ref, attn_ref, hgrn_ref, mq_ref, km_ref, vm_ref, w_ref, gffn_ref, wrt_ref,
                    y_ref, h2_ref, aff_ref, mix_ref):
    n_attn = ATTN_HEADS * HEAD_DIM
    mix_ref[:, 0:n_attn] = attn_ref[...]
    mix_ref[:, n_attn:n_attn + HG_WIDTH] = hgrn_ref[...]
    for hh in range(MEM_HEADS):
        sl = slice(hh * HEAD_DIM, (hh + 1) * HEAD_DIM)
        s = _nt(mq_ref[:, sl], km_ref[:, sl])
        p = jnp.exp(s - jnp.max(s, axis=-1, keepdims=True))
        p = p / jnp.sum(p, axis=-1, keepdims=True)
        c0 = n_attn + HG_WIDTH + hh * HEAD_DIM
        mix_ref[:, c0:c0 + HEAD_DIM] = _nn(p.astype(BF16), vm_ref[:, sl]).astype(BF16)
    y = x_ref[...] + _nn(mix_ref[...], w_ref[...])
    y_ref[...] = y
    h2 = _rms(y, gffn_ref[...])
    half = D_MODEL // 2
    lo = lax.bitcast_convert_type(h2[:, :half].astype(BF16).astype(F32), U32) >> 16
    hi = lax.bitcast_convert_type(h2[:, half:].astype(BF16).astype(F32), U32) & jnp.uint32(0xFFFF0000)
    word = hi | lo
    tm = word.shape[0]
    for k in range(half // LANES):
        h2_ref[pl.ds(k, tm, stride=half // LANES), :] = word[:, k * LANES:(k + 1) * LANES]
    h_hi = h2.astype(BF16)
    h_lo = (h2 - h_hi.astype(F32)).astype(BF16)
    wr = wrt_ref[...]
    w_hi = wr.astype(BF16)
    w_mid = (wr - w_hi.astype(F32)).astype(BF16)
    two = _nt(jnp.concatenate([w_hi, w_mid], axis=0), h_hi)
    logits = two[:N_EXPERTS] + two[N_EXPERTS:] + _nt(w_hi, h_lo)
    e = jnp.exp(logits - jnp.max(logits, axis=0, keepdims=True))
    aff_ref[...] = e / jnp.sum(e, axis=0, keepdims=True)


def _out_proj(x, attn, hgrn, mq, km, vm, w_out_bf, g_ffn, w_router_t, tm):
    n = x.shape[0]
    row = lambda w: pl.BlockSpec((tm, w), lambda i: (i, 0))
    full = lambda a: pl.BlockSpec(a.shape, lambda i: (0,) * a.ndim)
    res = lambda a: pl.BlockSpec(a.shape, lambda i: (0,) * a.ndim, pipeline_mode=pl.Buffered(1))
    sd = jax.ShapeDtypeStruct
    return pl.pallas_call(
        _outproj_kernel,
        grid=(n // tm,),
        in_specs=[row(D_MODEL), row(1024), row(512), row(512), full(km), full(vm), res(w_out_bf), full(g_ffn),
                  full(w_router_t)],
        out_specs=(row(D_MODEL), pl.BlockSpec((tm * TOKEN_TILE_ROWS, LANES), lambda i: (i, 0)),
                   pl.BlockSpec((N_EXPERTS, tm), lambda i: (0, i))),
        out_shape=(sd((n, D_MODEL), F32), sd((n * TOKEN_TILE_ROWS, LANES), U32), sd((N_EXPERTS, n), F32)),
        scratch_shapes=[pltpu.VMEM((tm, D_MODEL), BF16)],
        name="out_proj",
        compiler_params=_cparams(("parallel",)),
    )(x, attn, hgrn, mq, km, vm, w_out_bf, g_ffn, w_router_t)


def _split3(x):
    hi = x.astype(BF16)
    r1 = x - hi.astype(F32)
    mid = r1.astype(BF16)
    lo = (r1 - mid.astype(F32)).astype(BF16)
    return hi, mid, lo


def _topk_kernel(aff_ref, afft_ref, idx_ref, gate_ref, pos_ref, off_ref, cl_scr, offb_scr, totb_scr, *, cap):
    e_n, nb, _ = aff_ref.shape
    rows = e_n * nb
    aff = aff_ref[...]
    keys = lax.bitcast_convert_type(aff, I32)

    def count(mask):
        s = jnp.sum(mask.astype(F32), axis=1, keepdims=True)
        return jnp.sum(s, axis=2, keepdims=True)

    def bis(i, t):
        cand = t | (jnp.int32(1) << (30 - i))
        return jnp.where(count(keys >= cand) >= cap, cand, t)

    thr = lax.fori_loop(0, 31, bis, jnp.zeros((e_n, 1, 1), I32))
    gt = keys > thr
    eq = keys == thr
    need = cap - count(gt)

    upper = _ind(lax.broadcasted_iota(I32, (LANES, LANES), 0) <= lax.broadcasted_iota(I32, (LANES, LANES), 1))
    ones = jnp.ones((LANES, LANES), BF16)
    bi = lax.broadcasted_iota(I32, (nb, nb), 0)
    bj = lax.broadcasted_iota(I32, (nb, nb), 1)
    strict_lower = _ind(bj < bi)

    def prefix(mask):
        m2 = _ind(mask).reshape(rows, LANES)
        cl = _nn(m2, upper).reshape(e_n, nb, LANES)
        tot = _nn(m2, ones).reshape(e_n, nb, LANES)
        off = jnp.stack([_nn(strict_lower, tot[e].astype(BF16)) for e in range(e_n)], axis=0)
        return cl, off, tot

    cl, off, _ = prefix(eq)
    rank_eq = off + cl - eq.astype(F32)
    sel = gt | (eq & (rank_eq < need))
    cl, off, tot = prefix(sel)
    pos_ref[...] = jnp.where(sel, off + cl - 1.0, -1.0).astype(I32)
    off_ref[...] = off.astype(I32)
    cl_scr[...] = cl
    offb_scr[...] = off
    totb_scr[...] = tot

    lower_incl = _ind(lax.broadcasted_iota(I32, (LANES, LANES), 1) <= lax.broadcasted_iota(I32, (LANES, LANES), 0))
    reps = cap // LANES
    s_row = lax.broadcasted_iota(I32, (nb, cap), 1).astype(F32)
    b_col = lax.broadcasted_iota(I32, (nb, cap), 0).astype(F32)
    j_col = lax.broadcasted_iota(I32, (LANES, cap), 0).astype(F32)

    def per_expert(e, carry):
        sel_e = _ind(pos_ref[e] >= 0)
        clt = _nt(lower_incl, sel_e)
        offt = jnp.concatenate([offb_scr[e]] * reps, axis=1)
        endt = offt + jnp.concatenate([totb_scr[e]] * reps, axis=1)
        hit = (offt <= s_row) & (s_row < endt)
        onehot = _ind(hit)
        g_cnt = _nn(clt.astype(BF16), onehot)
        local = s_row[0:1, :] - jnp.sum(jnp.where(hit, offt, 0.0), axis=0, keepdims=True)
        j_row = jnp.sum((g_cnt <= local).astype(F32), axis=0, keepdims=True)
        b_row = jnp.sum(jnp.where(hit, b_col, 0.0), axis=0, keepdims=True)
        idx_ref[e] = (b_row * LANES + j_row).astype(I32)
        hi, mid, lo = _split3(afft_ref[e])
        g_aff = _nn(hi, onehot) + _nn(mid, onehot) + _nn(lo, onehot)
        gate_ref[e] = jnp.sum(jnp.where(j_col == j_row, g_aff, 0.0), axis=0, keepdims=True)
        return carry

    lax.fori_loop(0, e_n, per_expert, 0)


def _topk(aff_t, cap):
    e_n, n = aff_t.shape
    nb = n // LANES
    aff3 = aff_t.reshape(e_n, nb, LANES)
    afft3 = jnp.swapaxes(aff3, 1, 2)
    sd = jax.ShapeDtypeStruct
    return pl.pallas_call(
        functools.partial(_topk_kernel, cap=cap),
        out_shape=(sd((e_n, 1, cap), I32), sd((e_n, 1, cap), F32), sd((e_n, nb, LANES), I32), sd((e_n, nb, LANES), I32)),
        scratch_shapes=[pltpu.VMEM((e_n, nb, LANES), F32)] * 3,
        name="topk",
        compiler_params=_cparams(None),
    )(aff3, afft3)


GATHER_RING = 512


GATHER_ROWS = 1024


def _gather_kernel(idx_ref, src_ref, out_ref, sem):
    rows = out_ref.shape[0]
    ring = sem.shape[0]

    def copy(s, k):
        return pltpu.make_async_copy(src_ref.at[idx_ref[0, 0, s]], out_ref.at[s], sem.at[k])

    def group(gi, carry):
        for k in range(ring):
            s = gi * ring + k

            @pl.when(gi > 0)
            def _():
                copy(s - ring, k).wait()

            copy(s, k).start(priority=k % 2)
        return carry

    lax.fori_loop(0, rows // ring, group, 0)
    for k in range(ring):
        copy(rows - ring + k, k).wait()


def _gather(h2w, idx, cap):
    total = N_EXPERTS * cap
    rows = min(GATHER_ROWS, cap)
    ring = min(GATHER_RING, rows)
    assert total % rows == 0 and rows % ring == 0
    n = h2w.shape[0] // TOKEN_TILE_ROWS
    out = pl.pallas_call(
        _gather_kernel,
        grid=(total // rows,),
        in_specs=[pl.BlockSpec((1, 1, rows), lambda i: (i, 0, 0), memory_space=pltpu.SMEM),
                  pl.BlockSpec(memory_space=pl.ANY)],
        out_specs=pl.BlockSpec((rows, TOKEN_TILE_ROWS, LANES), lambda i: (i, 0, 0)),
        out_shape=jax.ShapeDtypeStruct((total, TOKEN_TILE_ROWS, LANES), U32),
        scratch_shapes=[pltpu.SemaphoreType.DMA((ring,))],
        name="gather",
        compiler_params=_cparams(("arbitrary",)),
    )(idx.reshape(total // rows, 1, rows), h2w.reshape(n, TOKEN_TILE_ROWS, LANES))
    return out.reshape(N_EXPERTS, cap * TOKEN_TILE_ROWS, LANES)


def _ffn_kernel(x_ref, gate_ref, wg_ref, wu_ref, wd_ref, o_ref, xb_ref, hid_ref, *, nf, tf):
    j = pl.program_id(1)
    t = pl.program_id(2)

    @pl.when(j == 0)
    def _():
        _, ts, d = xb_ref.shape
        for k in range(TOKEN_TILE_ROWS):
            w = x_ref[0, pl.ds(k, ts, stride=TOKEN_TILE_ROWS), :]
            xb_ref[t, :, k * LANES:(k + 1) * LANES] = lax.bitcast_convert_type(w << 16, F32).astype(BF16)
            xb_ref[t, :, d // 2 + k * LANES:d // 2 + (k + 1) * LANES] = lax.bitcast_convert_type(
                w & jnp.uint32(0xFFFF0000), F32).astype(BF16)

    @pl.when(j < nf)
    def _():
        x = xb_ref[t]
        hid = jax.nn.silu(_nn(x, wg_ref[0].astype(BF16))) * _nn(x, wu_ref[0].astype(BF16))
        hid_ref[t, :, pl.ds(pl.multiple_of(j * tf, tf), tf)] = hid.astype(BF16)

    @pl.when(j >= nf)
    def _():
        o_ref[0] = (_nn(hid_ref[t], wd_ref[0].astype(BF16)) * gate_ref[0]).astype(BF16)


def _ffn(xs, gate_col, wg, wu, wd, ts, tf, tn):
    e_n, cap = gate_col.shape[:2]
    d = D_MODEL
    nf = D_EXPERT // tf
    nt = cap // ts
    x_map = lambda e, j, t: (e, jnp.where(j == 0, t, nt - 1), 0)
    o_map = lambda e, j, t: (e, jnp.where(j < nf, 0, t), jnp.maximum(j - nf, 0))
    return pl.pallas_call(
        functools.partial(_ffn_kernel, nf=nf, tf=tf),
        grid=(e_n, nf + d // tn, nt),
        in_specs=[pl.BlockSpec((1, ts * TOKEN_TILE_ROWS, LANES), x_map),
                  pl.BlockSpec((1, ts, 1), lambda e, j, t: (e, jnp.where(j < nf, 0, t), 0)),
                  pl.BlockSpec((1, d, tf), lambda e, j, t: (e, 0, jnp.minimum(j, nf - 1))),
                  pl.BlockSpec((1, d, tf), lambda e, j, t: (e, 0, jnp.minimum(j, nf - 1))),
                  pl.BlockSpec((1, D_EXPERT, tn), lambda e, j, t: (e, 0, jnp.maximum(j - nf, 0)))],
        out_specs=pl.BlockSpec((1, ts, tn), o_map),
        out_shape=jax.ShapeDtypeStruct((e_n, cap, d), BF16),
        scratch_shapes=[pltpu.VMEM((nt, ts, d), BF16), pltpu.VMEM((nt, ts, D_EXPERT), BF16)],
        name="ffn",
        compiler_params=_cparams(("parallel", "arbitrary", "arbitrary")),
    )(xs, gate_col, wg, wu, wd)


ROW_GROUP = 16
WIN_GROUPS = 3
WIN = WIN_GROUPS * ROW_GROUP
FAR_GROUPS = LANES // ROW_GROUP + 1 - WIN_GROUPS
FAR = FAR_GROUPS * ROW_GROUP


def _combine_kernel(off_ref, tot_ref, y1_ref, post_ref, rows_ref, o_ref, buf, far, sem, far_sem, *, cap):
    b = pl.program_id(0)
    nb = pl.num_programs(0)
    n_groups = cap // ROW_GROUP

    def first_group(blk, e):
        return jnp.minimum(off_ref[e, blk] // ROW_GROUP, n_groups - WIN_GROUPS)

    def copy(blk, slot, e):
        return pltpu.make_async_copy(rows_ref.at[e, pl.ds(first_group(blk, e), WIN_GROUPS)],
                                     buf.at[slot, pl.ds(e * WIN_GROUPS, WIN_GROUPS)], sem.at[slot, e])

    def start(blk, slot):
        for e in range(N_EXPERTS):
            copy(blk, slot, e).start()

    def wait(blk, slot):
        for e in range(N_EXPERTS):
            copy(blk, slot, e).wait()

    slot = b % 2

    @pl.when(b == 0)
    def _():
        far[...] = jnp.zeros_like(far)
        start(0, 0)

    @pl.when(b + 1 < nb)
    def _():
        start(b + 1, 1 - slot)

    post = post_ref[...]
    lane = lax.broadcasted_iota(I32, (LANES, LANES), 1)

    def token_row(e):
        return jnp.broadcast_to(post[:, e:e + 1], (LANES, LANES))

    pieces = []
    for t in range(N_EXPERTS * WIN // LANES):
        miss = jnp.ones((LANES, LANES), I32)
        for e in range(t * LANES // WIN, min(N_EXPERTS, ((t + 1) * LANES - 1) // WIN + 1)):
            j = lane + (t * LANES - e * WIN)
            miss = jnp.where((j >= 0) & (j < WIN), token_row(e) - first_group(b, e) * ROW_GROUP - j, miss)
        pieces.append(_ind(miss == 0))
    w = jnp.concatenate(pieces, axis=1)
    wait(b, slot)
    o_ref[...] = y1_ref[...] + _nn(w, buf[slot].reshape(N_EXPERTS * WIN, o_ref.shape[1]))

    for e in range(N_EXPERTS):
        covered = (first_group(b, e) + WIN_GROUPS) * ROW_GROUP

        @pl.when(off_ref[e, b] + tot_ref[e, b] > covered)
        def _():
            g0 = jnp.minimum(first_group(b, e) + WIN_GROUPS, n_groups - FAR_GROUPS)
            cp = pltpu.make_async_copy(rows_ref.at[e, pl.ds(g0, FAR_GROUPS)], far.at[pl.ds(0, FAR_GROUPS)], far_sem)
            cp.start()
            cp.wait()
            row = token_row(e)
            hit = (row - g0 * ROW_GROUP == lane) & (row >= covered) & (lane < FAR)
            o_ref[...] += _nn(_ind(hit), far[...].reshape(LANES, o_ref.shape[1]))


def _combine(y1, pos_t, off, tot, rows, cap):
    n = y1.shape[0]
    nb = n // LANES
    d = y1.shape[1]
    grid_spec = pltpu.PrefetchScalarGridSpec(
        num_scalar_prefetch=2,
        grid=(nb,),
        in_specs=[pl.BlockSpec((LANES, d), lambda b, off, tot: (b, 0)),
                  pl.BlockSpec((LANES, N_EXPERTS), lambda b, off, tot: (b, 0)),
                  pl.BlockSpec(memory_space=pl.ANY)],
        out_specs=pl.BlockSpec((LANES, d), lambda b, off, tot: (b, 0)),
        scratch_shapes=[pltpu.VMEM((2, N_EXPERTS * WIN_GROUPS, ROW_GROUP, d), BF16),
                        pltpu.VMEM((LANES // ROW_GROUP, ROW_GROUP, d), BF16),
                        pltpu.SemaphoreType.DMA((2, N_EXPERTS)), pltpu.SemaphoreType.DMA(())],
    )
    assert cap % ROW_GROUP == 0 and cap >= WIN + FAR and (N_EXPERTS * WIN) % LANES == 0
    rows = rows.reshape(N_EXPERTS, cap // ROW_GROUP, ROW_GROUP, d)
    return pl.pallas_call(
        functools.partial(_combine_kernel, cap=cap),
        grid_spec=grid_spec,
        out_shape=jax.ShapeDtypeStruct((n, d), F32),
        name="combine",
        compiler_params=_cparams(("arbitrary",)),
    )(off, tot, y1, pos_t, rows)


def _rope_tables(seq_len):
    t = jnp.arange(seq_len)
    row = (t // GRID_W).astype(F32)
    col = (t % GRID_W).astype(F32)
    inv = ROPE_THETA ** (-jnp.arange(ROPE_PAIRS, dtype=F32) / ROPE_PAIRS)
    ang = jnp.stack([row[:, None] * inv, col[:, None] * inv], axis=1)
    ang = jnp.broadcast_to(ang[:, :, None, :], (seq_len, 2, 2, ROPE_PAIRS)).reshape(seq_len, HEAD_DIM)
    first_half = (jnp.arange(HEAD_DIM) % (2 * ROPE_PAIRS)) < ROPE_PAIRS
    return jnp.cos(ang), jnp.where(first_half[None, :], -jnp.sin(ang), jnp.sin(ang))


def _pick(n, pref):
    t = min(n, pref)
    assert n % t == 0, (n, pref)
    return t


def _layer(x, mem, p):
    n = x.shape[0]
    cos, sin_signed = p["rope"]
    km, vm = _mem_kv(mem, p["g_mem"], p["w_mem_kv"], p["g_mk"])
    (q, k, vt, hq, kf, gf, kb, gb, vi, vit, og, mq) = _in_proj(
        x, p["g_mix"], p["w_in"], p["wvt"], p["wvit"], cos, sin_signed, p["g_q"], p["g_k"], p["g_mq"],
        p["lb_fwd"], p["lb_bwd"], _pick(n, 512))
    attn = _attention(q, k, vt, _pick(n, 256), _pick(n, 1024))
    hgrn = _hgrn(hq, kf, gf, kb, gb, vi, vit, og, p["g_hg_out"], _pick(n, 512))
    y1, h2, aff_t = _out_proj(x, attn, hgrn, mq, km, vm, p["w_out"], p["g_ffn"], p["w_router_t"], _pick(n, 512))
    cap = EXPERT_CAPACITY_FACTOR * n // N_EXPERTS
    idx, gate, pos, off = _topk(aff_t, cap)
    xs = _gather(h2, idx, cap)
    gate_col = gate.reshape(N_EXPERTS, cap, 1)
    rows = _ffn(xs, gate_col, p["w_gate"], p["w_up"], p["w_down"], _pick(cap, 1024), 256, 512)
    pos_t = pos.reshape(N_EXPERTS, n).T
    off_s = off[:, :, 0]
    tot_s = jnp.diff(off_s, axis=1, append=jnp.full((N_EXPERTS, 1), cap, I32))
    return _combine(y1, pos_t, off_s, tot_s, rows, cap)


def kernel(x_prompt, x_sample, mem_prompt, mem_sample, g_mix, w_in, g_q, g_k, g_hg_out, lb_fwd, lb_bwd, g_mem, w_mem_kv,
           g_mq, g_mk, w_out, g_ffn, w_router, w_gate, w_up, w_down):
    assert g_mix.shape[0] == 1 and lb_fwd.shape[0] == 2, "single layer: lower bound is the first cumulative-softmax row"
    w_in_bf = w_in[0].astype(BF16)
    wv, wvi = lax.optimization_barrier((w_in[0, :, C_V:C_HQ], w_in[0, :, C_HI:C_HG]))
    p = {
        "g_mix": g_mix, "w_in": w_in_bf,
        "wvt": wv.T.astype(BF16), "wvit": wvi.T.astype(BF16),
        "g_q": g_q, "g_k": g_k, "g_hg_out": g_hg_out, "lb_fwd": lb_fwd, "lb_bwd": lb_bwd,
        "g_mem": g_mem, "w_mem_kv": w_mem_kv[0].astype(BF16), "g_mq": g_mq, "g_mk": g_mk,
        "w_out": w_out[0].astype(BF16), "g_ffn": g_ffn, "w_router_t": w_router[0].T,
        "w_gate": w_gate[0], "w_up": w_up[0], "w_down": w_down[0],
    }
    p["rope"] = _rope_tables(max(x_prompt.shape[1], x_sample.shape[1]))
    y_prompt = _layer(x_prompt[0], mem_prompt[0], p)
    y_sample = _layer(x_sample[0], mem_sample[0], p)
    return (y_prompt[None], y_sample[None])
```

```python
import functools

import jax
import jax.numpy as jnp
from jax import lax
from jax.experimental import pallas as pl
from jax.experimental.pallas import tpu as pltpu

F32 = jnp.float32
BF16 = jnp.bfloat16
I32 = jnp.int32
U32 = jnp.uint32

D_MODEL = 2048
HEAD_DIM = 128
ATTN_HEADS = 8
ATTN_KV_HEADS = 2
KV_GROUP = ATTN_HEADS // ATTN_KV_HEADS
HG_HEADS = 4
HG_WIDTH = HG_HEADS * HEAD_DIM
MEM_HEADS = 4
GRID_W = 64
ROPE_THETA = 10000.0
ROPE_PAIRS = HEAD_DIM // 4
N_EXPERTS = 16
EXPERT_CAPACITY_FACTOR = 2
D_EXPERT = 2048
EPS = 1e-6
HG_CHUNK = 128
LANES = 128
NEG_BIG = -1e30
LOG2E = 1.4426950408889634
TOKEN_TILE_ROWS = D_MODEL // 2 // LANES

Q_WIDTH = ATTN_HEADS * HEAD_DIM
KV_WIDTH = ATTN_KV_HEADS * HEAD_DIM
QG_WIDTH = KV_GROUP * HEAD_DIM
MEM_WIDTH = MEM_HEADS * HEAD_DIM


def _offsets(widths):
    out = [0]
    for w in widths:
        out.append(out[-1] + w)
    return out


C_Q, C_K, C_V, C_HQ, C_ZF, C_ZB, C_HI, C_HG, C_MQ, C_END = _offsets(
    (Q_WIDTH, KV_WIDTH, KV_WIDTH) + (HG_WIDTH,) * 5 + (MEM_WIDTH,))

VMEM_LIMIT = 56 * 1024 * 1024


def _cparams(sem, vmem=VMEM_LIMIT):
    return pltpu.CompilerParams(dimension_semantics=sem, vmem_limit_bytes=vmem)


def _nt(a, b):
    return lax.dot_general(a, b, (((1,), (1,)), ((), ())), preferred_element_type=F32)


def _nn(a, b):
    return jnp.dot(a, b, preferred_element_type=F32)


def _ind(mask):
    return jnp.where(mask, 1.0, 0.0).astype(BF16)


def _rms(x, g):
    return x * lax.rsqrt(jnp.mean(x * x, axis=-1, keepdims=True) + EPS) * g


def _memkv_kernel(mem_ref, gmem_ref, w_ref, gmk_ref, km_ref, vm_ref):
    h = _rms(mem_ref[...], gmem_ref[...]).astype(BF16)
    kv = _nn(h, w_ref[...])
    for hh in range(MEM_HEADS):
        sl = slice(hh * HEAD_DIM, (hh + 1) * HEAD_DIM)
        km_ref[:, sl] = _rms(kv[:, sl], gmk_ref[...]).astype(BF16)
    vm_ref[...] = kv[:, MEM_HEADS * HEAD_DIM:].astype(BF16)


def _mem_kv(mem, g_mem, w_mem_kv_bf, g_mk):
    m = mem.shape[0]
    width = MEM_HEADS * HEAD_DIM
    return pl.pallas_call(
        _memkv_kernel,
        out_shape=(jax.ShapeDtypeStruct((m, width), BF16), jax.ShapeDtypeStruct((m, width), BF16)),
        name="mem_kv",
        compiler_params=_cparams(None),
    )(mem, g_mem, w_mem_kv_bf, g_mk)


def _norm_rope(a, g, cos, sin_signed, first_half):
    y = _rms(a, g)
    rot = jnp.where(first_half, pltpu.roll(y, HEAD_DIM - ROPE_PAIRS, 1), pltpu.roll(y, ROPE_PAIRS, 1))
    return y * cos + rot * sin_signed


def _inproj_kernel(x_ref, gmix_ref, w_ref, wvt_ref, wvit_ref, cos_ref, sin_ref, gq_ref, gk_ref, gmq_ref, lbf_ref, lbb_ref,
                   q_ref, k_ref, vt_ref, hq_ref, kf_ref, gf_ref, kb_ref, gb_ref, vi_ref, vit_ref, og_ref, mq_ref, h_scr):
    tm = x_ref.shape[0]
    h_scr[...] = _rms(x_ref[...], gmix_ref[...]).astype(BF16)
    h = h_scr[...]
    cos = cos_ref[...]
    sin = sin_ref[...]
    lane = lax.broadcasted_iota(I32, (tm, HEAD_DIM), 1)
    first_half = (lane % (2 * ROPE_PAIRS)) < ROPE_PAIRS
    scale = HEAD_DIM ** -0.5

    def proj(c0, c1):
        return _nn(h, w_ref[:, c0:c1])

    for c in range(ATTN_KV_HEADS):
        a = proj(C_Q + QG_WIDTH * c, C_Q + QG_WIDTH * (c + 1))
        for hh in range(KV_GROUP):
            sl = slice(hh * HEAD_DIM, (hh + 1) * HEAD_DIM)
            y = _norm_rope(a[:, sl], gq_ref[...], cos, sin, first_half) * (scale * LOG2E)
            q_ref[:, QG_WIDTH * c + hh * HEAD_DIM:QG_WIDTH * c + (hh + 1) * HEAD_DIM] = y.astype(BF16)
    a = proj(C_K, C_V)
    for hh in range(ATTN_KV_HEADS):
        sl = slice(hh * HEAD_DIM, (hh + 1) * HEAD_DIM)
        k_ref[:, sl] = _norm_rope(a[:, sl], gk_ref[...], cos, sin, first_half).astype(BF16)
    vt_ref[...] = _nt(wvt_ref[...], h).astype(BF16)

    hq_ref[...] = jax.nn.silu(proj(C_HQ, C_ZF)).astype(BF16)

    def forget(z, lb_ref, k_out, g_out):
        a2 = lb_ref[...]
        e2 = jnp.exp(a2 - jnp.max(a2, axis=0, keepdims=True))
        lb = e2[0:1, :] / jnp.sum(e2, axis=0, keepdims=True)
        f = lb + (1.0 - lb) * jax.nn.sigmoid(z)
        g_out[...] = jnp.log(f) * LOG2E
        k_out[...] = ((1.0 - lb) * jax.nn.sigmoid(-z)).astype(BF16)

    forget(proj(C_ZF, C_ZB), lbf_ref, kf_ref, gf_ref)
    forget(proj(C_ZB, C_HI), lbb_ref, kb_ref, gb_ref)
    vi_ref[...] = proj(C_HI, C_HG).astype(BF16)
    vit_ref[...] = _nt(wvit_ref[...], h).astype(BF16)
    og_ref[...] = jax.nn.silu(proj(C_HG, C_MQ)).astype(BF16)
    a = proj(C_MQ, C_END)
    for hh in range(MEM_HEADS):
        sl = slice(hh * HEAD_DIM, (hh + 1) * HEAD_DIM)
        mq_ref[:, sl] = (_rms(a[:, sl], gmq_ref[...]) * (scale * LOG2E)).astype(BF16)


def _in_proj(x, g_mix, w_bf, wvt, wvit, cos, sin_signed, g_q, g_k, g_mq, lb_f, lb_b, tm):
    n = x.shape[0]
    grid = (n // tm,)
    row = lambda w: pl.BlockSpec((tm, w), lambda i: (i, 0))
    col = lambda h: pl.BlockSpec((h, tm), lambda i: (0, i))
    full = lambda a: pl.BlockSpec(a.shape, lambda i: (0,) * a.ndim)
    res = lambda a: pl.BlockSpec(a.shape, lambda i: (0,) * a.ndim, pipeline_mode=pl.Buffered(1))
    sd = jax.ShapeDtypeStruct
    hw = HG_WIDTH
    out_shape = (sd((n, Q_WIDTH), BF16), sd((n, KV_WIDTH), BF16), sd((KV_WIDTH, n), BF16),
                 sd((n, hw), BF16), sd((n, hw), BF16), sd((n, hw), F32), sd((n, hw), BF16), sd((n, hw), F32),
                 sd((n, hw), BF16), sd((hw, n), BF16), sd((n, hw), BF16), sd((n, MEM_WIDTH), BF16))
    out_specs = (row(Q_WIDTH), row(KV_WIDTH), col(KV_WIDTH), row(hw), row(hw), row(hw), row(hw), row(hw),
                 row(hw), col(hw), row(hw), row(MEM_WIDTH))
    return pl.pallas_call(
        _inproj_kernel,
        grid=grid,
        in_specs=[row(D_MODEL), full(g_mix), res(w_bf), res(wvt), res(wvit), row(HEAD_DIM), row(HEAD_DIM),
                  full(g_q), full(g_k), full(g_mq), full(lb_f), full(lb_b)],
        out_specs=out_specs,
        out_shape=out_shape,
        scratch_shapes=[pltpu.VMEM((tm, D_MODEL), BF16)],
        name="in_proj",
        compiler_params=_cparams(("parallel",)),
    )(x, g_mix, w_bf, wvt, wvit, cos, sin_signed, g_q, g_k, g_mq, lb_f, lb_b)


ONES_ROWS = 16


def _attn_kernel(q_ref, k_ref, vt_ref, o_ref, acc_ref, s_ref, *, tk, nk):
    tq = q_ref.shape[0]
    acc_ref[...] = jnp.zeros_like(acc_ref)
    ones = jnp.ones((ONES_ROWS, tk), BF16)

    def scores(hh, off):
        s_ref[hh] = _nt(k_ref[pl.ds(off, tk), :], q_ref[:, hh * HEAD_DIM:(hh + 1) * HEAD_DIM])

    lead = 2
    for hh in range(lead):
        scores(hh, 0)

    def body(i, ms):
        off = pl.multiple_of(i * tk, tk)
        off_next = pl.multiple_of(jnp.minimum(i + 1, nk - 1) * tk, tk)
        vx = jnp.concatenate([vt_ref[:, pl.ds(off, tk)], ones], axis=0)
        out = []
        for hh in range(KV_GROUP):
            nxt = hh + lead
            scores(nxt % KV_GROUP, off if nxt < KV_GROUP else off_next)
            s = s_ref[hh]
            m_new = jnp.maximum(ms[hh], jnp.max(s, axis=0, keepdims=True))
            p = jnp.exp2((s - m_new).astype(BF16))
            alpha = jnp.exp2(ms[hh] - m_new)
            acc_ref[hh] = alpha * acc_ref[hh] + _nn(vx, p)
            out.append(m_new)
        return tuple(out)

    init = tuple(jnp.full((1, tq), NEG_BIG, F32) for _ in range(KV_GROUP))
    lax.fori_loop(0, nk, body, init, unroll=max(1, min(8, nk // 2)))
    for hh in range(KV_GROUP):
        a = acc_ref[hh]
        o_ref[:, hh * HEAD_DIM:(hh + 1) * HEAD_DIM] = (a[0:HEAD_DIM] / a[HEAD_DIM:HEAD_DIM + 1]).T.astype(BF16)


def _attention(q, k, vt, tq, tk):
    s = q.shape[0]
    width = KV_GROUP * HEAD_DIM
    return pl.pallas_call(
        functools.partial(_attn_kernel, tk=tk, nk=s // tk),
        grid=(ATTN_KV_HEADS, s // tq),
        in_specs=[pl.BlockSpec((tq, width), lambda g, i: (i, g)),
                  pl.BlockSpec((s, HEAD_DIM), lambda g, i: (0, g)),
                  pl.BlockSpec((HEAD_DIM, s), lambda g, i: (g, 0))],
        out_specs=pl.BlockSpec((tq, width), lambda g, i: (i, g)),
        out_shape=jax.ShapeDtypeStruct((s, ATTN_HEADS * HEAD_DIM), BF16),
        scratch_shapes=[pltpu.VMEM((KV_GROUP, HEAD_DIM + ONES_ROWS, tq), F32), pltpu.VMEM((KV_GROUP, tk, tq), F32)],
        name="attention",
        compiler_params=_cparams(("parallel", "parallel")),
    )(q, k, vt)


def _hgrn_levels():
    b, out = HG_CHUNK, []
    while b >= 2:
        out.append(b)
        b //= 2
    return out


def _hgrn_pair_masks(rev):
    c = HG_CHUNK
    ti = lax.broadcasted_iota(I32, (c, c), 0)
    si = lax.broadcasted_iota(I32, (c, c), 1)
    row = lax.broadcasted_iota(I32, (c, HG_WIDTH), 0)
    pairs, signs = [], []
    for blk in _hgrn_levels():
        half = blk // 2
        same = (ti // blk) == (si // blk)
        if rev:
            m = same & (ti % blk < half) & (si % blk >= half)
            is_q = row % blk < half
        else:
            m = same & (ti % blk >= half) & (si % blk < half)
            is_q = row % blk >= half
        pairs.append(jnp.where(m, 1.0, 0.0))
        signs.append(jnp.where(is_q, 1.0, -1.0))
    return jnp.where(ti == si, 1.0, 0.0), pairs, signs


def _hgrn_chunk(q, k, g, v, vt, state_ref, rev, masks):
    c = HG_CHUNK
    w = HG_WIDTH
    row = lax.broadcasted_iota(I32, (c, w), 0)
    b = g
    sh = 1
    while sh < c:
        if rev:
            b = b + jnp.where(row < c - sh, pltpu.roll(b, c - sh, 0), 0.0)
        else:
            b = b + jnp.where(row >= sh, pltpu.roll(b, sh, 0), 0.0)
        sh *= 2
    qf = q.astype(F32)
    kf = k.astype(F32)
    eye, pair_masks, signs = masks
    att = [eye * _nt(q[:, h * HEAD_DIM:(h + 1) * HEAD_DIM], k[:, h * HEAD_DIM:(h + 1) * HEAD_DIM])
           for h in range(HG_HEADS)]
    for blk, pair, sign in zip(_hgrn_levels(), pair_masks, signs):
        half = blk // 2
        ref_row = half if rev else half - 1
        pos = row % blk
        if blk >= 8:
            r = jnp.concatenate(
                [jnp.broadcast_to(b[s0 + ref_row:s0 + ref_row + 1, :], (blk, w)) for s0 in range(0, c, blk)], axis=0)
        else:
            r = b
            for d in range(-ref_row, blk - ref_row):
                if d != 0:
                    r = jnp.where(pos - ref_row == d, pltpu.roll(b, d % c, 0), r)
        e = jnp.exp2((b - r) * sign)
        is_q = (pos < half) if rev else (pos >= half)
        x = (jnp.where(is_q, qf, kf) * e).astype(BF16)
        for h in range(HG_HEADS):
            sl = slice(h * HEAD_DIM, (h + 1) * HEAD_DIM)
            att[h] = att[h] + pair * _nt(x[:, sl], x[:, sl])
    b_end = b[0:1, :] if rev else b[c - 1:c, :]
    qd = (qf * jnp.exp2(b)).astype(BF16)
    kd = (kf * jnp.exp2(b_end - b)).astype(BF16)
    dec = jnp.exp2(b_end)
    outs = []
    for h in range(HG_HEADS):
        sl = slice(h * HEAD_DIM, (h + 1) * HEAD_DIM)
        st = state_ref[h]
        o = _nn(att[h].astype(BF16), v[:, sl]) + _nt(qd[:, sl], st.astype(BF16))
        state_ref[h] = dec[:, sl] * st + _nn(vt[sl, :], kd[:, sl])
        outs.append(o)
    return jnp.concatenate(outs, axis=1)


def _hgrn_fwd_kernel(q_ref, k_ref, g_ref, v_ref, vt_ref, o_ref, state_ref, *, nch):
    @pl.when(pl.program_id(0) == 0)
    def _():
        state_ref[...] = jnp.zeros_like(state_ref)

    masks = _hgrn_pair_masks(False)

    def body(i, carry):
        r0 = pl.multiple_of(i * HG_CHUNK, HG_CHUNK)
        rs = pl.ds(r0, HG_CHUNK)
        o_ref[rs, :] = _hgrn_chunk(q_ref[rs, :], k_ref[rs, :], g_ref[rs, :], v_ref[rs, :], vt_ref[:, rs], state_ref, False,
                                   masks)
        return carry

    lax.fori_loop(0, nch, body, 0)


def _hgrn_bwd_kernel(q_ref, k_ref, g_ref, v_ref, vt_ref, of_ref, og_ref, gout_ref, o_ref, state_ref, *, nch):
    @pl.when(pl.program_id(0) == 0)
    def _():
        state_ref[...] = jnp.zeros_like(state_ref)

    masks = _hgrn_pair_masks(True)

    def body(i, carry):
        r0 = pl.multiple_of((nch - 1 - i) * HG_CHUNK, HG_CHUNK)
        rs = pl.ds(r0, HG_CHUNK)
        o = _hgrn_chunk(q_ref[rs, :], k_ref[rs, :], g_ref[rs, :], v_ref[rs, :], vt_ref[:, rs], state_ref, True, masks)
        o = o + of_ref[rs, :]
        og = og_ref[rs, :].astype(F32)
        for h in range(HG_HEADS):
            sl = slice(h * HEAD_DIM, (h + 1) * HEAD_DIM)
            o_ref[rs, sl] = (_rms(o[:, sl], gout_ref[...]) * og[:, sl]).astype(BF16)
        return carry

    lax.fori_loop(0, nch, body, 0)


def _hgrn(hq, kf, gf, kb, gb, vi, vit, og, g_out, tb):
    n = hq.shape[0]
    nblk = n // tb
    nch = tb // HG_CHUNK
    state = pltpu.VMEM((HG_HEADS, HEAD_DIM, HEAD_DIM), F32)
    fr = lambda i: (i, 0)
    fc = lambda i: (0, i)
    o_f = pl.pallas_call(
        functools.partial(_hgrn_fwd_kernel, nch=nch),
        grid=(nblk,),
        in_specs=[pl.BlockSpec((tb, HG_WIDTH), fr)] * 4 + [pl.BlockSpec((HG_WIDTH, tb), fc)],
        out_specs=pl.BlockSpec((tb, HG_WIDTH), fr),
        out_shape=jax.ShapeDtypeStruct((n, HG_WIDTH), F32),
        scratch_shapes=[state],
        name="hgrn_fwd",
        compiler_params=_cparams(("arbitrary",)),
    )(hq, kf, gf, vi, vit)
    br = lambda i: (nblk - 1 - i, 0)
    bc = lambda i: (0, nblk - 1 - i)
    return pl.pallas_call(
        functools.partial(_hgrn_bwd_kernel, nch=nch),
        grid=(nblk,),
        in_specs=[pl.BlockSpec((tb, HG_WIDTH), br)] * 4 + [pl.BlockSpec((HG_WIDTH, tb), bc)]
        + [pl.BlockSpec((tb, HG_WIDTH), br)] * 2 + [pl.BlockSpec(g_out.shape, lambda i: (0, 0))],
        out_specs=pl.BlockSpec((tb, HG_WIDTH), br),
        out_shape=jax.ShapeDtypeStruct((n, HG_WIDTH), BF16),
        scratch_shapes=[state],
        name="hgrn_bwd",
        compiler_params=_cparams(("arbitrary",)),
    )(hq, kb, gb, vi, vit, o_f, og, g_out)


def _outproj_kernel(x_ref, attn_ref, hgrn_ref, mq_ref, km_ref, vm_ref, w_ref, gffn_ref, wrt_ref,
                    y_ref, h2_ref, aff_ref, mix_ref):
    n_attn = ATTN_HEADS * HEAD_DIM
    mix_ref[:, 0:n_attn] = attn_ref[...]
    mix_ref[:, n_attn:n_attn + HG_WIDTH] = hgrn_ref[...]
    for hh in range(MEM_HEADS):
        sl = slice(hh * HEAD_DIM, (hh + 1) * HEAD_DIM)
        s = _nt(mq_ref[:, sl], km_ref[:, sl])
        p = jnp.exp2(s - jnp.max(s, axis=-1, keepdims=True))
        p = p / jnp.sum(p, axis=-1, keepdims=True)
        c0 = n_attn + HG_WIDTH + hh * HEAD_DIM
        mix_ref[:, c0:c0 + HEAD_DIM] = _nn(p.astype(BF16), vm_ref[:, sl]).astype(BF16)
    y = x_ref[...] + _nn(mix_ref[...], w_ref[...])
    y_ref[...] = y
    h2 = _rms(y, gffn_ref[...])
    half = D_MODEL // 2
    lo = lax.bitcast_convert_type(h2[:, :half].astype(BF16).astype(F32), U32) >> 16
    hi = lax.bitcast_convert_type(h2[:, half:].astype(BF16).astype(F32), U32) & jnp.uint32(0xFFFF0000)
    word = hi | lo
    tm = word.shape[0]
    for k in range(half // LANES):
        h2_ref[pl.ds(k, tm, stride=half // LANES), :] = word[:, k * LANES:(k + 1) * LANES]
    h_hi = h2.astype(BF16)
    h_lo = (h2 - h_hi.astype(F32)).astype(BF16)
    wr = wrt_ref[...]
    w_hi = wr.astype(BF16)
    w_mid = (wr - w_hi.astype(F32)).astype(BF16)
    two = _nt(jnp.concatenate([w_hi, w_mid], axis=0), h_hi)
    logits = two[:N_EXPERTS] + two[N_EXPERTS:] + _nt(w_hi, h_lo)
    e = jnp.exp(logits - jnp.max(logits, axis=0, keepdims=True))
    aff_ref[...] = e / jnp.sum(e, axis=0, keepdims=True)


def _out_proj(x, attn, hgrn, mq, km, vm, w_out_bf, g_ffn, w_router_t, tm):
    n = x.shape[0]
    row = lambda w: pl.BlockSpec((tm, w), lambda i: (i, 0))
    full = lambda a: pl.BlockSpec(a.shape, lambda i: (0,) * a.ndim)
    res = lambda a: pl.BlockSpec(a.shape, lambda i: (0,) * a.ndim, pipeline_mode=pl.Buffered(1))
    sd = jax.ShapeDtypeStruct
    return pl.pallas_call(
        _outproj_kernel,
        grid=(n // tm,),
        in_specs=[row(D_MODEL), row(Q_WIDTH), row(HG_WIDTH), row(MEM_WIDTH), full(km), full(vm), res(w_out_bf), full(g_ffn),
                  full(w_router_t)],
        out_specs=(row(D_MODEL), pl.BlockSpec((tm * TOKEN_TILE_ROWS, LANES), lambda i: (i, 0)),
                   pl.BlockSpec((N_EXPERTS, tm), lambda i: (0, i))),
        out_shape=(sd((n, D_MODEL), F32), sd((n * TOKEN_TILE_ROWS, LANES), U32), sd((N_EXPERTS, n), F32)),
        scratch_shapes=[pltpu.VMEM((tm, D_MODEL), BF16)],
        name="out_proj",
        compiler_params=_cparams(("parallel",)),
    )(x, attn, hgrn, mq, km, vm, w_out_bf, g_ffn, w_router_t)


def _split3(x):
    hi = x.astype(BF16)
    r1 = x - hi.astype(F32)
    mid = r1.astype(BF16)
    lo = (r1 - mid.astype(F32)).astype(BF16)
    return hi, mid, lo


def _topk_kernel(aff_ref, afft_ref, idx_ref, gate_ref, pos_ref, off_ref, cl_scr, offb_scr, totb_scr, *, cap):
    e_n, nb, _ = aff_ref.shape
    rows = e_n * nb
    aff = aff_ref[...]
    keys = lax.bitcast_convert_type(aff, I32)

    def count(mask):
        s = jnp.sum(mask.astype(F32), axis=1, keepdims=True)
        return jnp.sum(s, axis=2, keepdims=True)

    def bis(i, t):
        cand = t | (jnp.int32(1) << (30 - i))
        return jnp.where(count(keys >= cand) >= cap, cand, t)

    thr = lax.fori_loop(0, 31, bis, jnp.zeros((e_n, 1, 1), I32))
    gt = keys > thr
    eq = keys == thr
    need = cap - count(gt)

    upper = _ind(lax.broadcasted_iota(I32, (LANES, LANES), 0) <= lax.broadcasted_iota(I32, (LANES, LANES), 1))
    ones = jnp.ones((LANES, LANES), BF16)
    bi = lax.broadcasted_iota(I32, (nb, nb), 0)
    bj = lax.broadcasted_iota(I32, (nb, nb), 1)
    strict_lower = _ind(bj < bi)

    def prefix(mask):
        m2 = _ind(mask).reshape(rows, LANES)
        cl = _nn(m2, upper).reshape(e_n, nb, LANES)
        tot = _nn(m2, ones).reshape(e_n, nb, LANES)
        off = jnp.stack([_nn(strict_lower, tot[e].astype(BF16)) for e in range(e_n)], axis=0)
        return cl, off, tot

    cl, off, _ = prefix(eq)
    rank_eq = off + cl - eq.astype(F32)
    sel = gt | (eq & (rank_eq < need))
    cl, off, tot = prefix(sel)
    pos_ref[...] = jnp.where(sel, off + cl - 1.0, -1.0).astype(I32)
    off_ref[...] = off.astype(I32)
    cl_scr[...] = cl
    offb_scr[...] = off
    totb_scr[...] = tot

    lower_incl = _ind(lax.broadcasted_iota(I32, (LANES, LANES), 1) <= lax.broadcasted_iota(I32, (LANES, LANES), 0))
    reps = cap // LANES
    s_row = lax.broadcasted_iota(I32, (nb, cap), 1).astype(F32)
    b_col = lax.broadcasted_iota(I32, (nb, cap), 0).astype(F32)
    j_col = lax.broadcasted_iota(I32, (LANES, cap), 0).astype(F32)

    def per_expert(e, carry):
        sel_e = _ind(pos_ref[e] >= 0)
        clt = _nt(lower_incl, sel_e)
        offt = jnp.concatenate([offb_scr[e]] * reps, axis=1)
        endt = offt + jnp.concatenate([totb_scr[e]] * reps, axis=1)
        hit = (offt <= s_row) & (s_row < endt)
        onehot = _ind(hit)
        g_cnt = _nn(clt.astype(BF16), onehot)
        local = s_row[0:1, :] - jnp.sum(jnp.where(hit, offt, 0.0), axis=0, keepdims=True)
        j_row = jnp.sum((g_cnt <= local).astype(F32), axis=0, keepdims=True)
        b_row = jnp.sum(jnp.where(hit, b_col, 0.0), axis=0, keepdims=True)
        idx_ref[e] = (b_row * LANES + j_row).astype(I32)
        hi, mid, lo = _split3(afft_ref[e])
        g_aff = _nn(hi, onehot) + _nn(mid, onehot) + _nn(lo, onehot)
        gate_ref[e] = jnp.sum(jnp.where(j_col == j_row, g_aff, 0.0), axis=0, keepdims=True)
        return carry

    lax.fori_loop(0, e_n, per_expert, 0)


def _topk(aff_t, cap):
    e_n, n = aff_t.shape
    nb = n // LANES
    aff3 = aff_t.reshape(e_n, nb, LANES)
    afft3 = jnp.swapaxes(aff3, 1, 2)
    sd = jax.ShapeDtypeStruct
    return pl.pallas_call(
        functools.partial(_topk_kernel, cap=cap),
        out_shape=(sd((e_n, 1, cap), I32), sd((e_n, 1, cap), F32), sd((e_n, nb, LANES), I32), sd((e_n, nb, LANES), I32)),
        scratch_shapes=[pltpu.VMEM((e_n, nb, LANES), F32)] * 3,
        name="topk",
        compiler_params=_cparams(None),
    )(aff3, afft3)


GATHER_RING = 512


GATHER_ROWS = 1024


def _gather_kernel(idx_ref, src_ref, out_ref, sem):
    rows = out_ref.shape[0]
    ring = sem.shape[0]

    def copy(s, k):
        return pltpu.make_async_copy(src_ref.at[idx_ref[0, 0, s]], out_ref.at[s], sem.at[k])

    def group(gi, carry):
        for k in range(ring):
            s = gi * ring + k

            @pl.when(gi > 0)
            def _():
                copy(s - ring, k).wait()

            copy(s, k).start(priority=k % 2)
        return carry

    lax.fori_loop(0, rows // ring, group, 0)
    for k in range(ring):
        copy(rows - ring + k, k).wait()


def _gather(h2w, idx, cap):
    total = N_EXPERTS * cap
    rows = min(GATHER_ROWS, cap)
    ring = min(GATHER_RING, rows)
    assert total % rows == 0 and rows % ring == 0
    n = h2w.shape[0] // TOKEN_TILE_ROWS
    out = pl.pallas_call(
        _gather_kernel,
        grid=(total // rows,),
        in_specs=[pl.BlockSpec((1, 1, rows), lambda i: (i, 0, 0), memory_space=pltpu.SMEM),
                  pl.BlockSpec(memory_space=pl.ANY)],
        out_specs=pl.BlockSpec((rows, TOKEN_TILE_ROWS, LANES), lambda i: (i, 0, 0)),
        out_shape=jax.ShapeDtypeStruct((total, TOKEN_TILE_ROWS, LANES), U32),
        scratch_shapes=[pltpu.SemaphoreType.DMA((ring,))],
        name="gather",
        compiler_params=_cparams(("arbitrary",)),
    )(idx.reshape(total // rows, 1, rows), h2w.reshape(n, TOKEN_TILE_ROWS, LANES))
    return out.reshape(N_EXPERTS, cap * TOKEN_TILE_ROWS, LANES)


def _ffn_kernel(x_ref, gate_ref, wg_ref, wu_ref, wd_ref, o_ref, xb_ref, hid_ref, *, nf, tf):
    j = pl.program_id(1)
    t = pl.program_id(2)

    @pl.when(j == 0)
    def _():
        _, ts, d = xb_ref.shape
        for k in range(TOKEN_TILE_ROWS):
            w = x_ref[0, pl.ds(k, ts, stride=TOKEN_TILE_ROWS), :]
            xb_ref[t, :, k * LANES:(k + 1) * LANES] = lax.bitcast_convert_type(w << 16, F32).astype(BF16)
            xb_ref[t, :, d // 2 + k * LANES:d // 2 + (k + 1) * LANES] = lax.bitcast_convert_type(
                w & jnp.uint32(0xFFFF0000), F32).astype(BF16)

    @pl.when(j < nf)
    def _():
        x = xb_ref[t]
        hid = jax.nn.silu(_nn(x, wg_ref[0].astype(BF16))) * _nn(x, wu_ref[0].astype(BF16))
        hid_ref[t, :, pl.ds(pl.multiple_of(j * tf, tf), tf)] = hid.astype(BF16)

    @pl.when(j >= nf)
    def _():
        o_ref[0] = (_nn(hid_ref[t], wd_ref[0].astype(BF16)) * gate_ref[0]).astype(BF16)


def _ffn(xs, gate_col, wg, wu, wd, ts, tf, tn):
    e_n, cap = gate_col.shape[:2]
    d = D_MODEL
    nf = D_EXPERT // tf
    nt = cap // ts
    x_map = lambda e, j, t: (e, jnp.where(j == 0, t, nt - 1), 0)
    o_map = lambda e, j, t: (e, jnp.where(j < nf, 0, t), jnp.maximum(j - nf, 0))
    return pl.pallas_call(
        functools.partial(_ffn_kernel, nf=nf, tf=tf),
        grid=(e_n, nf + d // tn, nt),
        in_specs=[pl.BlockSpec((1, ts * TOKEN_TILE_ROWS, LANES), x_map),
                  pl.BlockSpec((1, ts, 1), lambda e, j, t: (e, jnp.where(j < nf, 0, t), 0)),
                  pl.BlockSpec((1, d, tf), lambda e, j, t: (e, 0, jnp.minimum(j, nf - 1))),
                  pl.BlockSpec((1, d, tf), lambda e, j, t: (e, 0, jnp.minimum(j, nf - 1))),
                  pl.BlockSpec((1, D_EXPERT, tn), lambda e, j, t: (e, 0, jnp.maximum(j - nf, 0)))],
        out_specs=pl.BlockSpec((1, ts, tn), o_map),
        out_shape=jax.ShapeDtypeStruct((e_n, cap, d), BF16),
        scratch_shapes=[pltpu.VMEM((nt, ts, d), BF16), pltpu.VMEM((nt, ts, D_EXPERT), BF16)],
        name="ffn",
        compiler_params=_cparams(("parallel", "arbitrary", "arbitrary")),
    )(xs, gate_col, wg, wu, wd)


ROW_GROUP = 16
WIN_GROUPS = 3
WIN = WIN_GROUPS * ROW_GROUP
FAR_GROUPS = LANES // ROW_GROUP + 1 - WIN_GROUPS
FAR = FAR_GROUPS * ROW_GROUP


def _combine_kernel(off_ref, tot_ref, y1_ref, post_ref, rows_ref, o_ref, buf, far, sem, far_sem, *, cap):
    b = pl.program_id(0)
    nb = pl.num_programs(0)
    n_groups = cap // ROW_GROUP

    def first_group(blk, e):
        return jnp.minimum(off_ref[e, blk] // ROW_GROUP, n_groups - WIN_GROUPS)

    def copy(blk, slot, e):
        return pltpu.make_async_copy(rows_ref.at[e, pl.ds(first_group(blk, e), WIN_GROUPS)],
                                     buf.at[slot, pl.ds(e * WIN_GROUPS, WIN_GROUPS)], sem.at[slot, e])

    def start(blk, slot):
        for e in range(N_EXPERTS):
            copy(blk, slot, e).start()

    def wait(blk, slot):
        for e in range(N_EXPERTS):
            copy(blk, slot, e).wait()

    slot = b % 2

    @pl.when(b == 0)
    def _():
        far[...] = jnp.zeros_like(far)
        start(0, 0)

    @pl.when(b + 1 < nb)
    def _():
        start(b + 1, 1 - slot)

    post = post_ref[...]
    lane = lax.broadcasted_iota(I32, (LANES, LANES), 1)

    def token_row(e):
        return jnp.broadcast_to(post[:, e:e + 1], (LANES, LANES))

    pieces = []
    for t in range(N_EXPERTS * WIN // LANES):
        miss = jnp.ones((LANES, LANES), I32)
        for e in range(t * LANES // WIN, min(N_EXPERTS, ((t + 1) * LANES - 1) // WIN + 1)):
            j = lane + (t * LANES - e * WIN)
            miss = jnp.where((j >= 0) & (j < WIN), token_row(e) - first_group(b, e) * ROW_GROUP - j, miss)
        pieces.append(_ind(miss == 0))
    w = jnp.concatenate(pieces, axis=1)
    wait(b, slot)
    o_ref[...] = y1_ref[...] + _nn(w, buf[slot].reshape(N_EXPERTS * WIN, o_ref.shape[1]))

    for e in range(N_EXPERTS):
        covered = (first_group(b, e) + WIN_GROUPS) * ROW_GROUP

        @pl.when(off_ref[e, b] + tot_ref[e, b] > covered)
        def _():
            g0 = jnp.minimum(first_group(b, e) + WIN_GROUPS, n_groups - FAR_GROUPS)
            cp = pltpu.make_async_copy(rows_ref.at[e, pl.ds(g0, FAR_GROUPS)], far.at[pl.ds(0, FAR_GROUPS)], far_sem)
            cp.start()
            cp.wait()
            row = token_row(e)
            hit = (row - g0 * ROW_GROUP == lane) & (row >= covered) & (lane < FAR)
            o_ref[...] += _nn(_ind(hit), far[...].reshape(LANES, o_ref.shape[1]))


def _combine(y1, pos_t, off, tot, rows, cap):
    n = y1.shape[0]
    nb = n // LANES
    d = y1.shape[1]
    grid_spec = pltpu.PrefetchScalarGridSpec(
        num_scalar_prefetch=2,
        grid=(nb,),
        in_specs=[pl.BlockSpec((LANES, d), lambda b, off, tot: (b, 0)),
                  pl.BlockSpec((LANES, N_EXPERTS), lambda b, off, tot: (b, 0)),
                  pl.BlockSpec(memory_space=pl.ANY)],
        out_specs=pl.BlockSpec((LANES, d), lambda b, off, tot: (b, 0)),
        scratch_shapes=[pltpu.VMEM((2, N_EXPERTS * WIN_GROUPS, ROW_GROUP, d), BF16),
                        pltpu.VMEM((LANES // ROW_GROUP, ROW_GROUP, d), BF16),
                        pltpu.SemaphoreType.DMA((2, N_EXPERTS)), pltpu.SemaphoreType.DMA(())],
    )
    assert cap % ROW_GROUP == 0 and cap >= WIN + FAR and (N_EXPERTS * WIN) % LANES == 0
    rows = rows.reshape(N_EXPERTS, cap // ROW_GROUP, ROW_GROUP, d)
    return pl.pallas_call(
        functools.partial(_combine_kernel, cap=cap),
        grid_spec=grid_spec,
        out_shape=jax.ShapeDtypeStruct((n, d), F32),
        name="combine",
        compiler_params=_cparams(("arbitrary",)),
    )(off, tot, y1, pos_t, rows)


def _rope_tables(seq_len):
    t = jnp.arange(seq_len)
    row = (t // GRID_W).astype(F32)
    col = (t % GRID_W).astype(F32)
    inv = ROPE_THETA ** (-jnp.arange(ROPE_PAIRS, dtype=F32) / ROPE_PAIRS)
    ang = jnp.stack([row[:, None] * inv, col[:, None] * inv], axis=1)
    ang = jnp.broadcast_to(ang[:, :, None, :], (seq_len, 2, 2, ROPE_PAIRS)).reshape(seq_len, HEAD_DIM)
    first_half = (jnp.arange(HEAD_DIM) % (2 * ROPE_PAIRS)) < ROPE_PAIRS
    return jnp.cos(ang), jnp.where(first_half[None, :], -jnp.sin(ang), jnp.sin(ang))


def _pick(n, pref):
    t = min(n, pref)
    assert n % t == 0, (n, pref)
    return t


def _layer(x, mem, p):
    n = x.shape[0]
    cos, sin_signed = p["rope"]
    km, vm = _mem_kv(mem, p["g_mem"], p["w_mem_kv"], p["g_mk"])
    (q, k, vt, hq, kf, gf, kb, gb, vi, vit, og, mq) = _in_proj(
        x, p["g_mix"], p["w_in"], p["wvt"], p["wvit"], cos, sin_signed, p["g_q"], p["g_k"], p["g_mq"],
        p["lb_fwd"], p["lb_bwd"], _pick(n, 512))
    attn = _attention(q, k, vt, _pick(n, 256), _pick(n, 1024))
    hgrn = _hgrn(hq, kf, gf, kb, gb, vi, vit, og, p["g_hg_out"], _pick(n, 512))
    y1, h2, aff_t = _out_proj(x, attn, hgrn, mq, km, vm, p["w_out"], p["g_ffn"], p["w_router_t"], _pick(n, 512))
    cap = EXPERT_CAPACITY_FACTOR * n // N_EXPERTS
    idx, gate, pos, off = _topk(aff_t, cap)
    xs = _gather(h2, idx, cap)
    gate_col = gate.reshape(N_EXPERTS, cap, 1)
    rows = _ffn(xs, gate_col, p["w_gate"], p["w_up"], p["w_down"], _pick(cap, 1024), 256, 512)
    pos_t = pos.reshape(N_EXPERTS, n).T
    off_s = off[:, :, 0]
    tot_s = jnp.diff(off_s, axis=1, append=jnp.full((N_EXPERTS, 1), cap, I32))
    return _combine(y1, pos_t, off_s, tot_s, rows, cap)


def kernel(x_prompt, x_sample, mem_prompt, mem_sample, g_mix, w_in, g_q, g_k, g_hg_out, lb_fwd, lb_bwd, g_mem, w_mem_kv,
           g_mq, g_mk, w_out, g_ffn, w_router, w_gate, w_up, w_down):
    assert g_mix.shape[0] == 1 and lb_fwd.shape[0] == 2, "single layer: lower bound is the first cumulative-softmax row"
    w_in_bf = w_in[0].astype(BF16)
    wv, wvi = lax.optimization_barrier((w_in[0, :, C_V:C_HQ], w_in[0, :, C_HI:C_HG]))
    p = {
        "g_mix": g_mix, "w_in": w_in_bf,
        "wvt": wv.T.astype(BF16), "wvit": wvi.T.astype(BF16),
        "g_q": g_q, "g_k": g_k, "g_hg_out": g_hg_out, "lb_fwd": lb_fwd, "lb_bwd": lb_bwd,
        "g_mem": g_mem, "w_mem_kv": w_mem_kv[0].astype(BF16), "g_mq": g_mq, "g_mk": g_mk,
        "w_out": w_out[0].astype(BF16), "g_ffn": g_ffn, "w_router_t": w_router[0].T,
        "w_gate": w_gate[0], "w_up": w_up[0], "w_down": w_down[0],
    }
    p["rope"] = _rope_tables(max(x_prompt.shape[1], x_sample.shape[1]))
    y_prompt = _layer(x_prompt[0], mem_prompt[0], p)
    y_sample = _layer(x_sample[0], mem_sample[0], p)
    return (y_prompt[None], y_sample[None])
```

```python
import functools

import jax
import jax.numpy as jnp
from jax import lax
from jax.experimental import pallas as pl
from jax.experimental.pallas import tpu as pltpu

F32 = jnp.float32
BF16 = jnp.bfloat16
I32 = jnp.int32
U32 = jnp.uint32

D_MODEL = 2048
HEAD_DIM = 128
ATTN_HEADS = 8
ATTN_KV_HEADS = 2
KV_GROUP = ATTN_HEADS // ATTN_KV_HEADS
HG_HEADS = 4
HG_WIDTH = HG_HEADS * HEAD_DIM
MEM_HEADS = 4
GRID_W = 64
ROPE_THETA = 10000.0
ROPE_PAIRS = HEAD_DIM // 4
N_EXPERTS = 16
EXPERT_CAPACITY_FACTOR = 2
D_EXPERT = 2048
EPS = 1e-6
HG_CHUNK = 128
LANES = 128
NEG_BIG = -1e30
LOG2E = 1.4426950408889634
TOKEN_TILE_ROWS = D_MODEL // 2 // LANES

Q_WIDTH = ATTN_HEADS * HEAD_DIM
KV_WIDTH = ATTN_KV_HEADS * HEAD_DIM
QG_WIDTH = KV_GROUP * HEAD_DIM
MEM_WIDTH = MEM_HEADS * HEAD_DIM


def _offsets(widths):
    out = [0]
    for w in widths:
        out.append(out[-1] + w)
    return out


C_Q, C_K, C_V, C_HQ, C_ZF, C_ZB, C_HI, C_HG, C_MQ, C_END = _offsets(
    (Q_WIDTH, KV_WIDTH, KV_WIDTH) + (HG_WIDTH,) * 5 + (MEM_WIDTH,))

VMEM_LIMIT = 56 * 1024 * 1024


def _cparams(sem, vmem=VMEM_LIMIT):
    return pltpu.CompilerParams(dimension_semantics=sem, vmem_limit_bytes=vmem)


def _nt(a, b):
    return lax.dot_general(a, b, (((1,), (1,)), ((), ())), preferred_element_type=F32)


def _nn(a, b):
    return jnp.dot(a, b, preferred_element_type=F32)


def _ind(mask):
    return jnp.where(mask, 1.0, 0.0).astype(BF16)


def _rms(x, g):
    return x * lax.rsqrt(jnp.mean(x * x, axis=-1, keepdims=True) + EPS) * g


def _memkv_kernel(mem_ref, gmem_ref, w_ref, gmk_ref, km_ref, vm_ref):
    h = _rms(mem_ref[...], gmem_ref[...]).astype(BF16)
    kv = _nn(h, w_ref[...])
    for hh in range(MEM_HEADS):
        sl = slice(hh * HEAD_DIM, (hh + 1) * HEAD_DIM)
        km_ref[:, sl] = _rms(kv[:, sl], gmk_ref[...]).astype(BF16)
    vm_ref[...] = kv[:, MEM_HEADS * HEAD_DIM:].astype(BF16)


def _mem_kv(mem, g_mem, w_mem_kv_bf, g_mk):
    m = mem.shape[0]
    width = MEM_HEADS * HEAD_DIM
    return pl.pallas_call(
        _memkv_kernel,
        out_shape=(jax.ShapeDtypeStruct((m, width), BF16), jax.ShapeDtypeStruct((m, width), BF16)),
        name="mem_kv",
        compiler_params=_cparams(None),
    )(mem, g_mem, w_mem_kv_bf, g_mk)


def _norm_rope(a, g, cos, sin_signed, first_half):
    y = _rms(a, g)
    rot = jnp.where(first_half, pltpu.roll(y, HEAD_DIM - ROPE_PAIRS, 1), pltpu.roll(y, ROPE_PAIRS, 1))
    return y * cos + rot * sin_signed


def _inproj_kernel(x_ref, gmix_ref, w_ref, wvt_ref, wvit_ref, cos_ref, sin_ref, gq_ref, gk_ref, gmq_ref, lbf_ref, lbb_ref,
                   q_ref, k_ref, vt_ref, hq_ref, kf_ref, gf_ref, kb_ref, gb_ref, vi_ref, vit_ref, og_ref, mq_ref, h_scr):
    tm = x_ref.shape[0]
    h_scr[...] = _rms(x_ref[...], gmix_ref[...]).astype(BF16)
    h = h_scr[...]
    cos = cos_ref[...]
    sin = sin_ref[...]
    lane = lax.broadcasted_iota(I32, (tm, HEAD_DIM), 1)
    first_half = (lane % (2 * ROPE_PAIRS)) < ROPE_PAIRS
    scale = HEAD_DIM ** -0.5

    def proj(c0, c1):
        return _nn(h, w_ref[:, c0:c1])

    for c in range(ATTN_KV_HEADS):
        a = proj(C_Q + QG_WIDTH * c, C_Q + QG_WIDTH * (c + 1))
        for hh in range(KV_GROUP):
            sl = slice(hh * HEAD_DIM, (hh + 1) * HEAD_DIM)
            y = _norm_rope(a[:, sl], gq_ref[...], cos, sin, first_half) * (scale * LOG2E)
            q_ref[:, QG_WIDTH * c + hh * HEAD_DIM:QG_WIDTH * c + (hh + 1) * HEAD_DIM] = y.astype(BF16)
    a = proj(C_K, C_V)
    for hh in range(ATTN_KV_HEADS):
        sl = slice(hh * HEAD_DIM, (hh + 1) * HEAD_DIM)
        k_ref[:, sl] = _norm_rope(a[:, sl], gk_ref[...], cos, sin, first_half).astype(BF16)
    vt_ref[...] = _nt(wvt_ref[...], h).astype(BF16)

    hq_ref[...] = jax.nn.silu(proj(C_HQ, C_ZF)).astype(BF16)

    def forget(z, lb_ref, k_out, g_out):
        a2 = lb_ref[...]
        e2 = jnp.exp(a2 - jnp.max(a2, axis=0, keepdims=True))
        lb = e2[0:1, :] / jnp.sum(e2, axis=0, keepdims=True)
        f = lb + (1.0 - lb) * jax.nn.sigmoid(z)
        g_out[...] = jnp.log(f) * LOG2E
        k_out[...] = ((1.0 - lb) * jax.nn.sigmoid(-z)).astype(BF16)

    forget(proj(C_ZF, C_ZB), lbf_ref, kf_ref, gf_ref)
    forget(proj(C_ZB, C_HI), lbb_ref, kb_ref, gb_ref)
    vi_ref[...] = proj(C_HI, C_HG).astype(BF16)
    vit_ref[...] = _nt(wvit_ref[...], h).astype(BF16)
    og_ref[...] = jax.nn.silu(proj(C_HG, C_MQ)).astype(BF16)
    a = proj(C_MQ, C_END)
    for hh in range(MEM_HEADS):
        sl = slice(hh * HEAD_DIM, (hh + 1) * HEAD_DIM)
        mq_ref[:, sl] = (_rms(a[:, sl], gmq_ref[...]) * scale).astype(BF16)


def _in_proj(x, g_mix, w_bf, wvt, wvit, cos, sin_signed, g_q, g_k, g_mq, lb_f, lb_b, tm):
    n = x.shape[0]
    grid = (n // tm,)
    row = lambda w: pl.BlockSpec((tm, w), lambda i: (i, 0))
    col = lambda h: pl.BlockSpec((h, tm), lambda i: (0, i))
    full = lambda a: pl.BlockSpec(a.shape, lambda i: (0,) * a.ndim)
    res = lambda a: pl.BlockSpec(a.shape, lambda i: (0,) * a.ndim, pipeline_mode=pl.Buffered(1))
    sd = jax.ShapeDtypeStruct
    hw = HG_WIDTH
    out_shape = (sd((n, Q_WIDTH), BF16), sd((n, KV_WIDTH), BF16), sd((KV_WIDTH, n), BF16),
                 sd((n, hw), BF16), sd((n, hw), BF16), sd((n, hw), F32), sd((n, hw), BF16), sd((n, hw), F32),
                 sd((n, hw), BF16), sd((hw, n), BF16), sd((n, hw), BF16), sd((n, MEM_WIDTH), BF16))
    out_specs = (row(Q_WIDTH), row(KV_WIDTH), col(KV_WIDTH), row(hw), row(hw), row(hw), row(hw), row(hw),
                 row(hw), col(hw), row(hw), row(MEM_WIDTH))
    return pl.pallas_call(
        _inproj_kernel,
        grid=grid,
        in_specs=[row(D_MODEL), full(g_mix), res(w_bf), res(wvt), res(wvit), row(HEAD_DIM), row(HEAD_DIM),
                  full(g_q), full(g_k), full(g_mq), full(lb_f), full(lb_b)],
        out_specs=out_specs,
        out_shape=out_shape,
        scratch_shapes=[pltpu.VMEM((tm, D_MODEL), BF16)],
        name="in_proj",
        compiler_params=_cparams(("parallel",)),
    )(x, g_mix, w_bf, wvt, wvit, cos, sin_signed, g_q, g_k, g_mq, lb_f, lb_b)


ONES_ROWS = 16


def _attn_kernel(q_ref, k_ref, vt_ref, o_ref, acc_ref, s_ref, *, tk, nk):
    tq = q_ref.shape[0]
    acc_ref[...] = jnp.zeros_like(acc_ref)
    ones = jnp.ones((ONES_ROWS, tk), BF16)

    def scores(hh, off):
        s_ref[hh] = _nt(k_ref[pl.ds(off, tk), :], q_ref[:, hh * HEAD_DIM:(hh + 1) * HEAD_DIM])

    lead = 2
    for hh in range(lead):
        scores(hh, 0)

    def body(i, ms):
        off = pl.multiple_of(i * tk, tk)
        off_next = pl.multiple_of(jnp.minimum(i + 1, nk - 1) * tk, tk)
        vx = jnp.concatenate([vt_ref[:, pl.ds(off, tk)], ones], axis=0)
        out = []
        for hh in range(KV_GROUP):
            nxt = hh + lead
            scores(nxt % KV_GROUP, off if nxt < KV_GROUP else off_next)
            s = s_ref[hh]
            m_new = jnp.maximum(ms[hh], jnp.max(s, axis=0, keepdims=True))
            p = jnp.exp2((s - m_new).astype(BF16))
            alpha = jnp.exp2(ms[hh] - m_new)
            acc_ref[hh] = alpha * acc_ref[hh] + _nn(vx, p)
            out.append(m_new)
        return tuple(out)

    init = tuple(jnp.full((1, tq), NEG_BIG, F32) for _ in range(KV_GROUP))
    lax.fori_loop(0, nk, body, init, unroll=max(1, min(8, nk // 2)))
    for hh in range(KV_GROUP):
        a = acc_ref[hh]
        o_ref[:, hh * HEAD_DIM:(hh + 1) * HEAD_DIM] = (a[0:HEAD_DIM] / a[HEAD_DIM:HEAD_DIM + 1]).T.astype(BF16)


def _attention(q, k, vt, tq, tk):
    s = q.shape[0]
    width = KV_GROUP * HEAD_DIM
    return pl.pallas_call(
        functools.partial(_attn_kernel, tk=tk, nk=s // tk),
        grid=(ATTN_KV_HEADS, s // tq),
        in_specs=[pl.BlockSpec((tq, width), lambda g, i: (i, g)),
                  pl.BlockSpec((s, HEAD_DIM), lambda g, i: (0, g)),
                  pl.BlockSpec((HEAD_DIM, s), lambda g, i: (g, 0))],
        out_specs=pl.BlockSpec((tq, width), lambda g, i: (i, g)),
        out_shape=jax.ShapeDtypeStruct((s, ATTN_HEADS * HEAD_DIM), BF16),
        scratch_shapes=[pltpu.VMEM((KV_GROUP, HEAD_DIM + ONES_ROWS, tq), F32), pltpu.VMEM((KV_GROUP, tk, tq), F32)],
        name="attention",
        compiler_params=_cparams(("parallel", "parallel")),
    )(q, k, vt)


def _hgrn_levels():
    b, out = HG_CHUNK, []
    while b >= 2:
        out.append(b)
        b //= 2
    return out


def _hgrn_pair_masks(rev):
    c = HG_CHUNK
    ti = lax.broadcasted_iota(I32, (c, c), 0)
    si = lax.broadcasted_iota(I32, (c, c), 1)
    row = lax.broadcasted_iota(I32, (c, HG_WIDTH), 0)
    pairs, signs = [], []
    for blk in _hgrn_levels():
        half = blk // 2
        same = (ti // blk) == (si // blk)
        if rev:
            m = same & (ti % blk < half) & (si % blk >= half)
            is_q = row % blk < half
        else:
            m = same & (ti % blk >= half) & (si % blk < half)
            is_q = row % blk >= half
        pairs.append(jnp.where(m, 1.0, 0.0))
        signs.append(jnp.where(is_q, 1.0, -1.0))
    return jnp.where(ti == si, 1.0, 0.0), pairs, signs


def _hgrn_chunk(q, k, g, v, vt, state_ref, rev, masks):
    c = HG_CHUNK
    w = HG_WIDTH
    row = lax.broadcasted_iota(I32, (c, w), 0)
    b = g
    sh = 1
    while sh < c:
        if rev:
            b = b + jnp.where(row < c - sh, pltpu.roll(b, c - sh, 0), 0.0)
        else:
            b = b + jnp.where(row >= sh, pltpu.roll(b, sh, 0), 0.0)
        sh *= 2
    qf = q.astype(F32)
    kf = k.astype(F32)
    eye, pair_masks, signs = masks
    att = [eye * _nt(q[:, h * HEAD_DIM:(h + 1) * HEAD_DIM], k[:, h * HEAD_DIM:(h + 1) * HEAD_DIM])
           for h in range(HG_HEADS)]
    for blk, pair, sign in zip(_hgrn_levels(), pair_masks, signs):
        half = blk // 2
        ref_row = half if rev else half - 1
        pos = row % blk
        if blk >= 8:
            r = jnp.concatenate(
                [jnp.broadcast_to(b[s0 + ref_row:s0 + ref_row + 1, :], (blk, w)) for s0 in range(0, c, blk)], axis=0)
        else:
            r = b
            for d in range(-ref_row, blk - ref_row):
                if d != 0:
                    r = jnp.where(pos - ref_row == d, pltpu.roll(b, d % c, 0), r)
        e = jnp.exp2((b - r) * sign)
        is_q = (pos < half) if rev else (pos >= half)
        x = (jnp.where(is_q, qf, kf) * e).astype(BF16)
        for h in range(HG_HEADS):
            sl = slice(h * HEAD_DIM, (h + 1) * HEAD_DIM)
            att[h] = att[h] + pair * _nt(x[:, sl], x[:, sl])
    b_end = b[0:1, :] if rev else b[c - 1:c, :]
    qd = (qf * jnp.exp2(b)).astype(BF16)
    kd = (kf * jnp.exp2(b_end - b)).astype(BF16)
    dec = jnp.exp2(b_end)
    outs = []
    for h in range(HG_HEADS):
        sl = slice(h * HEAD_DIM, (h + 1) * HEAD_DIM)
        st = state_ref[h]
        o = _nn(att[h].astype(BF16), v[:, sl]) + _nt(qd[:, sl], st.astype(BF16))
        state_ref[h] = dec[:, sl] * st + _nn(vt[sl, :], kd[:, sl])
        outs.append(o)
    return jnp.concatenate(outs, axis=1)


def _hgrn_fwd_kernel(q_ref, k_ref, g_ref, v_ref, vt_ref, o_ref, state_ref, *, nch):
    @pl.when(pl.program_id(0) == 0)
    def _():
        state_ref[...] = jnp.zeros_like(state_ref)

    masks = _hgrn_pair_masks(False)

    def body(i, carry):
        r0 = pl.multiple_of(i * HG_CHUNK, HG_CHUNK)
        rs = pl.ds(r0, HG_CHUNK)
        o_ref[rs, :] = _hgrn_chunk(q_ref[rs, :], k_ref[rs, :], g_ref[rs, :], v_ref[rs, :], vt_ref[:, rs], state_ref, False,
                                   masks)
        return carry

    lax.fori_loop(0, nch, body, 0)


def _hgrn_bwd_kernel(q_ref, k_ref, g_ref, v_ref, vt_ref, of_ref, og_ref, gout_ref, o_ref, state_ref, *, nch):
    @pl.when(pl.program_id(0) == 0)
    def _():
        state_ref[...] = jnp.zeros_like(state_ref)

    masks = _hgrn_pair_masks(True)

    def body(i, carry):
        r0 = pl.multiple_of((nch - 1 - i) * HG_CHUNK, HG_CHUNK)
        rs = pl.ds(r0, HG_CHUNK)
        o = _hgrn_chunk(q_ref[rs, :], k_ref[rs, :], g_ref[rs, :], v_ref[rs, :], vt_ref[:, rs], state_ref, True, masks)
        o = o + of_ref[rs, :]
        og = og_ref[rs, :].astype(F32)
        for h in range(HG_HEADS):
            sl = slice(h * HEAD_DIM, (h + 1) * HEAD_DIM)
            o_ref[rs, sl] = (_rms(o[:, sl], gout_ref[...]) * og[:, sl]).astype(BF16)
        return carry

    lax.fori_loop(0, nch, body, 0)


def _hgrn(hq, kf, gf, kb, gb, vi, vit, og, g_out, tb):
    n = hq.shape[0]
    nblk = n // tb
    nch = tb // HG_CHUNK
    state = pltpu.VMEM((HG_HEADS, HEAD_DIM, HEAD_DIM), F32)
    fr = lambda i: (i, 0)
    fc = lambda i: (0, i)
    o_f = pl.pallas_call(
        functools.partial(_hgrn_fwd_kernel, nch=nch),
        grid=(nblk,),
        in_specs=[pl.BlockSpec((tb, HG_WIDTH), fr)] * 4 + [pl.BlockSpec((HG_WIDTH, tb), fc)],
        out_specs=pl.BlockSpec((tb, HG_WIDTH), fr),
        out_shape=jax.ShapeDtypeStruct((n, HG_WIDTH), F32),
        scratch_shapes=[state],
        name="hgrn_fwd",
        compiler_params=_cparams(("arbitrary",)),
    )(hq, kf, gf, vi, vit)
    br = lambda i: (nblk - 1 - i, 0)
    bc = lambda i: (0, nblk - 1 - i)
    return pl.pallas_call(
        functools.partial(_hgrn_bwd_kernel, nch=nch),
        grid=(nblk,),
        in_specs=[pl.BlockSpec((tb, HG_WIDTH), br)] * 4 + [pl.BlockSpec((HG_WIDTH, tb), bc)]
        + [pl.BlockSpec((tb, HG_WIDTH), br)] * 2 + [pl.BlockSpec(g_out.shape, lambda i: (0, 0))],
        out_specs=pl.BlockSpec((tb, HG_WIDTH), br),
        out_shape=jax.ShapeDtypeStruct((n, HG_WIDTH), BF16),
        scratch_shapes=[state],
        name="hgrn_bwd",
        compiler_params=_cparams(("arbitrary",)),
    )(hq, kb, gb, vi, vit, o_f, og, g_out)


def _outproj_kernel(x_ref, attn_ref, hgrn_ref, mq_ref, km_ref, vm_ref, w_ref, gffn_ref, wrt_ref,
                    y_ref, h2_ref, aff_ref, mix_ref):
    n_attn = ATTN_HEADS * HEAD_DIM
    mix_ref[:, 0:n_attn] = attn_ref[...]
    mix_ref[:, n_attn:n_attn + HG_WIDTH] = hgrn_ref[...]
    for hh in range(MEM_HEADS):
        sl = slice(hh * HEAD_DIM, (hh + 1) * HEAD_DIM)
        s = _nt(mq_ref[:, sl], km_ref[:, sl])
        p = jnp.exp(s - jnp.max(s, axis=-1, keepdims=True))
        p = p / jnp.sum(p, axis=-1, keepdims=True)
        c0 = n_attn + HG_WIDTH + hh * HEAD_DIM
        mix_ref[:, c0:c0 + HEAD_DIM] = _nn(p.astype(BF16), vm_ref[:, sl]).astype(BF16)
    y = x_ref[...] + _nn(mix_ref[...], w_ref[...])
    y_ref[...] = y
    h2 = _rms(y, gffn_ref[...])
    half = D_MODEL // 2
    lo = lax.bitcast_convert_type(h2[:, :half].astype(BF16).astype(F32), U32) >> 16
    hi = lax.bitcast_convert_type(h2[:, half:].astype(BF16).astype(F32), U32) & jnp.uint32(0xFFFF0000)
    word = hi | lo
    tm = word.shape[0]
    for k in range(half // LANES):
        h2_ref[pl.ds(k, tm, stride=half // LANES), :] = word[:, k * LANES:(k + 1) * LANES]
    h_hi = h2.astype(BF16)
    h_lo = (h2 - h_hi.astype(F32)).astype(BF16)
    wr = wrt_ref[...]
    w_hi = wr.astype(BF16)
    w_mid = (wr - w_hi.astype(F32)).astype(BF16)
    two = _nt(jnp.concatenate([w_hi, w_mid], axis=0), h_hi)
    logits = two[:N_EXPERTS] + two[N_EXPERTS:] + _nt(w_hi, h_lo)
    e = jnp.exp(logits - jnp.max(logits, axis=0, keepdims=True))
    aff_ref[...] = e / jnp.sum(e, axis=0, keepdims=True)


def _out_proj(x, attn, hgrn, mq, km, vm, w_out_bf, g_ffn, w_router_t, tm):
    n = x.shape[0]
    row = lambda w: pl.BlockSpec((tm, w), lambda i: (i, 0))
    full = lambda a: pl.BlockSpec(a.shape, lambda i: (0,) * a.ndim)
    res = lambda a: pl.BlockSpec(a.shape, lambda i: (0,) * a.ndim, pipeline_mode=pl.Buffered(1))
    sd = jax.ShapeDtypeStruct
    return pl.pallas_call(
        _outproj_kernel,
        grid=(n // tm,),
        in_specs=[row(D_MODEL), row(Q_WIDTH), row(HG_WIDTH), row(MEM_WIDTH), full(km), full(vm), res(w_out_bf), full(g_ffn),
                  full(w_router_t)],
        out_specs=(row(D_MODEL), pl.BlockSpec((tm * TOKEN_TILE_ROWS, LANES), lambda i: (i, 0)),
                   pl.BlockSpec((N_EXPERTS, tm), lambda i: (0, i))),
        out_shape=(sd((n, D_MODEL), F32), sd((n * TOKEN_TILE_ROWS, LANES), U32), sd((N_EXPERTS, n), F32)),
        scratch_shapes=[pltpu.VMEM((tm, D_MODEL), BF16)],
        name="out_proj",
        compiler_params=_cparams(("parallel",)),
    )(x, attn, hgrn, mq, km, vm, w_out_bf, g_ffn, w_router_t)


def _split3(x):
    hi = x.astype(BF16)
    r1 = x - hi.astype(F32)
    mid = r1.astype(BF16)
    lo = (r1 - mid.astype(F32)).astype(BF16)
    return hi, mid, lo


def _topk_kernel(aff_ref, afft_ref, idx_ref, gate_ref, pos_ref, off_ref, cl_scr, offb_scr, totb_scr, *, cap):
    e_n, nb, _ = aff_ref.shape
    rows = e_n * nb
    aff = aff_ref[...]
    keys = lax.bitcast_convert_type(aff, I32)

    def count(mask):
        s = jnp.sum(mask.astype(F32), axis=1, keepdims=True)
        return jnp.sum(s, axis=2, keepdims=True)

    def bis(i, t):
        cand = t | (jnp.int32(1) << (30 - i))
        return jnp.where(count(keys >= cand) >= cap, cand, t)

    thr = lax.fori_loop(0, 31, bis, jnp.zeros((e_n, 1, 1), I32))
    gt = keys > thr
    eq = keys == thr
    need = cap - count(gt)

    upper = _ind(lax.broadcasted_iota(I32, (LANES, LANES), 0) <= lax.broadcasted_iota(I32, (LANES, LANES), 1))
    ones = jnp.ones((LANES, LANES), BF16)
    bi = lax.broadcasted_iota(I32, (nb, nb), 0)
    bj = lax.broadcasted_iota(I32, (nb, nb), 1)
    strict_lower = _ind(bj < bi)

    def prefix(mask):
        m2 = _ind(mask).reshape(rows, LANES)
        cl = _nn(m2, upper).reshape(e_n, nb, LANES)
        tot = _nn(m2, ones).reshape(e_n, nb, LANES)
        off = jnp.stack([_nn(strict_lower, tot[e].astype(BF16)) for e in range(e_n)], axis=0)
        return cl, off, tot

    cl, off, _ = prefix(eq)
    rank_eq = off + cl - eq.astype(F32)
    sel = gt | (eq & (rank_eq < need))
    cl, off, tot = prefix(sel)
    pos_ref[...] = jnp.where(sel, off + cl - 1.0, -1.0).astype(I32)
    off_ref[...] = off.astype(I32)
    cl_scr[...] = cl
    offb_scr[...] = off
    totb_scr[...] = tot

    lower_incl = _ind(lax.broadcasted_iota(I32, (LANES, LANES), 1) <= lax.broadcasted_iota(I32, (LANES, LANES), 0))
    reps = cap // LANES
    s_row = lax.broadcasted_iota(I32, (nb, cap), 1).astype(F32)
    b_col = lax.broadcasted_iota(I32, (nb, cap), 0).astype(F32)
    j_col = lax.broadcasted_iota(I32, (LANES, cap), 0).astype(F32)

    def per_expert(e, carry):
        sel_e = _ind(pos_ref[e] >= 0)
        clt = _nt(lower_incl, sel_e)
        offt = jnp.concatenate([offb_scr[e]] * reps, axis=1)
        endt = offt + jnp.concatenate([totb_scr[e]] * reps, axis=1)
        hit = (offt <= s_row) & (s_row < endt)
        onehot = _ind(hit)
        g_cnt = _nn(clt.astype(BF16), onehot)
        local = s_row[0:1, :] - jnp.sum(jnp.where(hit, offt, 0.0), axis=0, keepdims=True)
        j_row = jnp.sum((g_cnt <= local).astype(F32), axis=0, keepdims=True)
        b_row = jnp.sum(jnp.where(hit, b_col, 0.0), axis=0, keepdims=True)
        idx_ref[e] = (b_row * LANES + j_row).astype(I32)
        hi, mid, lo = _split3(afft_ref[e])
        g_aff = _nn(hi, onehot) + _nn(mid, onehot) + _nn(lo, onehot)
        gate_ref[e] = jnp.sum(jnp.where(j_col == j_row, g_aff, 0.0), axis=0, keepdims=True)
        return carry

    lax.fori_loop(0, e_n, per_expert, 0)


def _topk(aff_t, cap):
    e_n, n = aff_t.shape
    nb = n // LANES
    aff3 = aff_t.reshape(e_n, nb, LANES)
    afft3 = jnp.swapaxes(aff3, 1, 2)
    sd = jax.ShapeDtypeStruct
    return pl.pallas_call(
        functools.partial(_topk_kernel, cap=cap),
        out_shape=(sd((e_n, 1, cap), I32), sd((e_n, 1, cap), F32), sd((e_n, nb, LANES), I32), sd((e_n, nb, LANES), I32)),
        scratch_shapes=[pltpu.VMEM((e_n, nb, LANES), F32)] * 3,
        name="topk",
        compiler_params=_cparams(None),
    )(aff3, afft3)


GATHER_RING = 512


GATHER_ROWS = 2048


def _gather_kernel(idx_ref, src_ref, out_ref, sem):
    rows = out_ref.shape[0]
    ring = sem.shape[0]

    def copy(s, k):
        return pltpu.make_async_copy(src_ref.at[idx_ref[0, 0, s]], out_ref.at[s], sem.at[k])

    def group(gi, carry):
        for k in range(ring):
            s = gi * ring + k

            @pl.when(gi > 0)
            def _():
                copy(s - ring, k).wait()

            copy(s, k).start(priority=k % 2)
        return carry

    lax.fori_loop(0, rows // ring, group, 0)
    for k in range(ring):
        copy(rows - ring + k, k).wait()


def _gather(h2w, idx, cap):
    total = N_EXPERTS * cap
    rows = min(GATHER_ROWS, cap)
    ring = min(GATHER_RING, rows)
    assert total % rows == 0 and rows % ring == 0
    n = h2w.shape[0] // TOKEN_TILE_ROWS
    out = pl.pallas_call(
        _gather_kernel,
        grid=(total // rows,),
        in_specs=[pl.BlockSpec((1, 1, rows), lambda i: (i, 0, 0), memory_space=pltpu.SMEM),
                  pl.BlockSpec(memory_space=pl.ANY)],
        out_specs=pl.BlockSpec((rows, TOKEN_TILE_ROWS, LANES), lambda i: (i, 0, 0)),
        out_shape=jax.ShapeDtypeStruct((total, TOKEN_TILE_ROWS, LANES), U32),
        scratch_shapes=[pltpu.SemaphoreType.DMA((ring,))],
        name="gather",
        compiler_params=_cparams(("arbitrary",)),
    )(idx.reshape(total // rows, 1, rows), h2w.reshape(n, TOKEN_TILE_ROWS, LANES))
    return out.reshape(N_EXPERTS, cap * TOKEN_TILE_ROWS, LANES)


def _ffn_kernel(x_ref, gate_ref, wg_ref, wu_ref, wd_ref, o_ref, xb_ref, hid_ref, *, nf, tf):
    j = pl.program_id(1)
    t = pl.program_id(2)

    @pl.when(j == 0)
    def _():
        _, ts, d = xb_ref.shape
        for k in range(TOKEN_TILE_ROWS):
            w = x_ref[0, pl.ds(k, ts, stride=TOKEN_TILE_ROWS), :]
            xb_ref[t, :, k * LANES:(k + 1) * LANES] = lax.bitcast_convert_type(w << 16, F32).astype(BF16)
            xb_ref[t, :, d // 2 + k * LANES:d // 2 + (k + 1) * LANES] = lax.bitcast_convert_type(
                w & jnp.uint32(0xFFFF0000), F32).astype(BF16)

    @pl.when(j < nf)
    def _():
        x = xb_ref[t]
        hid = jax.nn.silu(_nn(x, wg_ref[0].astype(BF16))) * _nn(x, wu_ref[0].astype(BF16))
        hid_ref[t, :, pl.ds(pl.multiple_of(j * tf, tf), tf)] = hid.astype(BF16)

    @pl.when(j >= nf)
    def _():
        o_ref[0] = (_nn(hid_ref[t], wd_ref[0].astype(BF16)) * gate_ref[0]).astype(BF16)


def _ffn(xs, gate_col, wg, wu, wd, ts, tf, tn):
    e_n, cap = gate_col.shape[:2]
    d = D_MODEL
    nf = D_EXPERT // tf
    nt = cap // ts
    x_map = lambda e, j, t: (e, jnp.where(j == 0, t, nt - 1), 0)
    o_map = lambda e, j, t: (e, jnp.where(j < nf, 0, t), jnp.maximum(j - nf, 0))
    return pl.pallas_call(
        functools.partial(_ffn_kernel, nf=nf, tf=tf),
        grid=(e_n, nf + d // tn, nt),
        in_specs=[pl.BlockSpec((1, ts * TOKEN_TILE_ROWS, LANES), x_map),
                  pl.BlockSpec((1, ts, 1), lambda e, j, t: (e, jnp.where(j < nf, 0, t), 0)),
                  pl.BlockSpec((1, d, tf), lambda e, j, t: (e, 0, jnp.minimum(j, nf - 1))),
                  pl.BlockSpec((1, d, tf), lambda e, j, t: (e, 0, jnp.minimum(j, nf - 1))),
                  pl.BlockSpec((1, D_EXPERT, tn), lambda e, j, t: (e, 0, jnp.maximum(j - nf, 0)))],
        out_specs=pl.BlockSpec((1, ts, tn), o_map),
        out_shape=jax.ShapeDtypeStruct((e_n, cap, d), BF16),
        scratch_shapes=[pltpu.VMEM((nt, ts, d), BF16), pltpu.VMEM((nt, ts, D_EXPERT), BF16)],
        name="ffn",
        compiler_params=_cparams(("parallel", "arbitrary", "arbitrary")),
    )(xs, gate_col, wg, wu, wd)


ROW_GROUP = 16
WIN_GROUPS = 3
WIN = WIN_GROUPS * ROW_GROUP
FAR_GROUPS = LANES // ROW_GROUP + 1 - WIN_GROUPS
FAR = FAR_GROUPS * ROW_GROUP


def _combine_kernel(off_ref, tot_ref, y1_ref, post_ref, rows_ref, o_ref, buf, far, sem, far_sem, *, cap):
    b = pl.program_id(0)
    nb = pl.num_programs(0)
    n_groups = cap // ROW_GROUP

    def first_group(blk, e):
        return jnp.minimum(off_ref[e, blk] // ROW_GROUP, n_groups - WIN_GROUPS)

    def copy(blk, slot, e):
        return pltpu.make_async_copy(rows_ref.at[e, pl.ds(first_group(blk, e), WIN_GROUPS)],
                                     buf.at[slot, pl.ds(e * WIN_GROUPS, WIN_GROUPS)], sem.at[slot, e])

    def start(blk, slot):
        for e in range(N_EXPERTS):
            copy(blk, slot, e).start()

    def wait(blk, slot):
        for e in range(N_EXPERTS):
            copy(blk, slot, e).wait()

    slot = b % 2

    @pl.when(b == 0)
    def _():
        far[...] = jnp.zeros_like(far)
        start(0, 0)

    @pl.when(b + 1 < nb)
    def _():
        start(b + 1, 1 - slot)

    post = post_ref[...]
    lane = lax.broadcasted_iota(I32, (LANES, LANES), 1)

    def token_row(e):
        return jnp.broadcast_to(post[:, e:e + 1], (LANES, LANES))

    pieces = []
    for t in range(N_EXPERTS * WIN // LANES):
        miss = jnp.ones((LANES, LANES), I32)
        for e in range(t * LANES // WIN, min(N_EXPERTS, ((t + 1) * LANES - 1) // WIN + 1)):
            j = lane + (t * LANES - e * WIN)
            miss = jnp.where((j >= 0) & (j < WIN), token_row(e) - first_group(b, e) * ROW_GROUP - j, miss)
        pieces.append(_ind(miss == 0))
    w = jnp.concatenate(pieces, axis=1)
    wait(b, slot)
    o_ref[...] = y1_ref[...] + _nn(w, buf[slot].reshape(N_EXPERTS * WIN, o_ref.shape[1]))

    for e in range(N_EXPERTS):
        covered = (first_group(b, e) + WIN_GROUPS) * ROW_GROUP

        @pl.when(off_ref[e, b] + tot_ref[e, b] > covered)
        def _():
            g0 = jnp.minimum(first_group(b, e) + WIN_GROUPS, n_groups - FAR_GROUPS)
            cp = pltpu.make_async_copy(rows_ref.at[e, pl.ds(g0, FAR_GROUPS)], far.at[pl.ds(0, FAR_GROUPS)], far_sem)
            cp.start()
            cp.wait()
            row = token_row(e)
            hit = (row - g0 * ROW_GROUP == lane) & (row >= covered) & (lane < FAR)
            o_ref[...] += _nn(_ind(hit), far[...].reshape(LANES, o_ref.shape[1]))


def _combine(y1, pos_t, off, tot, rows, cap):
    n = y1.shape[0]
    nb = n // LANES
    d = y1.shape[1]
    grid_spec = pltpu.PrefetchScalarGridSpec(
        num_scalar_prefetch=2,
        grid=(nb,),
        in_specs=[pl.BlockSpec((LANES, d), lambda b, off, tot: (b, 0)),
                  pl.BlockSpec((LANES, N_EXPERTS), lambda b, off, tot: (b, 0)),
                  pl.BlockSpec(memory_space=pl.ANY)],
        out_specs=pl.BlockSpec((LANES, d), lambda b, off, tot: (b, 0)),
        scratch_shapes=[pltpu.VMEM((2, N_EXPERTS * WIN_GROUPS, ROW_GROUP, d), BF16),
                        pltpu.VMEM((LANES // ROW_GROUP, ROW_GROUP, d), BF16),
                        pltpu.SemaphoreType.DMA((2, N_EXPERTS)), pltpu.SemaphoreType.DMA(())],
    )
    assert cap % ROW_GROUP == 0 and cap >= WIN + FAR and (N_EXPERTS * WIN) % LANES == 0
    rows = rows.reshape(N_EXPERTS, cap // ROW_GROUP, ROW_GROUP, d)
    return pl.pallas_call(
        functools.partial(_combine_kernel, cap=cap),
        grid_spec=grid_spec,
        out_shape=jax.ShapeDtypeStruct((n, d), F32),
        name="combine",
        compiler_params=_cparams(("arbitrary",)),
    )(off, tot, y1, pos_t, rows)


def _rope_tables(seq_len):
    t = jnp.arange(seq_len)
    row = (t // GRID_W).astype(F32)
    col = (t % GRID_W).astype(F32)
    inv = ROPE_THETA ** (-jnp.arange(ROPE_PAIRS, dtype=F32) / ROPE_PAIRS)
    ang = jnp.stack([row[:, None] * inv, col[:, None] * inv], axis=1)
    ang = jnp.broadcast_to(ang[:, :, None, :], (seq_len, 2, 2, ROPE_PAIRS)).reshape(seq_len, HEAD_DIM)
    first_half = (jnp.arange(HEAD_DIM) % (2 * ROPE_PAIRS)) < ROPE_PAIRS
    return jnp.cos(ang), jnp.where(first_half[None, :], -jnp.sin(ang), jnp.sin(ang))


def _pick(n, pref):
    t = min(n, pref)
    assert n % t == 0, (n, pref)
    return t


def _layer(x, mem, p):
    n = x.shape[0]
    cos, sin_signed = p["rope"]
    km, vm = _mem_kv(mem, p["g_mem"], p["w_mem_kv"], p["g_mk"])
    (q, k, vt, hq, kf, gf, kb, gb, vi, vit, og, mq) = _in_proj(
        x, p["g_mix"], p["w_in"], p["wvt"], p["wvit"], cos, sin_signed, p["g_q"], p["g_k"], p["g_mq"],
        p["lb_fwd"], p["lb_bwd"], _pick(n, 512))
    attn = _attention(q, k, vt, _pick(n, 256), _pick(n, 1024))
    hgrn = _hgrn(hq, kf, gf, kb, gb, vi, vit, og, p["g_hg_out"], _pick(n, 512))
    y1, h2, aff_t = _out_proj(x, attn, hgrn, mq, km, vm, p["w_out"], p["g_ffn"], p["w_router_t"], _pick(n, 512))
    cap = EXPERT_CAPACITY_FACTOR * n // N_EXPERTS
    idx, gate, pos, off = _topk(aff_t, cap)
    xs = _gather(h2, idx, cap)
    gate_col = gate.reshape(N_EXPERTS, cap, 1)
    rows = _ffn(xs, gate_col, p["w_gate"], p["w_up"], p["w_down"], _pick(cap, 1024), 256, 512)
    pos_t = pos.reshape(N_EXPERTS, n).T
    off_s = off[:, :, 0]
    tot_s = jnp.diff(off_s, axis=1, append=jnp.full((N_EXPERTS, 1), cap, I32))
    return _combine(y1, pos_t, off_s, tot_s, rows, cap)


def kernel(x_prompt, x_sample, mem_prompt, mem_sample, g_mix, w_in, g_q, g_k, g_hg_out, lb_fwd, lb_bwd, g_mem, w_mem_kv,
           g_mq, g_mk, w_out, g_ffn, w_router, w_gate, w_up, w_down):
    assert g_mix.shape[0] == 1 and lb_fwd.shape[0] == 2, "single layer: lower bound is the first cumulative-softmax row"
    w_in_bf = w_in[0].astype(BF16)
    wv, wvi = lax.optimization_barrier((w_in[0, :, C_V:C_HQ], w_in[0, :, C_HI:C_HG]))
    p = {
        "g_mix": g_mix, "w_in": w_in_bf,
        "wvt": wv.T.astype(BF16), "wvit": wvi.T.astype(BF16),
        "g_q": g_q, "g_k": g_k, "g_hg_out": g_hg_out, "lb_fwd": lb_fwd, "lb_bwd": lb_bwd,
        "g_mem": g_mem, "w_mem_kv": w_mem_kv[0].astype(BF16), "g_mq": g_mq, "g_mk": g_mk,
        "w_out": w_out[0].astype(BF16), "g_ffn": g_ffn, "w_router_t": w_router[0].T,
        "w_gate": w_gate[0], "w_up": w_up[0], "w_down": w_down[0],
    }
    p["rope"] = _rope_tables(max(x_prompt.shape[1], x_sample.shape[1]))
    y_prompt = _layer(x_prompt[0], mem_prompt[0], p)
    y_sample = _layer(x_sample[0], mem_sample[0], p)
    return (y_prompt[None], y_sample[None])
```
